```python
import math
import jax, jax.numpy as jnp
from jax import lax
import numpy as np

D_MODEL = 1024
BATCH = 8
SEQ = 2048
DEPTH = 1
DEC_BATCH = 128
DEC_SEQ = 4
PAST_LEN = 16384
PAGE_SIZE = 128

POOL_WINDOWS = (2, 4, 8, 16)
N_POOL_GROUPS = len(POOL_WINDOWS)
POOL_WIDTH = D_MODEL
POOL_GC = POOL_WIDTH // N_POOL_GROUPS
POOL_DG = D_MODEL // N_POOL_GROUPS
POOL_BUF = max(POOL_WINDOWS) - 1
GLA_HEADS = 4
GLA_DK = D_MODEL // 2 // GLA_HEADS
GLA_DV = D_MODEL // GLA_HEADS
QK_WIDTH = GLA_HEADS * GLA_DK
V_WIDTH = GLA_HEADS * GLA_DV
GK_RANK = 16
GATE_NORMALIZER = 16.0
GLA_CHUNK = 64
D_FF = 2816
PLE_DIM = 256
EPS = 1e-6
IN_SPLITS = (POOL_WIDTH, QK_WIDTH, QK_WIDTH, V_WIDTH, GK_RANK, V_WIDTH, D_MODEL, D_MODEL)
IN_WIDTH = sum(IN_SPLITS)

kernel_name = "hybrid_pool_gla_macaron_step"


def _rmsnorm(x, g):
    xf = x.astype(jnp.float32)
    y = xf * lax.rsqrt(jnp.mean(xf * xf, axis=-1, keepdims=True) + EPS)
    return (y * g.astype(jnp.float32)).astype(x.dtype)


def _swiglu(x, w_gate, w_up, w_down):
    return (jax.nn.silu(x @ w_gate) * (x @ w_up)) @ w_down


def _pool_mix(u, buf, pos0, pool_w, pool_scale):
    B, L, _ = u.shape
    ext = jnp.concatenate([buf.astype(jnp.float32), u.astype(jnp.float32)], axis=1)
    cs = jnp.concatenate([jnp.zeros((B, 1, POOL_WIDTH), jnp.float32), jnp.cumsum(ext, axis=1)], axis=1)
    end = cs[:, POOL_BUF + 1:]
    pos = pos0 + jnp.arange(L) + 1
    means = []
    for g, w in enumerate(POOL_WINDOWS):
        sl = slice(g * POOL_GC, (g + 1) * POOL_GC)
        start = cs[:, POOL_BUF + 1 - w: POOL_BUF + 1 - w + L, sl]
        cnt = jnp.minimum(w, pos).astype(jnp.float32)
        means.append((end[..., sl] - start) / cnt[None, :, None])
    pooled = jnp.concatenate(means, axis=-1) - u.astype(jnp.float32)
    pooled = pooled.reshape(B, L, N_POOL_GROUPS, POOL_GC).astype(u.dtype)
    out = jnp.einsum('blgc,gcd->blgd', pooled, pool_w).reshape(B, L, D_MODEL) * pool_scale
    new_buf = ext[:, -POOL_BUF:].astype(buf.dtype)
    return out, new_buf


def _gla(q, k, v, gk, S0):
    B, L, H, DK = q.shape
    DV = v.shape[-1]
    C = math.gcd(L, GLA_CHUNK)
    N = L // C

    def to_chunks(t):
        return t.astype(jnp.float32).reshape(B, N, C, H, t.shape[-1]).transpose(1, 0, 3, 2, 4)

    qc, kc, vc, gc = to_chunks(q), to_chunks(k), to_chunks(v), to_chunks(gk)
    mask = jnp.tril(jnp.ones((C, C), dtype=bool))

    def step(S, inp):
        qi, ki, vi, gi = inp
        b = jnp.cumsum(gi, axis=2)
        o_inter = jnp.einsum('bhic,bhcv->bhiv', qi * jnp.exp(b), S)
        diff = b[:, :, :, None, :] - b[:, :, None, :, :]
        decay = jnp.exp(jnp.where(mask[:, :, None], diff, -jnp.inf))
        A = jnp.einsum('bhic,bhjc,bhijc->bhij', qi, ki, decay)
        o = o_inter + jnp.einsum('bhij,bhjv->bhiv', A, vi)
        b_last = b[:, :, -1:, :]
        k_dec = ki * jnp.exp(b_last - b)
        S_new = S * jnp.exp(b_last[:, :, 0, :])[..., None] + jnp.einsum('bhjc,bhjv->bhcv', k_dec, vi)
        return S_new, o

    S, o = lax.scan(step, S0.astype(jnp.float32), (qc, kc, vc, gc))
    o = o.transpose(1, 0, 3, 2, 4).reshape(B, L, H, DV)
    return o, S


def _layer(x, p, buf, S0, pos0,
           ffn1_norm, ffn1_w_gate, ffn1_w_up, ffn1_w_down,
           mix_norm, w_in, w_gk_up, b_gk, gla_norm, pool_w, pool_scale, w_out,
           ffn2_norm, ffn2_w_gate, ffn2_w_up, ffn2_w_down,
           ple_norm, w_ple_gate, w_ple_proj):
    B, L, _ = x.shape
    h = x + 0.5 * _swiglu(_rmsnorm(x, ffn1_norm), ffn1_w_gate, ffn1_w_up, ffn1_w_down)
    xn = _rmsnorm(h, mix_norm)
    z = xn @ w_in
    offs = [int(o) for o in np.cumsum(IN_SPLITS)[:-1]]
    u, q, k, v, gk_lr, g, ga, gb = jnp.split(z, offs, axis=-1)
    a_out, new_buf = _pool_mix(u, buf, pos0, pool_w, pool_scale)
    gk = jax.nn.log_sigmoid((gk_lr @ w_gk_up + b_gk).astype(jnp.float32)) / GATE_NORMALIZER
    q = q.reshape(B, L, GLA_HEADS, GLA_DK) * (GLA_DK ** -0.5)
    k = k.reshape(B, L, GLA_HEADS, GLA_DK)
    v = v.reshape(B, L, GLA_HEADS, GLA_DV)
    gk = gk.reshape(B, L, GLA_HEADS, GLA_DK)
    o, S_new = _gla(q, k, v, gk, S0)
    o = _rmsnorm(o, gla_norm).reshape(B, L, V_WIDTH).astype(x.dtype)
    b_out = o * jax.nn.silu(g)
    mix = jax.nn.sigmoid(ga) * a_out + jax.nn.sigmoid(gb) * b_out
    h = h + mix @ w_out
    h = h + 0.5 * _swiglu(_rmsnorm(h, ffn2_norm), ffn2_w_gate, ffn2_w_up, ffn2_w_down)
    gate = jax.nn.sigmoid(_rmsnorm(h, ple_norm) @ w_ple_gate)
    h = h + gate * (p.astype(h.dtype) @ w_ple_proj)
    return h, new_buf, S_new.astype(S0.dtype)


def setup_inputs(seed: int = 0) -> dict:
    key = jax.random.key(seed)
    ks = jax.random.split(key, 32)
    f32 = jnp.float32

    def nrm(k, shape, scale=1.0):
        return jax.random.normal(k, shape, f32) * scale

    def gain(k, shape):
        return 1.0 + 0.05 * jax.random.normal(k, shape, f32)

    D, F = D_MODEL, D_FF
    return {
        "x_prompt": nrm(ks[0], (BATCH, SEQ, D)),
        "x_sample": nrm(ks[1], (DEC_BATCH, DEC_SEQ, D)),
        "p_prompt": nrm(ks[2], (DEPTH, BATCH, SEQ, PLE_DIM)),
        "p_sample": nrm(ks[3], (DEPTH, DEC_BATCH, DEC_SEQ, PLE_DIM)),
        "state_pool": nrm(ks[4], (DEPTH, DEC_BATCH, POOL_BUF, POOL_WIDTH)),
        "state_gla": nrm(ks[5], (DEPTH, DEC_BATCH, GLA_HEADS, GLA_DK, GLA_DV)),
        "ffn1_norm": gain(ks[6], (DEPTH, D)),
        "ffn1_w_gate": nrm(ks[7], (DEPTH, D, F), D ** -0.5),
        "ffn1_w_up": nrm(ks[8], (DEPTH, D, F), D ** -0.5),
        "ffn1_w_down": nrm(ks[9], (DEPTH, F, D), F ** -0.5),
        "mix_norm": gain(ks[10], (DEPTH, D)),
        "w_in": nrm(ks[11], (DEPTH, D, IN_WIDTH), D ** -0.5),
        "w_gk_up": nrm(ks[12], (DEPTH, GK_RANK, QK_WIDTH), GK_RANK ** -0.5),
        "b_gk": nrm(ks[13], (DEPTH, QK_WIDTH), 0.1),
        "gla_norm": gain(ks[14], (DEPTH, GLA_DV)),
        "pool_w": nrm(ks[15], (DEPTH, N_POOL_GROUPS, POOL_GC, POOL_DG), POOL_GC ** -0.5),
        "pool_scale": gain(ks[16], (DEPTH, D)),
        "w_out": nrm(ks[17], (DEPTH, D, D), D ** -0.5),
        "ffn2_norm": gain(ks[18], (DEPTH, D)),
        "ffn2_w_gate": nrm(ks[19], (DEPTH, D, F), D ** -0.5),
        "ffn2_w_up": nrm(ks[20], (DEPTH, D, F), D ** -0.5),
        "ffn2_w_down": nrm(ks[21], (DEPTH, F, D), F ** -0.5),
        "ple_norm": gain(ks[22], (DEPTH, D)),
        "w_ple_gate": nrm(ks[23], (DEPTH, D, D), D ** -0.5),
        "w_ple_proj": nrm(ks[24], (DEPTH, PLE_DIM, D), PLE_DIM ** -0.5),
        "final_norm": gain(ks[25], (D,)),
    }


def reference(x_prompt, x_sample, p_prompt, p_sample, state_pool, state_gla,
              ffn1_norm, ffn1_w_gate, ffn1_w_up, ffn1_w_down,
              mix_norm, w_in, w_gk_up, b_gk, gla_norm, pool_w, pool_scale, w_out,
              ffn2_norm, ffn2_w_gate, ffn2_w_up, ffn2_w_down,
              ple_norm, w_ple_gate, w_ple_proj, final_norm):
    B, L = x_prompt.shape[0], x_prompt.shape[1]
    hp, hs = x_prompt, x_sample
    pool_p, gla_p, pool_s, gla_s = [], [], [], []
    for i in range(DEPTH):
        lw = (ffn1_norm[i], ffn1_w_gate[i], ffn1_w_up[i], ffn1_w_down[i],
              mix_norm[i], w_in[i], w_gk_up[i], b_gk[i], gla_norm[i], pool_w[i], pool_scale[i], w_out[i],
              ffn2_norm[i], ffn2_w_gate[i], ffn2_w_up[i], ffn2_w_down[i],
              ple_norm[i], w_ple_gate[i], w_ple_proj[i])
        buf0 = jnp.zeros((B, POOL_BUF, POOL_WIDTH), state_pool.dtype)
        S0 = jnp.zeros((B, GLA_HEADS, GLA_DK, GLA_DV), state_gla.dtype)
        hp, bp, sp = _layer(hp, p_prompt[i], buf0, S0, 0, *lw)
        hs, bs, ss = _layer(hs, p_sample[i], state_pool[i], state_gla[i], PAST_LEN, *lw)
        pool_p.append(bp)
        gla_p.append(sp)
        pool_s.append(bs)
        gla_s.append(ss)
    y_prompt = _rmsnorm(hp, final_norm)
    y_sample = _rmsnorm(hs, final_norm)
    return (y_prompt, y_sample, jnp.stack(pool_p), jnp.stack(gla_p), jnp.stack(pool_s), jnp.stack(gla_s))
```

```python
import functools

import numpy as np
import jax
import jax.numpy as jnp
from jax import lax
from jax.experimental import pallas as pl
from jax.experimental.pallas import tpu as pltpu

POOL_WINDOWS = (2, 4, 8, 16)
POOL_BUF = max(POOL_WINDOWS) - 1
GLA_HEADS = 4
GATE_NORMALIZER = 16.0
EPS = 1e-6
PAST_LEN = 16384

ROW_TILE = 512
GLA_CHUNK = 64
HIST = 16
SAMPLE_PAD = 8
SAMPLE_BLOCK = 8
LANE = 128
VMEM_LIMIT = 56 * 1024 * 1024

F32 = jnp.float32
BF16 = jnp.bfloat16


def _dot(a, b):
    return jnp.dot(a, b, preferred_element_type=F32)


def _dot_nt(a, b):
    return lax.dot_general(a, b, (((1,), (1,)), ((), ())), preferred_element_type=F32)


def _dot_tn(a, b):
    return lax.dot_general(a, b, (((0,), (0,)), ((), ())), preferred_element_type=F32)


def _rms(x, w):
    ms = jnp.mean(x * x, axis=-1, keepdims=True)
    return x * lax.rsqrt(ms + EPS) * w


def _sigmoid(x):
    return 1.0 / (1.0 + jnp.exp(-x))


def _silu(x):
    return x * _sigmoid(x)


def _log_sigmoid(x):
    return jnp.minimum(x, 0.0) - jnp.log1p(jnp.exp(-jnp.abs(x)))


def _ff_chunks(f):
    step = 1024
    return [(lo, min(lo + step, f)) for lo in range(0, f, step)]


def _const_spec(shape):
    nd = len(shape)
    return pl.BlockSpec(shape, lambda *_: (0,) * nd, pipeline_mode=pl.Buffered(1))


def _swiglu_acc(xb, wg_ref, wu_ref, wd_ref):
    acc = None
    for lo, hi in _ff_chunks(wg_ref.shape[1]):
        g = _dot(xb, wg_ref[:, lo:hi])
        u = _dot(xb, wu_ref[:, lo:hi])
        a = (_silu(g) * u).astype(BF16)
        d = _dot(a, wd_ref[lo:hi, :])
        acc = d if acc is None else acc + d
    return acc


def _ffn1_kernel(x_ref, nw_ref, wg_ref, wu_ref, wd_ref, o_ref):
    x = x_ref[...]
    xb = _rms(x, nw_ref[...]).astype(BF16)
    o_ref[...] = x + 0.5 * _swiglu_acc(xb, wg_ref, wu_ref, wd_ref)


def _ffn1(x, nw, wg, wu, wd):
    m, d = x.shape
    f = wg.shape[1]
    tm = min(ROW_TILE, m)
    return pl.pallas_call(
        _ffn1_kernel,
        grid=(m // tm,),
        in_specs=[pl.BlockSpec((tm, d), lambda i: (i, 0)),
                  _const_spec((1, d)), _const_spec((d, f)), _const_spec((d, f)), _const_spec((f, d))],
        out_specs=pl.BlockSpec((tm, d), lambda i: (i, 0)),
        out_shape=jax.ShapeDtypeStruct((m, d), F32),
        compiler_params=pltpu.CompilerParams(dimension_semantics=("arbitrary",),
                                             vmem_limit_bytes=VMEM_LIMIT),
        name="ffn1",
    )(x, nw, wg, wu, wd)


def _tail_kernel(h_ref, mix_ref, p_ref, wo_ref, nw_ref, wg_ref, wu_ref, wd_ref,
                 pn_ref, wpg_ref, wpp_ref, fn_ref, o_ref):
    h = h_ref[...] + _dot(mix_ref[...], wo_ref[...])
    xb = _rms(h, nw_ref[...]).astype(BF16)
    h = h + 0.5 * _swiglu_acc(xb, wg_ref, wu_ref, wd_ref)
    gate = _sigmoid(_dot(_rms(h, pn_ref[...]).astype(BF16), wpg_ref[...]))
    h = h + gate * _dot(p_ref[...].astype(BF16), wpp_ref[...])
    o_ref[...] = _rms(h, fn_ref[...])


def _tail(h, mix, p, wo, nw, wg, wu, wd, pn, wpg, wpp, fn):
    m, d = h.shape
    f = wg.shape[1]
    pd = p.shape[1]
    tm = min(ROW_TILE, m)
    row = lambda w: pl.BlockSpec((tm, w), lambda i: (i, 0))
    return pl.pallas_call(
        _tail_kernel,
        grid=(m // tm,),
        in_specs=[row(d), row(d), row(pd),
                  _const_spec((d, d)), _const_spec((1, d)),
                  _const_spec((d, f)), _const_spec((d, f)), _const_spec((f, d)),
                  _const_spec((1, d)), _const_spec((d, d)), _const_spec((pd, d)), _const_spec((1, d))],
        out_specs=row(d),
        out_shape=jax.ShapeDtypeStruct((m, d), F32),
        compiler_params=pltpu.CompilerParams(dimension_semantics=("arbitrary",),
                                             vmem_limit_bytes=VMEM_LIMIT),
        name="tail",
    )(h, mix, p, wo, nw, wg, wu, wd, pn, wpg, wpp, fn)


def _gla_constants(c):
    levels = int(np.log2(c))
    idx = np.arange(c)
    i, t = idx[:, None], idx[None, :]
    mats = [(t <= i), (t > i)]
    masks = [(i == t)]
    for lv in range(levels):
        s = 1 << lv
        blk_i, blk_t = i // (2 * s), t // (2 * s)
        up_i, up_t = (i % (2 * s)) >= s, (t % (2 * s)) >= s
        same = blk_i == blk_t
        upper = same & up_i & up_t & (t <= i)
        lower = same & ~up_i & ~up_t & (t > i)
        mats.append(upper | lower)
        masks.append(same & up_i & ~up_t)
    cm = np.concatenate([m.astype(np.float32) for m in mats], axis=0)
    mk = np.stack([m.astype(np.float32) for m in masks], axis=0)
    return cm, mk


def _split_hi_lo(g):
    hi = g.astype(BF16)
    lo = (g - hi.astype(F32)).astype(BF16)
    return jnp.concatenate([hi, lo], axis=1)


def _column(row, eye):
    return jnp.sum(eye * row, axis=-1, keepdims=True)


def _mixer_kernel(h_ref, mn_ref, w_ref, wgk_ref, bgk_ref, gn_ref, pw_ref, ps_ref, cm_ref, mk_ref, eye_ref,
                  mix_ref, npool_ref, ngla_ref,
                  xb_s, ext_s, q_s, k_s, v_s, g_s, o_s, a_s, *, pos0):
    tm, d = h_ref.shape
    heads = GLA_HEADS
    qk = q_s.shape[1]
    dk, dv = qk // heads, d // heads
    c = GLA_CHUNK
    levels = mk_ref.shape[0] - 1
    l = pl.program_id(1)
    o_u, o_q, o_k, o_v = 0, d, d + qk, d + 2 * qk
    o_g = o_v + d
    o_ga, o_gb, o_lr = o_g + d, o_g + 2 * d, o_g + 3 * d

    @pl.when(l == 0)
    def _():
        ext_s[0:HIST, :] = jnp.zeros((HIST, d), F32)
        ngla_ref[...] = jnp.zeros(ngla_ref.shape, F32)

    xb_s[...] = _rms(h_ref[...], mn_ref[...]).astype(BF16)
    xb = xb_s[...]
    ext_s[HIST:HIST + tm, :] = _dot(xb, w_ref[:, o_u:o_u + d])
    q_s[...] = _dot(xb, w_ref[:, o_q:o_q + qk]) * (dk ** -0.5)
    k_s[...] = _dot(xb, w_ref[:, o_k:o_k + qk])
    v_s[...] = _dot(xb, w_ref[:, o_v:o_v + d]).astype(BF16)
    lr = _dot(xb, w_ref[:, o_lr:o_lr + LANE]).astype(BF16)
    g_s[...] = _log_sigmoid(_dot(lr, wgk_ref[...]) + bgk_ref[...]) * (1.0 / GATE_NORMALIZER)

    pos = (pos0 + 1 + l * tm + lax.broadcasted_iota(jnp.int32, (tm, 1), 0))
    gc = d // len(POOL_WINDOWS)
    for grp, w in enumerate(POOL_WINDOWS):
        cols = slice(grp * gc, (grp + 1) * gc)
        cur = ext_s[HIST:HIST + tm, cols]
        s = cur
        for dlt in range(1, w):
            s = s + ext_s[HIST - dlt:HIST - dlt + tm, cols]
        cnt = jnp.minimum(pos, w).astype(F32)
        pooled = (s / cnt - cur).astype(BF16)
        a_s[:, cols] = _dot(pooled, pw_ref[grp]) * ps_ref[:, cols]

    @pl.when(l == pl.num_programs(1) - 1)
    def _():
        npool_ref[0] = ext_s[HIST + tm - POOL_BUF:HIST + tm, :]

    ext_s[0:HIST, :] = ext_s[tm:tm + HIST, :]

    def chunk(ci, carry):
        r0 = pl.multiple_of(ci * c, c)
        rows = pl.ds(r0, c)
        for hd in range(heads):
            hc = slice(hd * dk, (hd + 1) * dk)
            hv = slice(hd * dv, (hd + 1) * dv)
            sums = _dot(cm_ref[...], _split_hi_lo(g_s[rows, hc]))
            e = jnp.exp(sums[:, :dk] + sums[:, dk:])
            qh, kh, vh = q_s[rows, hc], k_s[rows, hc], v_s[rows, hv]
            state = ngla_ref[0, hd]
            o = _dot((qh * e[0:c]).astype(BF16), state.astype(BF16))
            att = mk_ref[0] * _dot_nt(qh.astype(BF16), kh.astype(BF16))
            for lv in range(levels):
                f = e[(2 + lv) * c:(3 + lv) * c]
                att = att + mk_ref[1 + lv] * _dot_nt((qh * f).astype(BF16), (kh * f).astype(BF16))
            o = o + _dot(att.astype(BF16), vh)
            o_s[rows, hv] = o
            kdec = (kh * e[c:2 * c]).astype(BF16)
            decay = _column(e[c - 1:c], eye_ref[...])
            ngla_ref[0, hd] = state * decay + _dot_tn(kdec, vh)
        return carry

    lax.fori_loop(0, tm // c, chunk, 0)

    gn = gn_ref[...]
    for hd in range(heads):
        hv = slice(hd * dv, (hd + 1) * dv)
        on = _rms(o_s[:, hv], gn)
        gate = _silu(_dot(xb, w_ref[:, o_g + hd * dv:o_g + (hd + 1) * dv]))
        ga = _sigmoid(_dot(xb, w_ref[:, o_ga + hd * dv:o_ga + (hd + 1) * dv]))
        gb = _sigmoid(_dot(xb, w_ref[:, o_gb + hd * dv:o_gb + (hd + 1) * dv]))
        mix_ref[:, hv] = (ga * a_s[:, hv] + gb * (on * gate)).astype(BF16)


def _mixer(h, batch, mn, w, wgk, bgk, gn, pw, ps, cm, mk, eye):
    m, d = h.shape
    seq = m // batch
    tm = min(ROW_TILE, seq)
    nl = seq // tm
    qk = bgk.shape[1]
    heads = GLA_HEADS
    dk, dv = qk // heads, d // heads
    kern = functools.partial(_mixer_kernel, pos0=0)
    return pl.pallas_call(
        kern,
        grid=(batch, nl),
        in_specs=[pl.BlockSpec((tm, d), lambda b, l: (b * nl + l, 0)),
                  _const_spec(mn.shape), _const_spec(w.shape), _const_spec(wgk.shape), _const_spec(bgk.shape),
                  _const_spec(gn.shape), _const_spec(pw.shape), _const_spec(ps.shape),
                  _const_spec(cm.shape), _const_spec(mk.shape), _const_spec(eye.shape)],
        out_specs=[pl.BlockSpec((tm, d), lambda b, l: (b * nl + l, 0)),
                   pl.BlockSpec((1, POOL_BUF, d), lambda b, l: (b, 0, 0)),
                   pl.BlockSpec((1, heads, dk, dv), lambda b, l: (b, 0, 0, 0))],
        out_shape=[jax.ShapeDtypeStruct((m, d), BF16),
                   jax.ShapeDtypeStruct((batch, POOL_BUF, d), F32),
                   jax.ShapeDtypeStruct((batch, heads, dk, dv), F32)],
        scratch_shapes=[pltpu.VMEM((tm, d), BF16),
                        pltpu.VMEM((HIST + tm, d), F32),
                        pltpu.VMEM((tm, qk), F32),
                        pltpu.VMEM((tm, qk), F32),
                        pltpu.VMEM((tm, d), BF16),
                        pltpu.VMEM((tm, qk), F32),
                        pltpu.VMEM((tm, d), F32),
                        pltpu.VMEM((tm, d), F32)],
        compiler_params=pltpu.CompilerParams(dimension_semantics=("arbitrary", "arbitrary"),
                                             vmem_limit_bytes=VMEM_LIMIT),
        name="mixer_prompt",
    )(h, mn, w, wgk, bgk, gn, pw, ps, cm, mk, eye)


def _proj_kernel(h_ref, mn_ref, w_ref, wgk_ref, bgk_ref, z_ref, g_ref):
    d = h_ref.shape[1]
    qk = g_ref.shape[1]
    dk = qk // GLA_HEADS
    wide = z_ref.shape[1]
    xb = _rms(h_ref[...], mn_ref[...]).astype(BF16)
    z_ref[...] = _dot(xb, w_ref[:, 0:wide])
    z_ref[:, d:d + qk] = z_ref[:, d:d + qk] * (dk ** -0.5)
    lr = _dot(xb, w_ref[:, wide:wide + LANE]).astype(BF16)
    g_ref[...] = _log_sigmoid(_dot(lr, wgk_ref[...]) + bgk_ref[...]) * (1.0 / GATE_NORMALIZER)


def _proj(h, mn, w, wgk, bgk):
    m, d = h.shape
    qk = bgk.shape[1]
    wide = w.shape[1] - LANE
    return pl.pallas_call(
        _proj_kernel,
        grid=(1,),
        in_specs=[pl.BlockSpec((m, d), lambda i: (0, 0)),
                  _const_spec(mn.shape), _const_spec(w.shape), _const_spec(wgk.shape), _const_spec(bgk.shape)],
        out_specs=[pl.BlockSpec((m, wide), lambda i: (0, 0)), pl.BlockSpec((m, qk), lambda i: (0, 0))],
        out_shape=[jax.ShapeDtypeStruct((m, wide), F32), jax.ShapeDtypeStruct((m, qk), F32)],
        compiler_params=pltpu.CompilerParams(dimension_semantics=("arbitrary",),
                                             vmem_limit_bytes=VMEM_LIMIT),
        name="proj_sample",
    )(h, mn, w, wgk, bgk)


def _sample_mixer_kernel(z_ref, g_ref, sp_ref, sg_ref, gn_ref, pw_ref, ps_ref, eye_ref,
                         mix_ref, npool_ref, ngla_ref, ext_s, o_s, pl_s, *, n_tok, pos0):
    bb, tp, _ = z_ref.shape
    d = o_s.shape[2]
    heads = GLA_HEADS
    qk = g_ref.shape[2]
    dk, dv = qk // heads, d // heads
    o_u, o_q, o_k, o_v = 0, d, d + qk, d + 2 * qk
    o_g = o_v + d
    o_ga, o_gb = o_g + d, o_g + 2 * d
    row_qk = lax.broadcasted_iota(jnp.int32, (tp, qk), 0)
    row_1 = lax.broadcasted_iota(jnp.int32, (tp, 1), 0)
    gc = d // len(POOL_WINDOWS)

    def seq(n, carry):
        g = g_ref[n]
        b = g + jnp.where(row_qk >= 1, pltpu.roll(g, 1, 0), 0.0)
        b = b + jnp.where(row_qk >= 2, pltpu.roll(b, 2, 0), 0.0)
        b_last = b[n_tok - 1:n_tok, :]
        q = z_ref[n, :, o_q:o_q + qk]
        k = z_ref[n, :, o_k:o_k + qk]
        v = z_ref[n, :, o_v:o_v + d]
        q_in = (q * jnp.exp(b)).astype(BF16)
        k_dec = (k * jnp.exp(b_last - b)).astype(BF16)
        e_last = jnp.exp(b_last)
        vb = v.astype(BF16)
        for hd in range(heads):
            hc = slice(hd * dk, (hd + 1) * dk)
            hv = slice(hd * dv, (hd + 1) * dv)
            state = sg_ref[n, hd]
            o = _dot(q_in[:, hc], state.astype(BF16))
            for j in range(n_tok):
                w_ij = q[:, hc] * k[j:j + 1, hc] * jnp.exp(jnp.minimum(b[:, hc] - b[j:j + 1, hc], 0.0))
                a_ij = jnp.where(row_1 >= j, jnp.sum(w_ij, axis=-1, keepdims=True), 0.0)
                o = o + a_ij * v[j:j + 1, hv]
            o_s[n, :, hv] = o
            decay = _column(e_last[:, hc], eye_ref[...])
            ngla_ref[n, hd] = state * decay + _dot_tn(k_dec[:, hc], vb[:, hv])
        ext_s[0:HIST, :] = sp_ref[n]
        u = z_ref[n, :, o_u:o_u + d]
        ext_s[HIST:HIST + tp, :] = u
        for grp, w in enumerate(POOL_WINDOWS):
            cols = slice(grp * gc, (grp + 1) * gc)
            s = u[:, cols]
            for dlt in range(1, w):
                s = s + ext_s[HIST - dlt:HIST - dlt + tp, cols]
            cnt = jnp.minimum(pos0 + 1 + row_1, w).astype(F32)
            pl_s[n, :, cols] = s / cnt - u[:, cols]
        npool_ref[n] = ext_s[HIST + n_tok - POOL_BUF:HIST + n_tok, :]
        return carry

    lax.fori_loop(0, bb, seq, 0)

    rows = bb * tp
    gn = gn_ref[...]
    for hd in range(heads):
        hv = slice(hd * dv, (hd + 1) * dv)
        pooled = pl_s[:, :, hv].reshape(rows, dv).astype(BF16)
        a_out = _dot(pooled, pw_ref[hd]) * ps_ref[:, hv]
        on = _rms(o_s[:, :, hv].reshape(rows, dv), gn)
        gate = _silu(z_ref[:, :, o_g + hd * dv:o_g + (hd + 1) * dv].reshape(rows, dv))
        ga = _sigmoid(z_ref[:, :, o_ga + hd * dv:o_ga + (hd + 1) * dv].reshape(rows, dv))
        gb = _sigmoid(z_ref[:, :, o_gb + hd * dv:o_gb + (hd + 1) * dv].reshape(rows, dv))
        mix_ref[:, :, hv] = (ga * a_out + gb * (on * gate)).reshape(bb, tp, dv)


def _sample_mixer(z, g, sp, sg, gn, pw, ps, eye, n_tok):
    nb, tp, wide = z.shape
    qk = g.shape[2]
    d = sp.shape[2]
    heads = GLA_HEADS
    dk, dv = qk // heads, d // heads
    assert d // len(POOL_WINDOWS) == dv, "pool groups and attention heads share a column split here"
    bb = min(SAMPLE_BLOCK, nb)
    kern = functools.partial(_sample_mixer_kernel, n_tok=n_tok, pos0=PAST_LEN)
    blk3 = lambda a, b: pl.BlockSpec((bb, a, b), lambda i: (i, 0, 0))
    return pl.pallas_call(
        kern,
        grid=(nb // bb,),
        in_specs=[blk3(tp, wide), blk3(tp, qk), blk3(HIST, d),
                  pl.BlockSpec((bb, heads, dk, dv), lambda i: (i, 0, 0, 0)),
                  _const_spec(gn.shape), _const_spec(pw.shape), _const_spec(ps.shape), _const_spec(eye.shape)],
        out_specs=[blk3(tp, d), blk3(POOL_BUF, d),
                   pl.BlockSpec((bb, heads, dk, dv), lambda i: (i, 0, 0, 0))],
        out_shape=[jax.ShapeDtypeStruct((nb, tp, d), F32),
                   jax.ShapeDtypeStruct((nb, POOL_BUF, d), F32),
                   jax.ShapeDtypeStruct((nb, heads, dk, dv), F32)],
        scratch_shapes=[pltpu.VMEM((HIST + tp, d), F32),
                        pltpu.VMEM((bb, tp, d), F32),
                        pltpu.VMEM((bb, tp, d), F32)],
        compiler_params=pltpu.CompilerParams(dimension_semantics=("arbitrary",),
                                             vmem_limit_bytes=VMEM_LIMIT),
        name="mixer_sample",
    )(z, g, sp, sg, gn, pw, ps, eye)


def _layer_weights(i, ffn1_norm, ffn1_w_gate, ffn1_w_up, ffn1_w_down, mix_norm, w_in, w_gk_up, b_gk,
                   gla_norm, pool_w, pool_scale, w_out, ffn2_norm, ffn2_w_gate, ffn2_w_up, ffn2_w_down,
                   ple_norm, w_ple_gate, w_ple_proj):
    d = w_in.shape[1]
    qk = w_gk_up.shape[2]
    rank = w_gk_up.shape[1]
    o_lr = 2 * d + 2 * qk
    wi = w_in[i]
    packed = jnp.concatenate(
        [wi[:, :o_lr], wi[:, o_lr + rank:], wi[:, o_lr:o_lr + rank], jnp.zeros((d, LANE - rank), F32)],
        axis=1).astype(BF16)
    wgk = jnp.concatenate([w_gk_up[i], jnp.zeros((LANE - rank, qk), F32)], axis=0).astype(BF16)
    row = lambda a: a[i].reshape(1, -1)
    return dict(
        ffn1=(row(ffn1_norm), ffn1_w_gate[i].astype(BF16), ffn1_w_up[i].astype(BF16), ffn1_w_down[i].astype(BF16)),
        proj=(row(mix_norm), packed, wgk, row(b_gk)),
        mix=(row(gla_norm), pool_w[i].astype(BF16), row(pool_scale)),
        tail=(w_out[i].astype(BF16), row(ffn2_norm), ffn2_w_gate[i].astype(BF16), ffn2_w_up[i].astype(BF16),
              ffn2_w_down[i].astype(BF16), row(ple_norm), w_ple_gate[i].astype(BF16), w_ple_proj[i].astype(BF16)),
    )


def kernel(x_prompt, x_sample, p_prompt, p_sample, state_pool, state_gla, ffn1_norm, ffn1_w_gate, ffn1_w_up, ffn1_w_down, mix_norm, w_in, w_gk_up, b_gk, gla_norm, pool_w, pool_scale, w_out, ffn2_norm, ffn2_w_gate, ffn2_w_up, ffn2_w_down, ple_norm, w_ple_gate, w_ple_proj, final_norm):
    depth = w_in.shape[0]
    batch, seq, d = x_prompt.shape
    nb, n_tok, _ = x_sample.shape
    assert depth == 1, "the final norm is fused into the tail kernel of the only layer"
    assert n_tok <= 4 and POOL_BUF <= HIST and seq % GLA_CHUNK == 0
    cm, mk = _gla_constants(GLA_CHUNK)
    cm = jnp.asarray(cm, BF16)
    mk = jnp.asarray(mk, F32)
    eye = jnp.eye(w_gk_up.shape[2] // GLA_HEADS, dtype=F32)
    fn = final_norm.reshape(1, -1)

    lw = _layer_weights(0, ffn1_norm, ffn1_w_gate, ffn1_w_up, ffn1_w_down, mix_norm, w_in, w_gk_up, b_gk,
                        gla_norm, pool_w, pool_scale, w_out, ffn2_norm, ffn2_w_gate, ffn2_w_up, ffn2_w_down,
                        ple_norm, w_ple_gate, w_ple_proj)
    h1 = _ffn1(x_prompt.reshape(batch * seq, d), *lw["ffn1"])
    mix, pool_p, gla_p = _mixer(h1, batch, *lw["proj"], *lw["mix"], cm, mk, eye)
    y_prompt = _tail(h1, mix, p_prompt[0].reshape(batch * seq, -1), *lw["tail"], fn)
    h1 = _ffn1(x_sample.reshape(nb * n_tok, d), *lw["ffn1"])
    z, g = _proj(h1, *lw["proj"])
    pad3 = lambda a: jnp.pad(a.reshape(nb, n_tok, -1), ((0, 0), (0, SAMPLE_PAD - n_tok), (0, 0)))
    sp = jnp.pad(state_pool[0], ((0, 0), (HIST - POOL_BUF, 0), (0, 0)))
    mix, pool_s, gla_s = _sample_mixer(pad3(z), pad3(g), sp, state_gla[0], *lw["mix"], eye, n_tok)
    mix = mix[:, :n_tok].reshape(nb * n_tok, d).astype(BF16)
    y_sample = _tail(h1, mix, p_sample[0].reshape(nb * n_tok, -1), *lw["tail"], fn)
    return (y_prompt.reshape(batch, seq, d), y_sample.reshape(nb, n_tok, d),
            pool_p[None], gla_p[None], pool_s[None], gla_s[None])
```

```python
import functools

import numpy as np
import jax
import jax.numpy as jnp
from jax import lax
from jax.experimental import pallas as pl
from jax.experimental.pallas import tpu as pltpu

POOL_WINDOWS = (2, 4, 8, 16)
POOL_BUF = max(POOL_WINDOWS) - 1
GLA_HEADS = 4
GATE_NORMALIZER = 16.0
EPS = 1e-6
PAST_LEN = 16384

ROW_TILE = 512
GLA_CHUNK = 128
HIST = 16
SAMPLE_PAD = 8
SAMPLE_BLOCK = 8
LANE = 128
VMEM_LIMIT = 56 * 1024 * 1024

F32 = jnp.float32
BF16 = jnp.bfloat16


def _dot(a, b):
    return jnp.dot(a, b, preferred_element_type=F32)


def _dot_nt(a, b):
    return lax.dot_general(a, b, (((1,), (1,)), ((), ())), preferred_element_type=F32)


def _dot_tn(a, b):
    return lax.dot_general(a, b, (((0,), (0,)), ((), ())), preferred_element_type=F32)


def _rms(x, w):
    ms = jnp.mean(x * x, axis=-1, keepdims=True)
    return x * lax.rsqrt(ms + EPS) * w


def _sigmoid(x):
    return 1.0 / (1.0 + jnp.exp(-x))


def _silu(x):
    return x * _sigmoid(x)


def _log_sigmoid(x):
    return jnp.minimum(x, 0.0) - jnp.log1p(jnp.exp(-jnp.abs(x)))


def _ff_chunks(f):
    step = 1024
    return [(lo, min(lo + step, f)) for lo in range(0, f, step)]


def _const_spec(shape):
    nd = len(shape)
    return pl.BlockSpec(shape, lambda *_: (0,) * nd, pipeline_mode=pl.Buffered(1))


def _swiglu_acc(xb, wg_ref, wu_ref, wd_ref):
    acc = None
    for lo, hi in _ff_chunks(wg_ref.shape[1]):
        g = _dot(xb, wg_ref[:, lo:hi])
        u = _dot(xb, wu_ref[:, lo:hi])
        a = (_silu(g) * u).astype(BF16)
        d = _dot(a, wd_ref[lo:hi, :])
        acc = d if acc is None else acc + d
    return acc


def _ffn1_kernel(x_ref, nw_ref, wg_ref, wu_ref, wd_ref, o_ref):
    x = x_ref[...]
    xb = _rms(x, nw_ref[...]).astype(BF16)
    o_ref[...] = x + 0.5 * _swiglu_acc(xb, wg_ref, wu_ref, wd_ref)


def _ffn1(x, nw, wg, wu, wd):
    m, d = x.shape
    f = wg.shape[1]
    tm = min(ROW_TILE, m)
    return pl.pallas_call(
        _ffn1_kernel,
        grid=(m // tm,),
        in_specs=[pl.BlockSpec((tm, d), lambda i: (i, 0)),
                  _const_spec((1, d)), _const_spec((d, f)), _const_spec((d, f)), _const_spec((f, d))],
        out_specs=pl.BlockSpec((tm, d), lambda i: (i, 0)),
        out_shape=jax.ShapeDtypeStruct((m, d), F32),
        compiler_params=pltpu.CompilerParams(dimension_semantics=("arbitrary",),
                                             vmem_limit_bytes=VMEM_LIMIT),
        name="ffn1",
    )(x, nw, wg, wu, wd)


def _tail_kernel(h_ref, mix_ref, p_ref, wo_ref, nw_ref, wg_ref, wu_ref, wd_ref,
                 pn_ref, wpg_ref, wpp_ref, fn_ref, o_ref):
    h = h_ref[...] + _dot(mix_ref[...], wo_ref[...])
    xb = _rms(h, nw_ref[...]).astype(BF16)
    h = h + 0.5 * _swiglu_acc(xb, wg_ref, wu_ref, wd_ref)
    gate = _sigmoid(_dot(_rms(h, pn_ref[...]).astype(BF16), wpg_ref[...]))
    h = h + gate * _dot(p_ref[...].astype(BF16), wpp_ref[...])
    o_ref[...] = _rms(h, fn_ref[...])


def _tail(h, mix, p, wo, nw, wg, wu, wd, pn, wpg, wpp, fn):
    m, d = h.shape
    f = wg.shape[1]
    pd = p.shape[1]
    tm = min(ROW_TILE, m)
    row = lambda w: pl.BlockSpec((tm, w), lambda i: (i, 0))
    return pl.pallas_call(
        _tail_kernel,
        grid=(m // tm,),
        in_specs=[row(d), row(d), row(pd),
                  _const_spec((d, d)), _const_spec((1, d)),
                  _const_spec((d, f)), _const_spec((d, f)), _const_spec((f, d)),
                  _const_spec((1, d)), _const_spec((d, d)), _const_spec((pd, d)), _const_spec((1, d))],
        out_specs=row(d),
        out_shape=jax.ShapeDtypeStruct((m, d), F32),
        compiler_params=pltpu.CompilerParams(dimension_semantics=("arbitrary",),
                                             vmem_limit_bytes=VMEM_LIMIT),
        name="tail",
    )(h, mix, p, wo, nw, wg, wu, wd, pn, wpg, wpp, fn)


def _gla_constants(c):
    idx = np.arange(c)
    i, t = idx[:, None], idx[None, :]
    masks = []
    s = 1
    while s < c:
        same = (i // (2 * s)) == (t // (2 * s))
        masks.append(same & ((i % (2 * s)) >= s) & ((t % (2 * s)) < s))
        s *= 2
    tri = (t <= i).astype(np.float32)
    return tri, np.stack([m.astype(np.float32) for m in masks], axis=0)


def _level_exponents(g, b, b_ref, r0):
    c, n = g.shape
    row = lax.broadcasted_iota(jnp.int32, (c, n), 0)
    m4 = row & 3
    g_prev = pltpu.roll(g, 1, 0)
    g_next = pltpu.roll(g, c - 1, 0)
    out = [jnp.where((row & 1) == 1, g, 0.0),
           jnp.where(m4 == 0, g_next, jnp.where(m4 == 1, 0.0, jnp.where(m4 == 2, g, g + g_prev)))]
    s = 4
    while s < c:
        blocks = [jnp.broadcast_to(b_ref[pl.ds(r0 + m * 2 * s + s - 1, 1), :], (2 * s, n))
                  for m in range(c // (2 * s))]
        ref = blocks[0] if len(blocks) == 1 else jnp.concatenate(blocks, axis=0)
        out.append(-jnp.abs(b - ref))
        s *= 2
    return out


def _split_hi_lo(g):
    hi = g.astype(BF16)
    lo = (g - hi.astype(F32)).astype(BF16)
    return jnp.concatenate([hi, lo], axis=1)


def _column(row, eye):
    return jnp.sum(eye * row, axis=-1, keepdims=True)


def _mixer_kernel(h_ref, mn_ref, w_ref, wgk_ref, bgk_ref, gn_ref, pw_ref, ps_ref, tri_ref, mk_ref, eye_ref,
                  mix_ref, npool_ref, ngla_ref,
                  xb_s, ext_s, q_s, k_s, v_s, g_s, o_s, a_s, b_s, qin_s, kdec_s, dcy_s, qf_s, kf_s, att_s,
                  *, pos0):
    tm, d = h_ref.shape
    heads = GLA_HEADS
    qk = q_s.shape[1]
    dk, dv = qk // heads, d // heads
    c = GLA_CHUNK
    levels = mk_ref.shape[0]
    l = pl.program_id(1)
    o_u, o_q, o_k, o_v = 0, d, d + qk, d + 2 * qk
    o_g = o_v + d
    o_ga, o_gb, o_lr = o_g + d, o_g + 2 * d, o_g + 3 * d

    @pl.when(l == 0)
    def _():
        ext_s[0:HIST, :] = jnp.zeros((HIST, d), F32)
        ngla_ref[...] = jnp.zeros(ngla_ref.shape, F32)

    xb_s[...] = _rms(h_ref[...], mn_ref[...]).astype(BF16)
    xb = xb_s[...]
    ext_s[HIST:HIST + tm, :] = _dot(xb, w_ref[:, o_u:o_u + d])
    q_s[...] = _dot(xb, w_ref[:, o_q:o_q + qk]) * (dk ** -0.5)
    k_s[...] = _dot(xb, w_ref[:, o_k:o_k + qk])
    v_s[...] = _dot(xb, w_ref[:, o_v:o_v + d]).astype(BF16)
    lr = _dot(xb, w_ref[:, o_lr:o_lr + LANE]).astype(BF16)
    g_s[...] = _log_sigmoid(_dot(lr, wgk_ref[...]) + bgk_ref[...]) * (1.0 / GATE_NORMALIZER)

    pos = (pos0 + 1 + l * tm + lax.broadcasted_iota(jnp.int32, (tm, 1), 0))
    gc = d // len(POOL_WINDOWS)
    for grp, w in enumerate(POOL_WINDOWS):
        cols = slice(grp * gc, (grp + 1) * gc)
        cur = ext_s[HIST:HIST + tm, cols]
        s = cur
        for dlt in range(1, w):
            s = s + ext_s[HIST - dlt:HIST - dlt + tm, cols]
        cnt = jnp.minimum(pos, w).astype(F32)
        pooled = (s / cnt - cur).astype(BF16)
        a_s[:, cols] = _dot(pooled, pw_ref[grp]) * ps_ref[:, cols]

    @pl.when(l == pl.num_programs(1) - 1)
    def _():
        npool_ref[0] = ext_s[HIST + tm - POOL_BUF:HIST + tm, :]

    ext_s[0:HIST, :] = ext_s[tm:tm + HIST, :]

    def chunk_rows(ci):
        r0 = pl.multiple_of(ci * c, c)
        return r0, pl.ds(r0, c)

    def head_cols(hd):
        return slice(hd * dk, (hd + 1) * dk), slice(hd * dv, (hd + 1) * dv), slice(hd * c, (hd + 1) * c)

    def prep(ci, carry):
        r0, rows = chunk_rows(ci)
        g = g_s[rows, :]
        sums = _dot(tri_ref[...], _split_hi_lo(g))
        b = sums[:, :qk] + sums[:, qk:]
        b_s[rows, :] = b
        b_last = b[c - 1:c, :]
        q = q_s[rows, :]
        k = k_s[rows, :]
        qin_s[rows, :] = (q * jnp.exp(b)).astype(BF16)
        kdec_s[rows, :] = (k * jnp.exp(b_last - b)).astype(BF16)
        dcy_s[pl.ds(pl.multiple_of(ci * 8, 8), 1), :] = jnp.exp(b_last)
        for lv, ex in enumerate(_level_exponents(g, b, b_s, r0)):
            f = jnp.exp(ex)
            qf_s[lv, rows, :] = (q * f).astype(BF16)
            kf_s[lv, rows, :] = (k * f).astype(BF16)
        qk_prod = q * k
        for hd in range(heads):
            hc, hv, _ = head_cols(hd)
            o_s[rows, hv] = jnp.sum(qk_prod[:, hc], axis=-1, keepdims=True) * v_s[rows, hv].astype(F32)
        return carry

    def attend(ci, carry):
        _, rows = chunk_rows(ci)
        for hd in range(heads):
            hc, _, ha = head_cols(hd)
            att = None
            for lv in range(levels):
                t = mk_ref[lv] * _dot_nt(qf_s[lv, rows, hc], kf_s[lv, rows, hc])
                att = t if att is None else att + t
            att_s[rows, ha] = att.astype(BF16)
        return carry

    def advance(ci, carry):
        _, rows = chunk_rows(ci)
        for hd in range(heads):
            hc, hv, ha = head_cols(hd)
            state = ngla_ref[0, hd]
            vh = v_s[rows, hv]
            lhs = jnp.concatenate([qin_s[rows, hc], att_s[rows, ha]], axis=1)
            rhs = jnp.concatenate([state.astype(BF16), vh], axis=0)
            o_s[rows, hv] += _dot(lhs, rhs)
            decay = _column(dcy_s[pl.ds(pl.multiple_of(ci * 8, 8), 1), hc], eye_ref[...])
            ngla_ref[0, hd] = state * decay + _dot_tn(kdec_s[rows, hc], vh)
        return carry

    lax.fori_loop(0, tm // c, prep, 0)
    lax.fori_loop(0, tm // c, attend, 0)
    lax.fori_loop(0, tm // c, advance, 0)

    gn = gn_ref[...]
    for hd in range(heads):
        hv = slice(hd * dv, (hd + 1) * dv)
        on = _rms(o_s[:, hv], gn)
        gate = _silu(_dot(xb, w_ref[:, o_g + hd * dv:o_g + (hd + 1) * dv]))
        ga = _sigmoid(_dot(xb, w_ref[:, o_ga + hd * dv:o_ga + (hd + 1) * dv]))
        gb = _sigmoid(_dot(xb, w_ref[:, o_gb + hd * dv:o_gb + (hd + 1) * dv]))
        mix_ref[:, hv] = (ga * a_s[:, hv] + gb * (on * gate)).astype(BF16)


def _mixer(h, batch, mn, w, wgk, bgk, gn, pw, ps, tri, mk, eye):
    m, d = h.shape
    seq = m // batch
    tm = min(ROW_TILE, seq)
    nl = seq // tm
    qk = bgk.shape[1]
    heads = GLA_HEADS
    dk, dv = qk // heads, d // heads
    c = GLA_CHUNK
    levels = mk.shape[0]
    kern = functools.partial(_mixer_kernel, pos0=0)
    return pl.pallas_call(
        kern,
        grid=(batch, nl),
        in_specs=[pl.BlockSpec((tm, d), lambda b, l: (b * nl + l, 0)),
                  _const_spec(mn.shape), _const_spec(w.shape), _const_spec(wgk.shape), _const_spec(bgk.shape),
                  _const_spec(gn.shape), _const_spec(pw.shape), _const_spec(ps.shape),
                  _const_spec(tri.shape), _const_spec(mk.shape), _const_spec(eye.shape)],
        out_specs=[pl.BlockSpec((tm, d), lambda b, l: (b * nl + l, 0)),
                   pl.BlockSpec((1, POOL_BUF, d), lambda b, l: (b, 0, 0)),
                   pl.BlockSpec((1, heads, dk, dv), lambda b, l: (b, 0, 0, 0))],
        out_shape=[jax.ShapeDtypeStruct((m, d), BF16),
                   jax.ShapeDtypeStruct((batch, POOL_BUF, d), F32),
                   jax.ShapeDtypeStruct((batch, heads, dk, dv), F32)],
        scratch_shapes=[pltpu.VMEM((tm, d), BF16),
                        pltpu.VMEM((HIST + tm, d), F32),
                        pltpu.VMEM((tm, qk), F32),
                        pltpu.VMEM((tm, qk), F32),
                        pltpu.VMEM((tm, d), BF16),
                        pltpu.VMEM((tm, qk), F32),
                        pltpu.VMEM((tm, d), F32),
                        pltpu.VMEM((tm, d), F32),
                        pltpu.VMEM((tm, qk), F32),
                        pltpu.VMEM((tm, qk), BF16),
                        pltpu.VMEM((tm, qk), BF16),
                        pltpu.VMEM((tm // c * 8, qk), F32),
                        pltpu.VMEM((levels, tm, qk), BF16),
                        pltpu.VMEM((levels, tm, qk), BF16),
                        pltpu.VMEM((tm, heads * c), BF16)],
        compiler_params=pltpu.CompilerParams(dimension_semantics=("arbitrary", "arbitrary"),
                                             vmem_limit_bytes=VMEM_LIMIT),
        name="mixer_prompt",
    )(h, mn, w, wgk, bgk, gn, pw, ps, tri, mk, eye)


def _proj_kernel(h_ref, mn_ref, w_ref, wgk_ref, bgk_ref, z_ref, g_ref):
    d = h_ref.shape[1]
    qk = g_ref.shape[1]
    dk = qk // GLA_HEADS
    wide = z_ref.shape[1]
    xb = _rms(h_ref[...], mn_ref[...]).astype(BF16)
    z_ref[...] = _dot(xb, w_ref[:, 0:wide])
    z_ref[:, d:d + qk] = z_ref[:, d:d + qk] * (dk ** -0.5)
    lr = _dot(xb, w_ref[:, wide:wide + LANE]).astype(BF16)
    g_ref[...] = _log_sigmoid(_dot(lr, wgk_ref[...]) + bgk_ref[...]) * (1.0 / GATE_NORMALIZER)


def _proj(h, mn, w, wgk, bgk):
    m, d = h.shape
    qk = bgk.shape[1]
    wide = w.shape[1] - LANE
    return pl.pallas_call(
        _proj_kernel,
        grid=(1,),
        in_specs=[pl.BlockSpec((m, d), lambda i: (0, 0)),
                  _const_spec(mn.shape), _const_spec(w.shape), _const_spec(wgk.shape), _const_spec(bgk.shape)],
        out_specs=[pl.BlockSpec((m, wide), lambda i: (0, 0)), pl.BlockSpec((m, qk), lambda i: (0, 0))],
        out_shape=[jax.ShapeDtypeStruct((m, wide), F32), jax.ShapeDtypeStruct((m, qk), F32)],
        compiler_params=pltpu.CompilerParams(dimension_semantics=("arbitrary",),
                                             vmem_limit_bytes=VMEM_LIMIT),
        name="proj_sample",
    )(h, mn, w, wgk, bgk)


def _sample_mixer_kernel(z_ref, g_ref, sp_ref, sg_ref, gn_ref, pw_ref, ps_ref, eye_ref,
                         mix_ref, npool_ref, ngla_ref, ext_s, o_s, pl_s, *, n_tok, pos0):
    bb, tp, _ = z_ref.shape
    d = o_s.shape[2]
    heads = GLA_HEADS
    qk = g_ref.shape[2]
    dk, dv = qk // heads, d // heads
    o_u, o_q, o_k, o_v = 0, d, d + qk, d + 2 * qk
    o_g = o_v + d
    o_ga, o_gb = o_g + d, o_g + 2 * d
    row_qk = lax.broadcasted_iota(jnp.int32, (tp, qk), 0)
    row_1 = lax.broadcasted_iota(jnp.int32, (tp, 1), 0)
    gc = d // len(POOL_WINDOWS)

    def seq(n, carry):
        g = g_ref[n]
        b = g + jnp.where(row_qk >= 1, pltpu.roll(g, 1, 0), 0.0)
        b = b + jnp.where(row_qk >= 2, pltpu.roll(b, 2, 0), 0.0)
        b_last = b[n_tok - 1:n_tok, :]
        q = z_ref[n, :, o_q:o_q + qk]
        k = z_ref[n, :, o_k:o_k + qk]
        v = z_ref[n, :, o_v:o_v + d]
        q_in = (q * jnp.exp(b)).astype(BF16)
        k_dec = (k * jnp.exp(b_last - b)).astype(BF16)
        e_last = jnp.exp(b_last)
        vb = v.astype(BF16)
        for hd in range(heads):
            hc = slice(hd * dk, (hd + 1) * dk)
            hv = slice(hd * dv, (hd + 1) * dv)
            state = sg_ref[n, hd]
            o = _dot(q_in[:, hc], state.astype(BF16))
            for j in range(n_tok):
                w_ij = q[:, hc] * k[j:j + 1, hc] * jnp.exp(jnp.minimum(b[:, hc] - b[j:j + 1, hc], 0.0))
                a_ij = jnp.where(row_1 >= j, jnp.sum(w_ij, axis=-1, keepdims=True), 0.0)
                o = o + a_ij * v[j:j + 1, hv]
            o_s[n, :, hv] = o
            decay = _column(e_last[:, hc], eye_ref[...])
            ngla_ref[n, hd] = state * decay + _dot_tn(k_dec[:, hc], vb[:, hv])
        ext_s[0:HIST, :] = sp_ref[n]
        u = z_ref[n, :, o_u:o_u + d]
        ext_s[HIST:HIST + tp, :] = u
        for grp, w in enumerate(POOL_WINDOWS):
            cols = slice(grp * gc, (grp + 1) * gc)
            s = u[:, cols]
            for dlt in range(1, w):
                s = s + ext_s[HIST - dlt:HIST - dlt + tp, cols]
            cnt = jnp.minimum(pos0 + 1 + row_1, w).astype(F32)
            pl_s[n, :, cols] = s / cnt - u[:, cols]
        npool_ref[n] = ext_s[HIST + n_tok - POOL_BUF:HIST + n_tok, :]
        return carry

    lax.fori_loop(0, bb, seq, 0)

    rows = bb * tp
    gn = gn_ref[...]
    for hd in range(heads):
        hv = slice(hd * dv, (hd + 1) * dv)
        pooled = pl_s[:, :, hv].reshape(rows, dv).astype(BF16)
        a_out = _dot(pooled, pw_ref[hd]) * ps_ref[:, hv]
        on = _rms(o_s[:, :, hv].reshape(rows, dv), gn)
        gate = _silu(z_ref[:, :, o_g + hd * dv:o_g + (hd + 1) * dv].reshape(rows, dv))
        ga = _sigmoid(z_ref[:, :, o_ga + hd * dv:o_ga + (hd + 1) * dv].reshape(rows, dv))
        gb = _sigmoid(z_ref[:, :, o_gb + hd * dv:o_gb + (hd + 1) * dv].reshape(rows, dv))
        mix_ref[:, :, hv] = (ga * a_out + gb * (on * gate)).reshape(bb, tp, dv)


def _sample_mixer(z, g, sp, sg, gn, pw, ps, eye, n_tok):
    nb, tp, wide = z.shape
    qk = g.shape[2]
    d = sp.shape[2]
    heads = GLA_HEADS
    dk, dv = qk // heads, d // heads
    assert d // len(POOL_WINDOWS) == dv, "pool groups and attention heads share a column split here"
    bb = min(SAMPLE_BLOCK, nb)
    kern = functools.partial(_sample_mixer_kernel, n_tok=n_tok, pos0=PAST_LEN)
    blk3 = lambda a, b: pl.BlockSpec((bb, a, b), lambda i: (i, 0, 0))
    return pl.pallas_call(
        kern,
        grid=(nb // bb,),
        in_specs=[blk3(tp, wide), blk3(tp, qk), blk3(HIST, d),
                  pl.BlockSpec((bb, heads, dk, dv), lambda i: (i, 0, 0, 0)),
                  _const_spec(gn.shape), _const_spec(pw.shape), _const_spec(ps.shape), _const_spec(eye.shape)],
        out_specs=[blk3(tp, d), blk3(POOL_BUF, d),
                   pl.BlockSpec((bb, heads, dk, dv), lambda i: (i, 0, 0, 0))],
        out_shape=[jax.ShapeDtypeStruct((nb, tp, d), F32),
                   jax.ShapeDtypeStruct((nb, POOL_BUF, d), F32),
                   jax.ShapeDtypeStruct((nb, heads, dk, dv), F32)],
        scratch_shapes=[pltpu.VMEM((HIST + tp, d), F32),
                        pltpu.VMEM((bb, tp, d), F32),
                        pltpu.VMEM((bb, tp, d), F32)],
        compiler_params=pltpu.CompilerParams(dimension_semantics=("arbitrary",),
                                             vmem_limit_bytes=VMEM_LIMIT),
        name="mixer_sample",
    )(z, g, sp, sg, gn, pw, ps, eye)


def _layer_weights(i, ffn1_norm, ffn1_w_gate, ffn1_w_up, ffn1_w_down, mix_norm, w_in, w_gk_up, b_gk,
                   gla_norm, pool_w, pool_scale, w_out, ffn2_norm, ffn2_w_gate, ffn2_w_up, ffn2_w_down,
                   ple_norm, w_ple_gate, w_ple_proj):
    d = w_in.shape[1]
    qk = w_gk_up.shape[2]
    rank = w_gk_up.shape[1]
    o_lr = 2 * d + 2 * qk
    wi = w_in[i]
    packed = jnp.concatenate(
        [wi[:, :o_lr], wi[:, o_lr + rank:], wi[:, o_lr:o_lr + rank], jnp.zeros((d, LANE - rank), F32)],
        axis=1).astype(BF16)
    wgk = jnp.concatenate([w_gk_up[i], jnp.zeros((LANE - rank, qk), F32)], axis=0).astype(BF16)
    row = lambda a: a[i].reshape(1, -1)
    return dict(
        ffn1=(row(ffn1_norm), ffn1_w_gate[i].astype(BF16), ffn1_w_up[i].astype(BF16), ffn1_w_down[i].astype(BF16)),
        proj=(row(mix_norm), packed, wgk, row(b_gk)),
        mix=(row(gla_norm), pool_w[i].astype(BF16), row(pool_scale)),
        tail=(w_out[i].astype(BF16), row(ffn2_norm), ffn2_w_gate[i].astype(BF16), ffn2_w_up[i].astype(BF16),
              ffn2_w_down[i].astype(BF16), row(ple_norm), w_ple_gate[i].astype(BF16), w_ple_proj[i].astype(BF16)),
    )


def kernel(x_prompt, x_sample, p_prompt, p_sample, state_pool, state_gla, ffn1_norm, ffn1_w_gate, ffn1_w_up, ffn1_w_down, mix_norm, w_in, w_gk_up, b_gk, gla_norm, pool_w, pool_scale, w_out, ffn2_norm, ffn2_w_gate, ffn2_w_up, ffn2_w_down, ple_norm, w_ple_gate, w_ple_proj, final_norm):
    depth = w_in.shape[0]
    batch, seq, d = x_prompt.shape
    nb, n_tok, _ = x_sample.shape
    assert depth == 1, "the final norm is fused into the tail kernel of the only layer"
    assert n_tok <= 4 and POOL_BUF <= HIST and seq % GLA_CHUNK == 0
    tri, mk = _gla_constants(GLA_CHUNK)
    tri = jnp.asarray(tri, BF16)
    mk = jnp.asarray(mk, F32)
    eye = jnp.eye(w_gk_up.shape[2] // GLA_HEADS, dtype=F32)
    fn = final_norm.reshape(1, -1)

    lw = _layer_weights(0, ffn1_norm, ffn1_w_gate, ffn1_w_up, ffn1_w_down, mix_norm, w_in, w_gk_up, b_gk,
                        gla_norm, pool_w, pool_scale, w_out, ffn2_norm, ffn2_w_gate, ffn2_w_up, ffn2_w_down,
                        ple_norm, w_ple_gate, w_ple_proj)
    h1 = _ffn1(x_prompt.reshape(batch * seq, d), *lw["ffn1"])
    mix, pool_p, gla_p = _mixer(h1, batch, *lw["proj"], *lw["mix"], tri, mk, eye)
    y_prompt = _tail(h1, mix, p_prompt[0].reshape(batch * seq, -1), *lw["tail"], fn)
    h1 = _ffn1(x_sample.reshape(nb * n_tok, d), *lw["ffn1"])
    z, g = _proj(h1, *lw["proj"])
    pad3 = lambda a: jnp.pad(a.reshape(nb, n_tok, -1), ((0, 0), (0, SAMPLE_PAD - n_tok), (0, 0)))
    sp = jnp.pad(state_pool[0], ((0, 0), (HIST - POOL_BUF, 0), (0, 0)))
    mix, pool_s, gla_s = _sample_mixer(pad3(z), pad3(g), sp, state_gla[0], *lw["mix"], eye, n_tok)
    mix = mix[:, :n_tok].reshape(nb * n_tok, d).astype(BF16)
    y_sample = _tail(h1, mix, p_sample[0].reshape(nb * n_tok, -1), *lw["tail"], fn)
    return (y_prompt.reshape(batch, seq, d), y_sample.reshape(nb, n_tok, d),
            pool_p[None], gla_p[None], pool_s[None], gla_s[None])
```

```python
import functools

import numpy as np
import jax
import jax.numpy as jnp
from jax import lax
from jax.experimental import pallas as pl
from jax.experimental.pallas import tpu as pltpu

POOL_WINDOWS = (2, 4, 8, 16)
POOL_BUF = max(POOL_WINDOWS) - 1
GLA_HEADS = 4
GATE_NORMALIZER = 16.0
EPS = 1e-6
PAST_LEN = 16384

ROW_TILE = 512
GLA_CHUNK = 128
HIST = 16
SAMPLE_PAD = 8
SAMPLE_BLOCK = 8
LANE = 128
GATE_BLOCK = 256
LOG2_DECAY_SCALE = float(np.log2(np.e)) / GATE_NORMALIZER
VMEM_LIMIT = 56 * 1024 * 1024

F32 = jnp.float32
BF16 = jnp.bfloat16


def _dot(a, b):
    return jnp.dot(a, b, preferred_element_type=F32)


def _dot_nt(a, b):
    return lax.dot_general(a, b, (((1,), (1,)), ((), ())), preferred_element_type=F32)


def _dot_tn(a, b):
    return lax.dot_general(a, b, (((0,), (0,)), ((), ())), preferred_element_type=F32)


def _rms(x, w):
    ms = jnp.mean(x * x, axis=-1, keepdims=True)
    return x * lax.rsqrt(ms + EPS) * w


def _sigmoid(x):
    return 1.0 / (1.0 + jnp.exp(-x))


def _silu(x):
    return x * _sigmoid(x)


def _log_sigmoid(x):
    return jnp.minimum(x, 0.0) - jnp.log1p(jnp.exp(-jnp.abs(x)))


def _ff_chunks(f):
    step = 1024
    return [(lo, min(lo + step, f)) for lo in range(0, f, step)]


def _const_spec(shape):
    nd = len(shape)
    return pl.BlockSpec(shape, lambda *_: (0,) * nd, pipeline_mode=pl.Buffered(1))


def _swiglu_acc(xb, wg_ref, wu_ref, wd_ref):
    acc = None
    for lo, hi in _ff_chunks(wg_ref.shape[1]):
        g = _dot(xb, wg_ref[:, lo:hi])
        u = _dot(xb, wu_ref[:, lo:hi])
        a = (_silu(g) * u).astype(BF16)
        d = _dot(a, wd_ref[lo:hi, :])
        acc = d if acc is None else acc + d
    return acc


def _ffn1_kernel(x_ref, nw_ref, wg_ref, wu_ref, wd_ref, o_ref):
    x = x_ref[...]
    xb = _rms(x, nw_ref[...]).astype(BF16)
    o_ref[...] = x + 0.5 * _swiglu_acc(xb, wg_ref, wu_ref, wd_ref)


def _ffn1(x, nw, wg, wu, wd):
    m, d = x.shape
    f = wg.shape[1]
    tm = min(ROW_TILE, m)
    return pl.pallas_call(
        _ffn1_kernel,
        grid=(m // tm,),
        in_specs=[pl.BlockSpec((tm, d), lambda i: (i, 0)),
                  _const_spec((1, d)), _const_spec((d, f)), _const_spec((d, f)), _const_spec((f, d))],
        out_specs=pl.BlockSpec((tm, d), lambda i: (i, 0)),
        out_shape=jax.ShapeDtypeStruct((m, d), F32),
        compiler_params=pltpu.CompilerParams(dimension_semantics=("arbitrary",),
                                             vmem_limit_bytes=VMEM_LIMIT),
        name="ffn1",
    )(x, nw, wg, wu, wd)


def _tail_kernel(h_ref, mix_ref, p_ref, wo_ref, nw_ref, wg_ref, wu_ref, wd_ref,
                 pn_ref, wpg_ref, wpp_ref, fn_ref, o_ref):
    h = h_ref[...] + _dot(mix_ref[...], wo_ref[...])
    xb = _rms(h, nw_ref[...]).astype(BF16)
    h = h + 0.5 * _swiglu_acc(xb, wg_ref, wu_ref, wd_ref)
    gate = _sigmoid(_dot(_rms(h, pn_ref[...]).astype(BF16), wpg_ref[...]))
    h = h + gate * _dot(p_ref[...].astype(BF16), wpp_ref[...])
    o_ref[...] = _rms(h, fn_ref[...])


def _tail(h, mix, p, wo, nw, wg, wu, wd, pn, wpg, wpp, fn):
    m, d = h.shape
    f = wg.shape[1]
    pd = p.shape[1]
    tm = min(ROW_TILE, m)
    row = lambda w: pl.BlockSpec((tm, w), lambda i: (i, 0))
    return pl.pallas_call(
        _tail_kernel,
        grid=(m // tm,),
        in_specs=[row(d), row(d), row(pd),
                  _const_spec((d, d)), _const_spec((1, d)),
                  _const_spec((d, f)), _const_spec((d, f)), _const_spec((f, d)),
                  _const_spec((1, d)), _const_spec((d, d)), _const_spec((pd, d)), _const_spec((1, d))],
        out_specs=row(d),
        out_shape=jax.ShapeDtypeStruct((m, d), F32),
        compiler_params=pltpu.CompilerParams(dimension_semantics=("arbitrary",),
                                             vmem_limit_bytes=VMEM_LIMIT),
        name="tail",
    )(h, mix, p, wo, nw, wg, wu, wd, pn, wpg, wpp, fn)


def _gla_constants(c):
    idx = np.arange(c)
    i, t = idx[:, None], idx[None, :]
    masks = []
    s = 1
    while s < c:
        same = (i // (2 * s)) == (t // (2 * s))
        masks.append(same & ((i % (2 * s)) >= s) & ((t % (2 * s)) < s))
        s *= 2
    tri = (t <= i).astype(np.float32)
    return tri, np.stack([m.astype(np.float32) for m in masks], axis=0)


def _level_exponents(g, b, b_ref, r0):
    c, n = g.shape
    row = lax.broadcasted_iota(jnp.int32, (c, n), 0)
    m4 = row & 3
    g_prev = pltpu.roll(g, 1, 0)
    g_next = pltpu.roll(g, c - 1, 0)
    out = [jnp.where((row & 1) == 1, g, 0.0),
           jnp.where(m4 == 0, g_next, jnp.where(m4 == 1, 0.0, jnp.where(m4 == 2, g, g + g_prev)))]
    s = 4
    while s < c:
        blocks = [jnp.broadcast_to(b_ref[r0 + m * 2 * s + s - 1:r0 + m * 2 * s + s, :], (2 * s, n))
                  for m in range(c // (2 * s))]
        ref = blocks[0] if len(blocks) == 1 else jnp.concatenate(blocks, axis=0)
        out.append(-jnp.abs(b - ref))
        s *= 2
    return out


def _split_hi_lo(g):
    hi = g.astype(BF16)
    lo = (g - hi.astype(F32)).astype(BF16)
    return jnp.concatenate([hi, lo], axis=1)


def _column(row, eye):
    return jnp.sum(eye * row, axis=-1, keepdims=True)


def _mixer_kernel(h_ref, mn_ref, w_ref, wgk_ref, bgk_ref, gn_ref, pw_ref, ps_ref, tri_ref, mk_ref, eye_ref,
                  mix_ref, npool_ref, ngla_ref,
                  xb_s, ext_s, q_s, k_s, v_s, g_s, o_s, a_s, b_s, qin_s, kdec_s, dcy_s, m_s, att_s, gate_s,
                  *, pos0):
    tm, d = h_ref.shape
    heads = GLA_HEADS
    qk = q_s.shape[1]
    dk, dv = qk // heads, d // heads
    c = GLA_CHUNK
    levels = mk_ref.shape[0]
    l = pl.program_id(1)
    o_u, o_q, o_k, o_v = 0, d, d + qk, d + 2 * qk
    o_g = o_v + d
    o_ga, o_gb, o_lr = o_g + d, o_g + 2 * d, o_g + 3 * d

    @pl.when(l == 0)
    def _():
        ext_s[0:HIST, :] = jnp.zeros((HIST, d), F32)
        ngla_ref[...] = jnp.zeros(ngla_ref.shape, F32)

    xb_s[...] = _rms(h_ref[...], mn_ref[...]).astype(BF16)
    xb = xb_s[...]
    ext_s[HIST:HIST + tm, :] = _dot(xb, w_ref[:, o_u:o_u + d])
    q_s[...] = _dot(xb, w_ref[:, o_q:o_q + qk]) * (dk ** -0.5)
    k_s[...] = _dot(xb, w_ref[:, o_k:o_k + qk])
    v_s[...] = _dot(xb, w_ref[:, o_v:o_v + d]).astype(BF16)
    lr = _dot(xb, w_ref[:, o_lr:o_lr + LANE]).astype(BF16)
    g_s[...] = _log_sigmoid(_dot(lr, wgk_ref[...]) + bgk_ref[...]) * LOG2_DECAY_SCALE

    pos = (pos0 + 1 + l * tm + lax.broadcasted_iota(jnp.int32, (tm, 1), 0))
    gc = d // len(POOL_WINDOWS)
    for grp, w in enumerate(POOL_WINDOWS):
        cols = slice(grp * gc, (grp + 1) * gc)
        cur = ext_s[HIST:HIST + tm, cols]
        s = cur
        for dlt in range(1, w):
            s = s + ext_s[HIST - dlt:HIST - dlt + tm, cols]
        cnt = jnp.minimum(pos, w).astype(F32)
        pooled = (s / cnt - cur).astype(BF16)
        a_s[:, cols] = _dot(pooled, pw_ref[grp]) * ps_ref[:, cols]

    @pl.when(l == pl.num_programs(1) - 1)
    def _():
        npool_ref[0] = ext_s[HIST + tm - POOL_BUF:HIST + tm, :]

    ext_s[0:HIST, :] = ext_s[tm:tm + HIST, :]

    n_chunks = tm // c
    gate_blocks = np.array_split(np.arange(3 * d // GATE_BLOCK), n_chunks)

    def head_cols(hd):
        return slice(hd * dk, (hd + 1) * dk), slice(hd * dv, (hd + 1) * dv), slice(hd * c, (hd + 1) * c)

    row = lax.broadcasted_iota(jnp.int32, (c, qk), 0)
    for ci in range(n_chunks):
        r0 = ci * c
        rows = slice(r0, r0 + c)
        g = g_s[rows, :]
        sums = _dot(tri_ref[...], _split_hi_lo(g))
        b = sums[:, :qk] + sums[:, qk:]
        b_s[rows, :] = b
        b_last = b[c - 1:c, :]
        q = q_s[rows, :]
        k = k_s[rows, :]
        qin_s[rows, :] = (q * jnp.exp2(b)).astype(BF16)
        kdec_s[rows, :] = (k * jnp.exp2(b_last - b)).astype(BF16)
        dcy_s[ci * 8:ci * 8 + 1, :] = jnp.exp2(b_last)
        for lv, ex in enumerate(_level_exponents(g, b, b_s, r0)):
            operand = jnp.where((row & (1 << lv)) != 0, q, k) * jnp.exp2(ex)
            m_s[lv, rows, :] = operand.astype(BF16)
        qk_prod = q * k
        for hd in range(heads):
            hc, hv, _ = head_cols(hd)
            o_s[rows, hv] = jnp.sum(qk_prod[:, hc], axis=-1, keepdims=True) * v_s[rows, hv].astype(F32)
        for blk in gate_blocks[ci]:
            lo = int(blk) * GATE_BLOCK
            z = _dot(xb_s[...], w_ref[:, o_g + lo:o_g + lo + GATE_BLOCK])
            gate_s[:, lo:lo + GATE_BLOCK] = _silu(z) if lo < d else _sigmoid(z)

    for ci in range(n_chunks):
        rows = slice(ci * c, (ci + 1) * c)
        for hd in range(heads):
            hc, _, ha = head_cols(hd)
            att = None
            for lv in range(levels):
                operand = m_s[lv, rows, hc]
                t = mk_ref[lv] * _dot_nt(operand, operand)
                att = t if att is None else att + t
            att_s[rows, ha] = att.astype(BF16)

    for ci in range(n_chunks):
        rows = slice(ci * c, (ci + 1) * c)
        for hd in range(heads):
            hc, hv, ha = head_cols(hd)
            state = ngla_ref[0, hd]
            vh = v_s[rows, hv]
            lhs = jnp.concatenate([qin_s[rows, hc], att_s[rows, ha]], axis=1)
            rhs = jnp.concatenate([state.astype(BF16), vh], axis=0)
            o_s[rows, hv] += _dot(lhs, rhs)
            decay = _column(dcy_s[ci * 8:ci * 8 + 1, hc], eye_ref[...])
            ngla_ref[0, hd] = state * decay + _dot_tn(kdec_s[rows, hc], vh)

    gn = gn_ref[...]
    for hd in range(heads):
        hv = slice(hd * dv, (hd + 1) * dv)
        on = _rms(o_s[:, hv], gn)
        gate = gate_s[:, hd * dv:(hd + 1) * dv]
        ga = gate_s[:, d + hd * dv:d + (hd + 1) * dv]
        gb = gate_s[:, 2 * d + hd * dv:2 * d + (hd + 1) * dv]
        mix_ref[:, hv] = (ga * a_s[:, hv] + gb * (on * gate)).astype(BF16)


def _mixer(h, batch, mn, w, wgk, bgk, gn, pw, ps, tri, mk, eye):
    m, d = h.shape
    seq = m // batch
    tm = min(ROW_TILE, seq)
    nl = seq // tm
    qk = bgk.shape[1]
    heads = GLA_HEADS
    dk, dv = qk // heads, d // heads
    c = GLA_CHUNK
    levels = mk.shape[0]
    kern = functools.partial(_mixer_kernel, pos0=0)
    return pl.pallas_call(
        kern,
        grid=(batch, nl),
        in_specs=[pl.BlockSpec((tm, d), lambda b, l: (b * nl + l, 0)),
                  _const_spec(mn.shape), _const_spec(w.shape), _const_spec(wgk.shape), _const_spec(bgk.shape),
                  _const_spec(gn.shape), _const_spec(pw.shape), _const_spec(ps.shape),
                  _const_spec(tri.shape), _const_spec(mk.shape), _const_spec(eye.shape)],
        out_specs=[pl.BlockSpec((tm, d), lambda b, l: (b * nl + l, 0)),
                   pl.BlockSpec((1, POOL_BUF, d), lambda b, l: (b, 0, 0)),
                   pl.BlockSpec((1, heads, dk, dv), lambda b, l: (b, 0, 0, 0))],
        out_shape=[jax.ShapeDtypeStruct((m, d), BF16),
                   jax.ShapeDtypeStruct((batch, POOL_BUF, d), F32),
                   jax.ShapeDtypeStruct((batch, heads, dk, dv), F32)],
        scratch_shapes=[pltpu.VMEM((tm, d), BF16),
                        pltpu.VMEM((HIST + tm, d), F32),
                        pltpu.VMEM((tm, qk), F32),
                        pltpu.VMEM((tm, qk), F32),
                        pltpu.VMEM((tm, d), BF16),
                        pltpu.VMEM((tm, qk), F32),
                        pltpu.VMEM((tm, d), F32),
                        pltpu.VMEM((tm, d), F32),
                        pltpu.VMEM((tm, qk), F32),
                        pltpu.VMEM((tm, qk), BF16),
                        pltpu.VMEM((tm, qk), BF16),
                        pltpu.VMEM((tm // c * 8, qk), F32),
                        pltpu.VMEM((levels, tm, qk), BF16),
                        pltpu.VMEM((tm, heads * c), BF16),
                        pltpu.VMEM((tm, 3 * d), F32)],
        compiler_params=pltpu.CompilerParams(dimension_semantics=("arbitrary", "arbitrary"),
                                             vmem_limit_bytes=VMEM_LIMIT),
        name="mixer_prompt",
    )(h, mn, w, wgk, bgk, gn, pw, ps, tri, mk, eye)


def _proj_kernel(h_ref, mn_ref, w_ref, wgk_ref, bgk_ref, z_ref, g_ref):
    d = h_ref.shape[1]
    qk = g_ref.shape[1]
    dk = qk // GLA_HEADS
    wide = z_ref.shape[1]
    xb = _rms(h_ref[...], mn_ref[...]).astype(BF16)
    z_ref[...] = _dot(xb, w_ref[:, 0:wide])
    z_ref[:, d:d + qk] = z_ref[:, d:d + qk] * (dk ** -0.5)
    lr = _dot(xb, w_ref[:, wide:wide + LANE]).astype(BF16)
    g_ref[...] = _log_sigmoid(_dot(lr, wgk_ref[...]) + bgk_ref[...]) * LOG2_DECAY_SCALE


def _proj(h, mn, w, wgk, bgk):
    m, d = h.shape
    qk = bgk.shape[1]
    wide = w.shape[1] - LANE
    return pl.pallas_call(
        _proj_kernel,
        grid=(1,),
        in_specs=[pl.BlockSpec((m, d), lambda i: (0, 0)),
                  _const_spec(mn.shape), _const_spec(w.shape), _const_spec(wgk.shape), _const_spec(bgk.shape)],
        out_specs=[pl.BlockSpec((m, wide), lambda i: (0, 0)), pl.BlockSpec((m, qk), lambda i: (0, 0))],
        out_shape=[jax.ShapeDtypeStruct((m, wide), F32), jax.ShapeDtypeStruct((m, qk), F32)],
        compiler_params=pltpu.CompilerParams(dimension_semantics=("arbitrary",),
                                             vmem_limit_bytes=VMEM_LIMIT),
        name="proj_sample",
    )(h, mn, w, wgk, bgk)


def _sample_mixer_kernel(z_ref, g_ref, sp_ref, sg_ref, gn_ref, pw_ref, ps_ref, eye_ref,
                         mix_ref, npool_ref, ngla_ref, ext_s, o_s, pl_s, *, n_tok, pos0):
    bb, tp, _ = z_ref.shape
    d = o_s.shape[2]
    heads = GLA_HEADS
    qk = g_ref.shape[2]
    dk, dv = qk // heads, d // heads
    o_u, o_q, o_k, o_v = 0, d, d + qk, d + 2 * qk
    o_g = o_v + d
    o_ga, o_gb = o_g + d, o_g + 2 * d
    row_qk = lax.broadcasted_iota(jnp.int32, (tp, qk), 0)
    row_1 = lax.broadcasted_iota(jnp.int32, (tp, 1), 0)
    gc = d // len(POOL_WINDOWS)

    def seq(n, carry):
        g = g_ref[n]
        b = g + jnp.where(row_qk >= 1, pltpu.roll(g, 1, 0), 0.0)
        b = b + jnp.where(row_qk >= 2, pltpu.roll(b, 2, 0), 0.0)
        b_last = b[n_tok - 1:n_tok, :]
        q = z_ref[n, :, o_q:o_q + qk]
        k = z_ref[n, :, o_k:o_k + qk]
        v = z_ref[n, :, o_v:o_v + d]
        q_in = (q * jnp.exp2(b)).astype(BF16)
        k_dec = (k * jnp.exp2(b_last - b)).astype(BF16)
        e_last = jnp.exp2(b_last)
        vb = v.astype(BF16)
        for hd in range(heads):
            hc = slice(hd * dk, (hd + 1) * dk)
            hv = slice(hd * dv, (hd + 1) * dv)
            state = sg_ref[n, hd]
            o = _dot(q_in[:, hc], state.astype(BF16))
            for j in range(n_tok):
                w_ij = q[:, hc] * k[j:j + 1, hc] * jnp.exp2(jnp.minimum(b[:, hc] - b[j:j + 1, hc], 0.0))
                a_ij = jnp.where(row_1 >= j, jnp.sum(w_ij, axis=-1, keepdims=True), 0.0)
                o = o + a_ij * v[j:j + 1, hv]
            o_s[n, :, hv] = o
            decay = _column(e_last[:, hc], eye_ref[...])
            ngla_ref[n, hd] = state * decay + _dot_tn(k_dec[:, hc], vb[:, hv])
        ext_s[0:HIST, :] = sp_ref[n]
        u = z_ref[n, :, o_u:o_u + d]
        ext_s[HIST:HIST + tp, :] = u
        for grp, w in enumerate(POOL_WINDOWS):
            cols = slice(grp * gc, (grp + 1) * gc)
            s = u[:, cols]
            for dlt in range(1, w):
                s = s + ext_s[HIST - dlt:HIST - dlt + tp, cols]
            cnt = jnp.minimum(pos0 + 1 + row_1, w).astype(F32)
            pl_s[n, :, cols] = s / cnt - u[:, cols]
        npool_ref[n] = ext_s[HIST + n_tok - POOL_BUF:HIST + n_tok, :]
        return carry

    lax.fori_loop(0, bb, seq, 0)

    rows = bb * tp
    gn = gn_ref[...]
    for hd in range(heads):
        hv = slice(hd * dv, (hd + 1) * dv)
        pooled = pl_s[:, :, hv].reshape(rows, dv).astype(BF16)
        a_out = _dot(pooled, pw_ref[hd]) * ps_ref[:, hv]
        on = _rms(o_s[:, :, hv].reshape(rows, dv), gn)
        gate = _silu(z_ref[:, :, o_g + hd * dv:o_g + (hd + 1) * dv].reshape(rows, dv))
        ga = _sigmoid(z_ref[:, :, o_ga + hd * dv:o_ga + (hd + 1) * dv].reshape(rows, dv))
        gb = _sigmoid(z_ref[:, :, o_gb + hd * dv:o_gb + (hd + 1) * dv].reshape(rows, dv))
        mix_ref[:, :, hv] = (ga * a_out + gb * (on * gate)).reshape(bb, tp, dv)


def _sample_mixer(z, g, sp, sg, gn, pw, ps, eye, n_tok):
    nb, tp, wide = z.shape
    qk = g.shape[2]
    d = sp.shape[2]
    heads = GLA_HEADS
    dk, dv = qk // heads, d // heads
    assert d // len(POOL_WINDOWS) == dv, "pool groups and attention heads share a column split here"
    bb = min(SAMPLE_BLOCK, nb)
    kern = functools.partial(_sample_mixer_kernel, n_tok=n_tok, pos0=PAST_LEN)
    blk3 = lambda a, b: pl.BlockSpec((bb, a, b), lambda i: (i, 0, 0))
    return pl.pallas_call(
        kern,
        grid=(nb // bb,),
        in_specs=[blk3(tp, wide), blk3(tp, qk), blk3(HIST, d),
                  pl.BlockSpec((bb, heads, dk, dv), lambda i: (i, 0, 0, 0)),
                  _const_spec(gn.shape), _const_spec(pw.shape), _const_spec(ps.shape), _const_spec(eye.shape)],
        out_specs=[blk3(tp, d), blk3(POOL_BUF, d),
                   pl.BlockSpec((bb, heads, dk, dv), lambda i: (i, 0, 0, 0))],
        out_shape=[jax.ShapeDtypeStruct((nb, tp, d), F32),
                   jax.ShapeDtypeStruct((nb, POOL_BUF, d), F32),
                   jax.ShapeDtypeStruct((nb, heads, dk, dv), F32)],
        scratch_shapes=[pltpu.VMEM((HIST + tp, d), F32),
                        pltpu.VMEM((bb, tp, d), F32),
                        pltpu.VMEM((bb, tp, d), F32)],
        compiler_params=pltpu.CompilerParams(dimension_semantics=("arbitrary",),
                                             vmem_limit_bytes=VMEM_LIMIT),
        name="mixer_sample",
    )(z, g, sp, sg, gn, pw, ps, eye)


def _layer_weights(i, ffn1_norm, ffn1_w_gate, ffn1_w_up, ffn1_w_down, mix_norm, w_in, w_gk_up, b_gk,
                   gla_norm, pool_w, pool_scale, w_out, ffn2_norm, ffn2_w_gate, ffn2_w_up, ffn2_w_down,
                   ple_norm, w_ple_gate, w_ple_proj):
    d = w_in.shape[1]
    qk = w_gk_up.shape[2]
    rank = w_gk_up.shape[1]
    o_lr = 2 * d + 2 * qk
    wi = w_in[i]
    packed = jnp.concatenate(
        [wi[:, :o_lr], wi[:, o_lr + rank:], wi[:, o_lr:o_lr + rank], jnp.zeros((d, LANE - rank), F32)],
        axis=1).astype(BF16)
    wgk = jnp.concatenate([w_gk_up[i], jnp.zeros((LANE - rank, qk), F32)], axis=0).astype(BF16)
    row = lambda a: a[i].reshape(1, -1)
    return dict(
        ffn1=(row(ffn1_norm), ffn1_w_gate[i].astype(BF16), ffn1_w_up[i].astype(BF16), ffn1_w_down[i].astype(BF16)),
        proj=(row(mix_norm), packed, wgk, row(b_gk)),
        mix=(row(gla_norm), pool_w[i].astype(BF16), row(pool_scale)),
        tail=(w_out[i].astype(BF16), row(ffn2_norm), ffn2_w_gate[i].astype(BF16), ffn2_w_up[i].astype(BF16),
              ffn2_w_down[i].astype(BF16), row(ple_norm), w_ple_gate[i].astype(BF16), w_ple_proj[i].astype(BF16)),
    )


def kernel(x_prompt, x_sample, p_prompt, p_sample, state_pool, state_gla, ffn1_norm, ffn1_w_gate, ffn1_w_up, ffn1_w_down, mix_norm, w_in, w_gk_up, b_gk, gla_norm, pool_w, pool_scale, w_out, ffn2_norm, ffn2_w_gate, ffn2_w_up, ffn2_w_down, ple_norm, w_ple_gate, w_ple_proj, final_norm):
    depth = w_in.shape[0]
    batch, seq, d = x_prompt.shape
    nb, n_tok, _ = x_sample.shape
    assert depth == 1, "the final norm is fused into the tail kernel of the only layer"
    assert n_tok <= 4 and POOL_BUF <= HIST and seq % GLA_CHUNK == 0
    tri, mk = _gla_constants(GLA_CHUNK)
    tri = jnp.asarray(tri, BF16)
    mk = jnp.asarray(mk, F32)
    eye = jnp.eye(w_gk_up.shape[2] // GLA_HEADS, dtype=F32)
    fn = final_norm.reshape(1, -1)

    lw = _layer_weights(0, ffn1_norm, ffn1_w_gate, ffn1_w_up, ffn1_w_down, mix_norm, w_in, w_gk_up, b_gk,
                        gla_norm, pool_w, pool_scale, w_out, ffn2_norm, ffn2_w_gate, ffn2_w_up, ffn2_w_down,
                        ple_norm, w_ple_gate, w_ple_proj)
    h1 = _ffn1(x_prompt.reshape(batch * seq, d), *lw["ffn1"])
    mix, pool_p, gla_p = _mixer(h1, batch, *lw["proj"], *lw["mix"], tri, mk, eye)
    y_prompt = _tail(h1, mix, p_prompt[0].reshape(batch * seq, -1), *lw["tail"], fn)
    h1 = _ffn1(x_sample.reshape(nb * n_tok, d), *lw["ffn1"])
    z, g = _proj(h1, *lw["proj"])
    pad3 = lambda a: jnp.pad(a.reshape(nb, n_tok, -1), ((0, 0), (0, SAMPLE_PAD - n_tok), (0, 0)))
    sp = jnp.pad(state_pool[0], ((0, 0), (HIST - POOL_BUF, 0), (0, 0)))
    mix, pool_s, gla_s = _sample_mixer(pad3(z), pad3(g), sp, state_gla[0], *lw["mix"], eye, n_tok)
    mix = mix[:, :n_tok].reshape(nb * n_tok, d).astype(BF16)
    y_sample = _tail(h1, mix, p_sample[0].reshape(nb * n_tok, -1), *lw["tail"], fn)
    return (y_prompt.reshape(batch, seq, d), y_sample.reshape(nb, n_tok, d),
            pool_p[None], gla_p[None], pool_s[None], gla_s[None])
```

```python
import functools

import numpy as np
import jax
import jax.numpy as jnp
from jax import lax
from jax.experimental import pallas as pl
from jax.experimental.pallas import tpu as pltpu

POOL_WINDOWS = (2, 4, 8, 16)
POOL_BUF = max(POOL_WINDOWS) - 1
GLA_HEADS = 4
GATE_NORMALIZER = 16.0
EPS = 1e-6
PAST_LEN = 16384

ROW_TILE = 512
GLA_CHUNK = 128
HIST = 16
POOL_PAD = 16
SAMPLE_PAD = 8
SAMPLE_BLOCK = 8
LANE = 128
GATE_BLOCK = 256
LOG2_DECAY_SCALE = float(np.log2(np.e)) / GATE_NORMALIZER
VMEM_LIMIT = 56 * 1024 * 1024

F32 = jnp.float32
BF16 = jnp.bfloat16


def _dot(a, b):
    return jnp.dot(a, b, preferred_element_type=F32)


def _dot_nt(a, b):
    return lax.dot_general(a, b, (((1,), (1,)), ((), ())), preferred_element_type=F32)


def _dot_tn(a, b):
    return lax.dot_general(a, b, (((0,), (0,)), ((), ())), preferred_element_type=F32)


def _rms(x, w):
    ms = jnp.mean(x * x, axis=-1, keepdims=True)
    return x * lax.rsqrt(ms + EPS) * w


def _sigmoid(x):
    return 1.0 / (1.0 + jnp.exp(-x))


def _silu(x):
    return x * _sigmoid(x)


def _log_sigmoid(x):
    return jnp.minimum(x, 0.0) - jnp.log1p(jnp.exp(-jnp.abs(x)))


def _ff_chunks(f):
    step = 1024
    return [(lo, min(lo + step, f)) for lo in range(0, f, step)]


def _const_spec(shape):
    nd = len(shape)
    return pl.BlockSpec(shape, lambda *_: (0,) * nd, pipeline_mode=pl.Buffered(1))


def _swiglu_acc(xb, wg_ref, wu_ref, wd_ref):
    acc = None
    for lo, hi in _ff_chunks(wg_ref.shape[1]):
        g = _dot(xb, wg_ref[:, lo:hi])
        u = _dot(xb, wu_ref[:, lo:hi])
        a = (_silu(g) * u).astype(BF16)
        d = _dot(a, wd_ref[lo:hi, :])
        acc = d if acc is None else acc + d
    return acc


def _ffn1_kernel(x_ref, nw_ref, wg_ref, wu_ref, wd_ref, o_ref):
    x = x_ref[...]
    xb = _rms(x, nw_ref[...]).astype(BF16)
    o_ref[...] = x + 0.5 * _swiglu_acc(xb, wg_ref, wu_ref, wd_ref)


def _ffn1(x, nw, wg, wu, wd):
    m, d = x.shape
    f = wg.shape[1]
    tm = min(ROW_TILE, m)
    return pl.pallas_call(
        _ffn1_kernel,
        grid=(m // tm,),
        in_specs=[pl.BlockSpec((tm, d), lambda i: (i, 0)),
                  _const_spec((1, d)), _const_spec((d, f)), _const_spec((d, f)), _const_spec((f, d))],
        out_specs=pl.BlockSpec((tm, d), lambda i: (i, 0)),
        out_shape=jax.ShapeDtypeStruct((m, d), F32),
        compiler_params=pltpu.CompilerParams(dimension_semantics=("arbitrary",),
                                             vmem_limit_bytes=VMEM_LIMIT),
        name="ffn1",
    )(x, nw, wg, wu, wd)


def _tail_kernel(h_ref, mix_ref, p_ref, wo_ref, nw_ref, wg_ref, wu_ref, wd_ref,
                 pn_ref, wpg_ref, wpp_ref, fn_ref, o_ref):
    h = h_ref[...] + _dot(mix_ref[...], wo_ref[...])
    xb = _rms(h, nw_ref[...]).astype(BF16)
    h = h + 0.5 * _swiglu_acc(xb, wg_ref, wu_ref, wd_ref)
    gate = _sigmoid(_dot(_rms(h, pn_ref[...]).astype(BF16), wpg_ref[...]))
    h = h + gate * _dot(p_ref[...].astype(BF16), wpp_ref[...])
    o_ref[...] = _rms(h, fn_ref[...])


def _tail(h, mix, p, wo, nw, wg, wu, wd, pn, wpg, wpp, fn):
    m, d = h.shape
    f = wg.shape[1]
    pd = p.shape[1]
    tm = min(ROW_TILE, m)
    row = lambda w: pl.BlockSpec((tm, w), lambda i: (i, 0))
    return pl.pallas_call(
        _tail_kernel,
        grid=(m // tm,),
        in_specs=[row(d), row(d), row(pd),
                  _const_spec((d, d)), _const_spec((1, d)),
                  _const_spec((d, f)), _const_spec((d, f)), _const_spec((f, d)),
                  _const_spec((1, d)), _const_spec((d, d)), _const_spec((pd, d)), _const_spec((1, d))],
        out_specs=row(d),
        out_shape=jax.ShapeDtypeStruct((m, d), F32),
        compiler_params=pltpu.CompilerParams(dimension_semantics=("arbitrary",),
                                             vmem_limit_bytes=VMEM_LIMIT),
        name="tail",
    )(h, mix, p, wo, nw, wg, wu, wd, pn, wpg, wpp, fn)


def _gla_constants(c):
    idx = np.arange(c)
    i, t = idx[:, None], idx[None, :]
    masks = []
    s = 1
    while s < c:
        same = (i // (2 * s)) == (t // (2 * s))
        masks.append(same & ((i % (2 * s)) >= s) & ((t % (2 * s)) < s))
        s *= 2
    tri = (t <= i).astype(np.float32)
    return tri, np.stack([m.astype(np.float32) for m in masks], axis=0)


def _level_exponents(g, b, b_ref, r0):
    c, n = g.shape
    row = lax.broadcasted_iota(jnp.int32, (c, n), 0)
    m4 = row & 3
    g_prev = pltpu.roll(g, 1, 0)
    g_next = pltpu.roll(g, c - 1, 0)
    out = [jnp.where((row & 1) == 1, g, 0.0),
           jnp.where(m4 == 0, g_next, jnp.where(m4 == 1, 0.0, jnp.where(m4 == 2, g, g + g_prev)))]
    s = 4
    while s < c:
        blocks = [jnp.broadcast_to(b_ref[r0 + m * 2 * s + s - 1:r0 + m * 2 * s + s, :], (2 * s, n))
                  for m in range(c // (2 * s))]
        ref = blocks[0] if len(blocks) == 1 else jnp.concatenate(blocks, axis=0)
        sign = jnp.where((lax.broadcasted_iota(jnp.int32, (c, 1), 0) & s) != 0, 1.0, -1.0)
        out.append((b - ref) * sign)
        s *= 2
    return out


def _split_hi_lo(g):
    hi = g.astype(BF16)
    lo = (g - hi.astype(F32)).astype(BF16)
    return jnp.concatenate([hi, lo], axis=1)


def _column(row, eye):
    return jnp.sum(eye * row, axis=-1, keepdims=True)


def _mixer_kernel(h_ref, mn_ref, wa_ref, wb_ref, wlr_ref, wgk_ref, bgk_ref, gn_ref, pw_ref, ps_ref,
                  tri_ref, mk_ref, eye_ref,
                  mix_ref, npool_ref, ngla_ref,
                  xb_s, ext_s, tmp_s, q_s, k_s, v_s, g_s, o_s, a_s, b_s, qin_s, kdec_s, dcy_s, m_s, att_s, gate_s,
                  *, pos0):
    tm, d = h_ref.shape
    heads = GLA_HEADS
    qk = q_s.shape[1]
    dk, dv = qk // heads, d // heads
    c = GLA_CHUNK
    n_chunks = tm // c
    levels = mk_ref.shape[0]
    l = pl.program_id(1)
    x0 = POOL_PAD + HIST
    o_q, o_k, o_v = d, d + qk, d + 2 * qk

    @pl.when(l == 0)
    def _():
        ext_s[0:x0, :] = jnp.zeros((x0, d), F32)
        ngla_ref[...] = jnp.zeros(ngla_ref.shape, F32)

    xb_s[...] = _rms(h_ref[...], mn_ref[...]).astype(BF16)
    q_s[...] = _dot(xb_s[...], wa_ref[:, o_q:o_q + qk]) * (dk ** -0.5)
    k_s[...] = _dot(xb_s[...], wa_ref[:, o_k:o_k + qk])
    lr = _dot(xb_s[...], wlr_ref[...]).astype(BF16)
    g_s[...] = _log_sigmoid(_dot(lr, wgk_ref[...]) + bgk_ref[...]) * LOG2_DECAY_SCALE

    def u_block(lo):
        ext_s[x0:x0 + tm, lo:lo + GATE_BLOCK] = _dot(xb_s[...], wa_ref[:, lo:lo + GATE_BLOCK])

    def v_block(lo):
        v_s[:, lo:lo + GATE_BLOCK] = _dot(xb_s[...], wa_ref[:, o_v + lo:o_v + lo + GATE_BLOCK]).astype(BF16)

    def gate_block(lo):
        z = _dot(xb_s[...], wb_ref[:, lo:lo + GATE_BLOCK])
        gate_s[:, lo:lo + GATE_BLOCK] = _silu(z) if lo < d else _sigmoid(z)

    early_work = ([functools.partial(u_block, lo) for lo in range(0, d, GATE_BLOCK)]
                  + [functools.partial(v_block, lo) for lo in range(0, d, GATE_BLOCK)])
    early_split = np.array_split(np.arange(len(early_work)), n_chunks)
    late_work = [functools.partial(gate_block, lo) for lo in range(0, 3 * d, GATE_BLOCK)]
    late_split = np.array_split(np.arange(len(late_work)), 2 * n_chunks)

    def head_cols(hd):
        return slice(hd * dk, (hd + 1) * dk), slice(hd * dv, (hd + 1) * dv), slice(hd * c, (hd + 1) * c)

    row = lax.broadcasted_iota(jnp.int32, (c, qk), 0)
    for ci in range(n_chunks):
        for idx in early_split[ci]:
            early_work[int(idx)]()
        r0 = ci * c
        rows = slice(r0, r0 + c)
        g = g_s[rows, :]
        sums = _dot(tri_ref[...], _split_hi_lo(g))
        b = sums[:, :qk] + sums[:, qk:]
        b_s[rows, :] = b
        b_last = b[c - 1:c, :]
        q = q_s[rows, :]
        k = k_s[rows, :]
        qin_s[rows, :] = (q * jnp.exp2(b)).astype(BF16)
        kdec_s[rows, :] = (k * jnp.exp2(b_last - b)).astype(BF16)
        dcy_s[ci * 8:ci * 8 + 1, :] = jnp.exp2(b_last)
        for lv, ex in enumerate(_level_exponents(g, b, b_s, r0)):
            operand = jnp.where((row & (1 << lv)) != 0, q, k) * jnp.exp2(ex)
            m_s[lv, rows, :] = operand.astype(BF16)

    pos = (pos0 + 1 + l * tm + lax.broadcasted_iota(jnp.int32, (tm, 1), 0))
    gc = d // len(POOL_WINDOWS)

    def pool_group(grp):
        w = POOL_WINDOWS[grp]
        cols = slice(grp * gc, (grp + 1) * gc)
        shifts = [1 << i for i in range(int(np.log2(w)))]
        starts = [x0] * len(shifts)
        for i in range(len(shifts) - 2, -1, -1):
            starts[i] = (starts[i + 1] - shifts[i + 1]) // 8 * 8
        for i, (sh, st) in enumerate(zip(shifts, starts)):
            if i == 0:
                val = ext_s[st:x0 + tm, cols] + ext_s[st - sh:x0 + tm - sh, cols]
            else:
                val = tmp_s[i - 1, st:x0 + tm, :] + tmp_s[i - 1, st - sh:x0 + tm - sh, :]
            if i < len(shifts) - 1:
                tmp_s[i, st:x0 + tm, :] = val
        cur = ext_s[x0:x0 + tm, cols]
        inv = 1.0 / jnp.minimum(pos, w).astype(F32)
        pooled = (val * inv - cur).astype(BF16)
        a_s[:, cols] = _dot(pooled, pw_ref[grp]) * ps_ref[:, cols]

    pool_split = np.array_split(np.arange(len(POOL_WINDOWS)), n_chunks)
    for ci in range(n_chunks):
        for idx in late_split[ci]:
            late_work[int(idx)]()
        rows = slice(ci * c, (ci + 1) * c)
        for hd in range(heads):
            hc, _, ha = head_cols(hd)
            att = None
            for lv in range(levels):
                operand = m_s[lv, rows, hc]
                t = mk_ref[lv] * _dot_nt(operand, operand)
                att = t if att is None else att + t
            att_s[rows, ha] = att.astype(BF16)
        for grp in pool_split[ci]:
            pool_group(int(grp))

    for ci in range(n_chunks):
        for idx in late_split[n_chunks + ci]:
            late_work[int(idx)]()
        rows = slice(ci * c, (ci + 1) * c)
        qk_prod = q_s[rows, :] * k_s[rows, :]
        for hd in range(heads):
            hc, hv, ha = head_cols(hd)
            state = ngla_ref[0, hd]
            vh = v_s[rows, hv]
            lhs = jnp.concatenate([qin_s[rows, hc], att_s[rows, ha]], axis=1)
            rhs = jnp.concatenate([state.astype(BF16), vh], axis=0)
            diag = jnp.sum(qk_prod[:, hc], axis=-1, keepdims=True)
            o_s[rows, hv] = _dot(lhs, rhs) + diag * vh.astype(F32)
            decay = _column(dcy_s[ci * 8:ci * 8 + 1, hc], eye_ref[...])
            ngla_ref[0, hd] = state * decay + _dot_tn(kdec_s[rows, hc], vh)

    gn = gn_ref[...]
    for hd in range(heads):
        hv = slice(hd * dv, (hd + 1) * dv)
        on = _rms(o_s[:, hv], gn)
        gate = gate_s[:, hd * dv:(hd + 1) * dv]
        ga = gate_s[:, d + hd * dv:d + (hd + 1) * dv]
        gb = gate_s[:, 2 * d + hd * dv:2 * d + (hd + 1) * dv]
        mix_ref[:, hv] = (ga * a_s[:, hv] + gb * (on * gate)).astype(BF16)

    @pl.when(l == pl.num_programs(1) - 1)
    def _():
        npool_ref[0] = ext_s[x0 + tm - POOL_BUF:x0 + tm, :]

    ext_s[POOL_PAD:x0, :] = ext_s[tm + POOL_PAD:tm + x0, :]


def _mixer(h, batch, mn, wa, wb, wlr, wgk, bgk, gn, pw, ps, tri, mk, eye):
    m, d = h.shape
    seq = m // batch
    tm = min(ROW_TILE, seq)
    nl = seq // tm
    qk = bgk.shape[1]
    heads = GLA_HEADS
    dk, dv = qk // heads, d // heads
    c = GLA_CHUNK
    levels = mk.shape[0]
    gc = d // len(POOL_WINDOWS)
    n_stage = int(np.log2(max(POOL_WINDOWS))) - 1
    consts = (mn, wa, wb, wlr, wgk, bgk, gn, pw, ps, tri, mk, eye)
    kern = functools.partial(_mixer_kernel, pos0=0)
    return pl.pallas_call(
        kern,
        grid=(batch, nl),
        in_specs=[pl.BlockSpec((tm, d), lambda b, l: (b * nl + l, 0))] + [_const_spec(a.shape) for a in consts],
        out_specs=[pl.BlockSpec((tm, d), lambda b, l: (b * nl + l, 0)),
                   pl.BlockSpec((1, POOL_BUF, d), lambda b, l: (b, 0, 0)),
                   pl.BlockSpec((1, heads, dk, dv), lambda b, l: (b, 0, 0, 0))],
        out_shape=[jax.ShapeDtypeStruct((m, d), BF16),
                   jax.ShapeDtypeStruct((batch, POOL_BUF, d), F32),
                   jax.ShapeDtypeStruct((batch, heads, dk, dv), F32)],
        scratch_shapes=[pltpu.VMEM((tm, d), BF16),
                        pltpu.VMEM((POOL_PAD + HIST + tm, d), F32),
                        pltpu.VMEM((n_stage, POOL_PAD + HIST + tm, gc), F32),
                        pltpu.VMEM((tm, qk), F32),
                        pltpu.VMEM((tm, qk), F32),
                        pltpu.VMEM((tm, d), BF16),
                        pltpu.VMEM((tm, qk), F32),
                        pltpu.VMEM((tm, d), F32),
                        pltpu.VMEM((tm, d), F32),
                        pltpu.VMEM((tm, qk), F32),
                        pltpu.VMEM((tm, qk), BF16),
                        pltpu.VMEM((tm, qk), BF16),
                        pltpu.VMEM((tm // c * 8, qk), F32),
                        pltpu.VMEM((levels, tm, qk), BF16),
                        pltpu.VMEM((tm, heads * c), BF16),
                        pltpu.VMEM((tm, 3 * d), F32)],
        compiler_params=pltpu.CompilerParams(dimension_semantics=("arbitrary", "arbitrary"),
                                             vmem_limit_bytes=VMEM_LIMIT),
        name="mixer_prompt",
    )(h, *consts)


def _proj_kernel(h_ref, mn_ref, wa_ref, wb_ref, wlr_ref, wgk_ref, bgk_ref, z_ref, g_ref):
    d = h_ref.shape[1]
    qk = g_ref.shape[1]
    dk = qk // GLA_HEADS
    na = wa_ref.shape[1]
    xb = _rms(h_ref[...], mn_ref[...]).astype(BF16)
    z_ref[:, 0:na] = _dot(xb, wa_ref[...])
    z_ref[:, na:] = _dot(xb, wb_ref[...])
    z_ref[:, d:d + qk] = z_ref[:, d:d + qk] * (dk ** -0.5)
    lr = _dot(xb, wlr_ref[...]).astype(BF16)
    g_ref[...] = _log_sigmoid(_dot(lr, wgk_ref[...]) + bgk_ref[...]) * LOG2_DECAY_SCALE


def _proj(h, mn, wa, wb, wlr, wgk, bgk):
    m, d = h.shape
    qk = bgk.shape[1]
    wide = wa.shape[1] + wb.shape[1]
    return pl.pallas_call(
        _proj_kernel,
        grid=(1,),
        in_specs=[pl.BlockSpec((m, d), lambda i: (0, 0)),
                  _const_spec(mn.shape), _const_spec(wa.shape), _const_spec(wb.shape), _const_spec(wlr.shape),
                  _const_spec(wgk.shape), _const_spec(bgk.shape)],
        out_specs=[pl.BlockSpec((m, wide), lambda i: (0, 0)), pl.BlockSpec((m, qk), lambda i: (0, 0))],
        out_shape=[jax.ShapeDtypeStruct((m, wide), F32), jax.ShapeDtypeStruct((m, qk), F32)],
        compiler_params=pltpu.CompilerParams(dimension_semantics=("arbitrary",),
                                             vmem_limit_bytes=VMEM_LIMIT),
        name="proj_sample",
    )(h, mn, wa, wb, wlr, wgk, bgk)


def _sample_mixer_kernel(z_ref, g_ref, sp_ref, sg_ref, gn_ref, pw_ref, ps_ref, eye_ref,
                         mix_ref, npool_ref, ngla_ref, ext_s, o_s, pl_s, *, n_tok, pos0):
    bb, tp, _ = z_ref.shape
    d = o_s.shape[2]
    heads = GLA_HEADS
    qk = g_ref.shape[2]
    dk, dv = qk // heads, d // heads
    o_u, o_q, o_k, o_v = 0, d, d + qk, d + 2 * qk
    o_g = o_v + d
    o_ga, o_gb = o_g + d, o_g + 2 * d
    row_qk = lax.broadcasted_iota(jnp.int32, (tp, qk), 0)
    row_1 = lax.broadcasted_iota(jnp.int32, (tp, 1), 0)
    gc = d // len(POOL_WINDOWS)

    def seq(n, carry):
        g = g_ref[n]
        b = g + jnp.where(row_qk >= 1, pltpu.roll(g, 1, 0), 0.0)
        b = b + jnp.where(row_qk >= 2, pltpu.roll(b, 2, 0), 0.0)
        b_last = b[n_tok - 1:n_tok, :]
        q = z_ref[n, :, o_q:o_q + qk]
        k = z_ref[n, :, o_k:o_k + qk]
        v = z_ref[n, :, o_v:o_v + d]
        q_in = (q * jnp.exp2(b)).astype(BF16)
        k_dec = (k * jnp.exp2(b_last - b)).astype(BF16)
        e_last = jnp.exp2(b_last)
        vb = v.astype(BF16)
        for hd in range(heads):
            hc = slice(hd * dk, (hd + 1) * dk)
            hv = slice(hd * dv, (hd + 1) * dv)
            state = sg_ref[n, hd]
            o = _dot(q_in[:, hc], state.astype(BF16))
            for j in range(n_tok):
                w_ij = q[:, hc] * k[j:j + 1, hc] * jnp.exp2(jnp.minimum(b[:, hc] - b[j:j + 1, hc], 0.0))
                a_ij = jnp.where(row_1 >= j, jnp.sum(w_ij, axis=-1, keepdims=True), 0.0)
                o = o + a_ij * v[j:j + 1, hv]
            o_s[n, :, hv] = o
            decay = _column(e_last[:, hc], eye_ref[...])
            ngla_ref[n, hd] = state * decay + _dot_tn(k_dec[:, hc], vb[:, hv])
        ext_s[0:HIST, :] = sp_ref[n]
        u = z_ref[n, :, o_u:o_u + d]
        ext_s[HIST:HIST + tp, :] = u
        for grp, w in enumerate(POOL_WINDOWS):
            cols = slice(grp * gc, (grp + 1) * gc)
            s = u[:, cols]
            for dlt in range(1, w):
                s = s + ext_s[HIST - dlt:HIST - dlt + tp, cols]
            cnt = jnp.minimum(pos0 + 1 + row_1, w).astype(F32)
            pl_s[n, :, cols] = s / cnt - u[:, cols]
        npool_ref[n] = ext_s[HIST + n_tok - POOL_BUF:HIST + n_tok, :]
        return carry

    lax.fori_loop(0, bb, seq, 0)

    rows = bb * tp
    gn = gn_ref[...]
    for hd in range(heads):
        hv = slice(hd * dv, (hd + 1) * dv)
        pooled = pl_s[:, :, hv].reshape(rows, dv).astype(BF16)
        a_out = _dot(pooled, pw_ref[hd]) * ps_ref[:, hv]
        on = _rms(o_s[:, :, hv].reshape(rows, dv), gn)
        gate = _silu(z_ref[:, :, o_g + hd * dv:o_g + (hd + 1) * dv].reshape(rows, dv))
        ga = _sigmoid(z_ref[:, :, o_ga + hd * dv:o_ga + (hd + 1) * dv].reshape(rows, dv))
        gb = _sigmoid(z_ref[:, :, o_gb + hd * dv:o_gb + (hd + 1) * dv].reshape(rows, dv))
        mix_ref[:, :, hv] = (ga * a_out + gb * (on * gate)).reshape(bb, tp, dv)


def _sample_mixer(z, g, sp, sg, gn, pw, ps, eye, n_tok):
    nb, tp, wide = z.shape
    qk = g.shape[2]
    d = sp.shape[2]
    heads = GLA_HEADS
    dk, dv = qk // heads, d // heads
    assert d // len(POOL_WINDOWS) == dv, "pool groups and attention heads share a column split here"
    bb = min(SAMPLE_BLOCK, nb)
    kern = functools.partial(_sample_mixer_kernel, n_tok=n_tok, pos0=PAST_LEN)
    blk3 = lambda a, b: pl.BlockSpec((bb, a, b), lambda i: (i, 0, 0))
    return pl.pallas_call(
        kern,
        grid=(nb // bb,),
        in_specs=[blk3(tp, wide), blk3(tp, qk), blk3(HIST, d),
                  pl.BlockSpec((bb, heads, dk, dv), lambda i: (i, 0, 0, 0)),
                  _const_spec(gn.shape), _const_spec(pw.shape), _const_spec(ps.shape), _const_spec(eye.shape)],
        out_specs=[blk3(tp, d), blk3(POOL_BUF, d),
                   pl.BlockSpec((bb, heads, dk, dv), lambda i: (i, 0, 0, 0))],
        out_shape=[jax.ShapeDtypeStruct((nb, tp, d), F32),
                   jax.ShapeDtypeStruct((nb, POOL_BUF, d), F32),
                   jax.ShapeDtypeStruct((nb, heads, dk, dv), F32)],
        scratch_shapes=[pltpu.VMEM((HIST + tp, d), F32),
                        pltpu.VMEM((bb, tp, d), F32),
                        pltpu.VMEM((bb, tp, d), F32)],
        compiler_params=pltpu.CompilerParams(dimension_semantics=("arbitrary",),
                                             vmem_limit_bytes=VMEM_LIMIT),
        name="mixer_sample",
    )(z, g, sp, sg, gn, pw, ps, eye)


def _layer_weights(i, ffn1_norm, ffn1_w_gate, ffn1_w_up, ffn1_w_down, mix_norm, w_in, w_gk_up, b_gk,
                   gla_norm, pool_w, pool_scale, w_out, ffn2_norm, ffn2_w_gate, ffn2_w_up, ffn2_w_down,
                   ple_norm, w_ple_gate, w_ple_proj):
    d = w_in.shape[1]
    qk = w_gk_up.shape[2]
    rank = w_gk_up.shape[1]
    o_lr = 2 * d + 2 * qk
    wi = w_in[i]
    wa = wi[:, :o_lr].astype(BF16)
    wb = wi[:, o_lr + rank:].astype(BF16)
    wlr = jnp.pad(wi[:, o_lr:o_lr + rank], ((0, 0), (0, LANE - rank))).astype(BF16)
    wgk = jnp.concatenate([w_gk_up[i], jnp.zeros((LANE - rank, qk), F32)], axis=0).astype(BF16)
    row = lambda a: a[i].reshape(1, -1)
    return dict(
        ffn1=(row(ffn1_norm), ffn1_w_gate[i].astype(BF16), ffn1_w_up[i].astype(BF16), ffn1_w_down[i].astype(BF16)),
        proj=(row(mix_norm), wa, wb, wlr, wgk, row(b_gk)),
        mix=(row(gla_norm), pool_w[i].astype(BF16), row(pool_scale)),
        tail=(w_out[i].astype(BF16), row(ffn2_norm), ffn2_w_gate[i].astype(BF16), ffn2_w_up[i].astype(BF16),
              ffn2_w_down[i].astype(BF16), row(ple_norm), w_ple_gate[i].astype(BF16), w_ple_proj[i].astype(BF16)),
    )


def kernel(x_prompt, x_sample, p_prompt, p_sample, state_pool, state_gla, ffn1_norm, ffn1_w_gate, ffn1_w_up, ffn1_w_down, mix_norm, w_in, w_gk_up, b_gk, gla_norm, pool_w, pool_scale, w_out, ffn2_norm, ffn2_w_gate, ffn2_w_up, ffn2_w_down, ple_norm, w_ple_gate, w_ple_proj, final_norm):
    depth = w_in.shape[0]
    batch, seq, d = x_prompt.shape
    nb, n_tok, _ = x_sample.shape
    assert depth == 1, "the final norm is fused into the tail kernel of the only layer"
    assert n_tok <= 4 and POOL_BUF <= HIST and seq % GLA_CHUNK == 0
    tri, mk = _gla_constants(GLA_CHUNK)
    tri = jnp.asarray(tri, BF16)
    mk = jnp.asarray(mk, F32)
    eye = jnp.eye(w_gk_up.shape[2] // GLA_HEADS, dtype=F32)
    fn = final_norm.reshape(1, -1)

    lw = _layer_weights(0, ffn1_norm, ffn1_w_gate, ffn1_w_up, ffn1_w_down, mix_norm, w_in, w_gk_up, b_gk,
                        gla_norm, pool_w, pool_scale, w_out, ffn2_norm, ffn2_w_gate, ffn2_w_up, ffn2_w_down,
                        ple_norm, w_ple_gate, w_ple_proj)
    h1 = _ffn1(x_prompt.reshape(batch * seq, d), *lw["ffn1"])
    mix, pool_p, gla_p = _mixer(h1, batch, *lw["proj"], *lw["mix"], tri, mk, eye)
    y_prompt = _tail(h1, mix, p_prompt[0].reshape(batch * seq, -1), *lw["tail"], fn)
    h1 = _ffn1(x_sample.reshape(nb * n_tok, d), *lw["ffn1"])
    z, g = _proj(h1, *lw["proj"])
    pad3 = lambda a: jnp.pad(a.reshape(nb, n_tok, -1), ((0, 0), (0, SAMPLE_PAD - n_tok), (0, 0)))
    sp = jnp.pad(state_pool[0], ((0, 0), (HIST - POOL_BUF, 0), (0, 0)))
    mix, pool_s, gla_s = _sample_mixer(pad3(z), pad3(g), sp, state_gla[0], *lw["mix"], eye, n_tok)
    mix = mix[:, :n_tok].reshape(nb * n_tok, d).astype(BF16)
    y_sample = _tail(h1, mix, p_sample[0].reshape(nb * n_tok, -1), *lw["tail"], fn)
    return (y_prompt.reshape(batch, seq, d), y_sample.reshape(nb, n_tok, d),
            pool_p[None], gla_p[None], pool_s[None], gla_s[None])
```

```python
import functools

import numpy as np
import jax
import jax.numpy as jnp
from jax import lax
from jax.experimental import pallas as pl
from jax.experimental.pallas import tpu as pltpu

POOL_WINDOWS = (2, 4, 8, 16)
POOL_BUF = max(POOL_WINDOWS) - 1
GLA_HEADS = 4
GATE_NORMALIZER = 16.0
EPS = 1e-6
PAST_LEN = 16384

ROW_TILE = 512
GLA_CHUNK = 128
HIST = 16
POOL_PAD = 16
SAMPLE_PAD = 8
SAMPLE_BLOCK = 16
LANE = 128
GATE_BLOCK = 256
LOG2_DECAY_SCALE = float(np.log2(np.e)) / GATE_NORMALIZER
VMEM_LIMIT = 56 * 1024 * 1024

F32 = jnp.float32
BF16 = jnp.bfloat16


def _dot(a, b):
    return jnp.dot(a, b, preferred_element_type=F32)


def _dot_nt(a, b):
    return lax.dot_general(a, b, (((1,), (1,)), ((), ())), preferred_element_type=F32)


def _dot_tn(a, b):
    return lax.dot_general(a, b, (((0,), (0,)), ((), ())), preferred_element_type=F32)


def _rms(x, w):
    ms = jnp.mean(x * x, axis=-1, keepdims=True)
    return x * lax.rsqrt(ms + EPS) * w


def _sigmoid(x):
    return 1.0 / (1.0 + jnp.exp(-x))


def _silu(x):
    return x * _sigmoid(x)


def _log_sigmoid(x):
    return jnp.minimum(x, 0.0) - jnp.log1p(jnp.exp(-jnp.abs(x)))


def _ff_chunks(f):
    step = 1024
    return [(lo, min(lo + step, f)) for lo in range(0, f, step)]


def _const_spec(shape):
    nd = len(shape)
    return pl.BlockSpec(shape, lambda *_: (0,) * nd, pipeline_mode=pl.Buffered(1))


def _swiglu_acc(xb, wg_ref, wu_ref, wd_ref):
    acc = None
    for lo, hi in _ff_chunks(wg_ref.shape[1]):
        g = _dot(xb, wg_ref[:, lo:hi])
        u = _dot(xb, wu_ref[:, lo:hi])
        a = (_silu(g) * u).astype(BF16)
        d = _dot(a, wd_ref[lo:hi, :])
        acc = d if acc is None else acc + d
    return acc


def _ffn1_kernel(x_ref, nw_ref, wg_ref, wu_ref, wd_ref, o_ref):
    x = x_ref[...]
    xb = _rms(x, nw_ref[...]).astype(BF16)
    o_ref[...] = x + 0.5 * _swiglu_acc(xb, wg_ref, wu_ref, wd_ref)


def _ffn1(x, nw, wg, wu, wd):
    m, d = x.shape
    f = wg.shape[1]
    tm = min(ROW_TILE, m)
    return pl.pallas_call(
        _ffn1_kernel,
        grid=(m // tm,),
        in_specs=[pl.BlockSpec((tm, d), lambda i: (i, 0)),
                  _const_spec((1, d)), _const_spec((d, f)), _const_spec((d, f)), _const_spec((f, d))],
        out_specs=pl.BlockSpec((tm, d), lambda i: (i, 0)),
        out_shape=jax.ShapeDtypeStruct((m, d), F32),
        compiler_params=pltpu.CompilerParams(dimension_semantics=("arbitrary",),
                                             vmem_limit_bytes=VMEM_LIMIT),
        name="ffn1",
    )(x, nw, wg, wu, wd)


def _tail_kernel(h_ref, mix_ref, p_ref, wo_ref, nw_ref, wg_ref, wu_ref, wd_ref,
                 pn_ref, wpg_ref, wpp_ref, fn_ref, o_ref):
    h = h_ref[...] + _dot(mix_ref[...], wo_ref[...])
    xb = _rms(h, nw_ref[...]).astype(BF16)
    h = h + 0.5 * _swiglu_acc(xb, wg_ref, wu_ref, wd_ref)
    gate = _sigmoid(_dot(_rms(h, pn_ref[...]).astype(BF16), wpg_ref[...]))
    h = h + gate * _dot(p_ref[...].astype(BF16), wpp_ref[...])
    o_ref[...] = _rms(h, fn_ref[...])


def _tail(h, mix, p, wo, nw, wg, wu, wd, pn, wpg, wpp, fn):
    m, d = h.shape
    f = wg.shape[1]
    pd = p.shape[1]
    tm = min(ROW_TILE, m)
    row = lambda w: pl.BlockSpec((tm, w), lambda i: (i, 0))
    return pl.pallas_call(
        _tail_kernel,
        grid=(m // tm,),
        in_specs=[row(d), row(d), row(pd),
                  _const_spec((d, d)), _const_spec((1, d)),
                  _const_spec((d, f)), _const_spec((d, f)), _const_spec((f, d)),
                  _const_spec((1, d)), _const_spec((d, d)), _const_spec((pd, d)), _const_spec((1, d))],
        out_specs=row(d),
        out_shape=jax.ShapeDtypeStruct((m, d), F32),
        compiler_params=pltpu.CompilerParams(dimension_semantics=("arbitrary",),
                                             vmem_limit_bytes=VMEM_LIMIT),
        name="tail",
    )(h, mix, p, wo, nw, wg, wu, wd, pn, wpg, wpp, fn)


def _gla_constants(c):
    idx = np.arange(c)
    i, t = idx[:, None], idx[None, :]
    masks = []
    s = 1
    while s < c:
        same = (i // (2 * s)) == (t // (2 * s))
        masks.append(same & ((i % (2 * s)) >= s) & ((t % (2 * s)) < s))
        s *= 2
    tri = (t <= i).astype(np.float32)
    return tri, np.stack([m.astype(np.float32) for m in masks], axis=0)


def _level_exponents(g, b, b_ref, r0):
    c, n = g.shape
    row = lax.broadcasted_iota(jnp.int32, (c, n), 0)
    m4 = row & 3
    g_prev = pltpu.roll(g, 1, 0)
    g_next = pltpu.roll(g, c - 1, 0)
    out = [jnp.where((row & 1) == 1, g, 0.0),
           jnp.where(m4 == 0, g_next, jnp.where(m4 == 1, 0.0, jnp.where(m4 == 2, g, g + g_prev)))]
    s = 4
    while s < c:
        blocks = [jnp.broadcast_to(b_ref[r0 + m * 2 * s + s - 1:r0 + m * 2 * s + s, :], (2 * s, n))
                  for m in range(c // (2 * s))]
        ref = blocks[0] if len(blocks) == 1 else jnp.concatenate(blocks, axis=0)
        sign = jnp.where((lax.broadcasted_iota(jnp.int32, (c, 1), 0) & s) != 0, 1.0, -1.0)
        out.append((b - ref) * sign)
        s *= 2
    return out


def _split_hi_lo(g):
    hi = g.astype(BF16)
    lo = (g - hi.astype(F32)).astype(BF16)
    return jnp.concatenate([hi, lo], axis=1)


def _column(row, eye):
    return jnp.sum(eye * row, axis=-1, keepdims=True)


def _mixer_kernel(h_ref, mn_ref, wa_ref, wb_ref, wlr_ref, wgk_ref, bgk_ref, gn_ref, pw_ref, ps_ref,
                  tri_ref, mk_ref, eye_ref,
                  mix_ref, npool_ref, ngla_ref,
                  xb_s, ext_s, tmp_s, q_s, k_s, v_s, g_s, o_s, a_s, b_s, qin_s, kdec_s, dcy_s, m_s, att_s, gate_s,
                  *, pos0):
    tm, d = h_ref.shape
    heads = GLA_HEADS
    qk = q_s.shape[1]
    dk, dv = qk // heads, d // heads
    c = GLA_CHUNK
    n_chunks = tm // c
    levels = mk_ref.shape[0]
    l = pl.program_id(1)
    x0 = POOL_PAD + HIST
    o_q, o_k, o_v = d, d + qk, d + 2 * qk

    @pl.when(l == 0)
    def _():
        ext_s[0:x0, :] = jnp.zeros((x0, d), F32)
        ngla_ref[...] = jnp.zeros(ngla_ref.shape, F32)

    xb_s[...] = _rms(h_ref[...], mn_ref[...]).astype(BF16)
    q_s[...] = _dot(xb_s[...], wa_ref[:, o_q:o_q + qk]) * (dk ** -0.5)
    k_s[...] = _dot(xb_s[...], wa_ref[:, o_k:o_k + qk])
    lr = _dot(xb_s[...], wlr_ref[...]).astype(BF16)
    g_s[...] = _log_sigmoid(_dot(lr, wgk_ref[...]) + bgk_ref[...]) * LOG2_DECAY_SCALE

    def u_block(lo):
        ext_s[x0:x0 + tm, lo:lo + GATE_BLOCK] = _dot(xb_s[...], wa_ref[:, lo:lo + GATE_BLOCK])

    def v_block(lo):
        v_s[:, lo:lo + GATE_BLOCK] = _dot(xb_s[...], wa_ref[:, o_v + lo:o_v + lo + GATE_BLOCK]).astype(BF16)

    def gate_block(lo):
        z = _dot(xb_s[...], wb_ref[:, lo:lo + GATE_BLOCK])
        gate_s[:, lo:lo + GATE_BLOCK] = _silu(z) if lo < d else _sigmoid(z)

    early_work = ([functools.partial(u_block, lo) for lo in range(0, d, GATE_BLOCK)]
                  + [functools.partial(v_block, lo) for lo in range(0, d, GATE_BLOCK)])
    early_split = np.array_split(np.arange(len(early_work)), n_chunks)
    late_work = [functools.partial(gate_block, lo) for lo in range(0, 3 * d, GATE_BLOCK)]
    late_split = np.array_split(np.arange(len(late_work)), 2 * n_chunks)

    def head_cols(hd):
        return slice(hd * dk, (hd + 1) * dk), slice(hd * dv, (hd + 1) * dv), slice(hd * c, (hd + 1) * c)

    row = lax.broadcasted_iota(jnp.int32, (c, qk), 0)
    for ci in range(n_chunks):
        for idx in early_split[ci]:
            early_work[int(idx)]()
        r0 = ci * c
        rows = slice(r0, r0 + c)
        g = g_s[rows, :]
        sums = _dot(tri_ref[...], _split_hi_lo(g))
        b = sums[:, :qk] + sums[:, qk:]
        b_s[rows, :] = b
        b_last = b[c - 1:c, :]
        q = q_s[rows, :]
        k = k_s[rows, :]
        qin_s[rows, :] = (q * jnp.exp2(b)).astype(BF16)
        kdec_s[rows, :] = (k * jnp.exp2(b_last - b)).astype(BF16)
        dcy_s[ci * 8:ci * 8 + 1, :] = jnp.exp2(b_last)
        for lv, ex in enumerate(_level_exponents(g, b, b_s, r0)):
            operand = jnp.where((row & (1 << lv)) != 0, q, k) * jnp.exp2(ex)
            m_s[lv, rows, :] = operand.astype(BF16)

    pos = (pos0 + 1 + l * tm + lax.broadcasted_iota(jnp.int32, (tm, 1), 0))
    gc = d // len(POOL_WINDOWS)

    def pool_group(grp):
        w = POOL_WINDOWS[grp]
        cols = slice(grp * gc, (grp + 1) * gc)
        shifts = [1 << i for i in range(int(np.log2(w)))]
        starts = [x0] * len(shifts)
        for i in range(len(shifts) - 2, -1, -1):
            starts[i] = (starts[i + 1] - shifts[i + 1]) // 8 * 8
        for i, (sh, st) in enumerate(zip(shifts, starts)):
            if i == 0:
                val = ext_s[st:x0 + tm, cols] + ext_s[st - sh:x0 + tm - sh, cols]
            else:
                val = tmp_s[i - 1, st:x0 + tm, :] + tmp_s[i - 1, st - sh:x0 + tm - sh, :]
            if i < len(shifts) - 1:
                tmp_s[i, st:x0 + tm, :] = val
        cur = ext_s[x0:x0 + tm, cols]
        inv = 1.0 / jnp.minimum(pos, w).astype(F32)
        pooled = (val * inv - cur).astype(BF16)
        a_s[:, cols] = _dot(pooled, pw_ref[grp]) * ps_ref[:, cols]

    pool_split = np.array_split(np.arange(len(POOL_WINDOWS)), n_chunks)
    for ci in range(n_chunks):
        for idx in late_split[ci]:
            late_work[int(idx)]()
        rows = slice(ci * c, (ci + 1) * c)
        for hd in range(heads):
            hc, _, ha = head_cols(hd)
            att = None
            for lv in range(levels):
                operand = m_s[lv, rows, hc]
                t = mk_ref[lv] * _dot_nt(operand, operand)
                att = t if att is None else att + t
            att_s[rows, ha] = att.astype(BF16)
        for grp in pool_split[ci]:
            pool_group(int(grp))

    for ci in range(n_chunks):
        for idx in late_split[n_chunks + ci]:
            late_work[int(idx)]()
        rows = slice(ci * c, (ci + 1) * c)
        qk_prod = q_s[rows, :] * k_s[rows, :]
        for hd in range(heads):
            hc, hv, ha = head_cols(hd)
            state = ngla_ref[0, hd]
            vh = v_s[rows, hv]
            lhs = jnp.concatenate([qin_s[rows, hc], att_s[rows, ha]], axis=1)
            rhs = jnp.concatenate([state.astype(BF16), vh], axis=0)
            diag = jnp.sum(qk_prod[:, hc], axis=-1, keepdims=True)
            o_s[rows, hv] = _dot(lhs, rhs) + diag * vh.astype(F32)
            decay = _column(dcy_s[ci * 8:ci * 8 + 1, hc], eye_ref[...])
            ngla_ref[0, hd] = state * decay + _dot_tn(kdec_s[rows, hc], vh)

    gn = gn_ref[...]
    for hd in range(heads):
        hv = slice(hd * dv, (hd + 1) * dv)
        on = _rms(o_s[:, hv], gn)
        gate = gate_s[:, hd * dv:(hd + 1) * dv]
        ga = gate_s[:, d + hd * dv:d + (hd + 1) * dv]
        gb = gate_s[:, 2 * d + hd * dv:2 * d + (hd + 1) * dv]
        mix_ref[:, hv] = (ga * a_s[:, hv] + gb * (on * gate)).astype(BF16)

    @pl.when(l == pl.num_programs(1) - 1)
    def _():
        npool_ref[0] = ext_s[x0 + tm - POOL_BUF:x0 + tm, :]

    ext_s[POOL_PAD:x0, :] = ext_s[tm + POOL_PAD:tm + x0, :]


def _mixer(h, batch, mn, wa, wb, wlr, wgk, bgk, gn, pw, ps, tri, mk, eye):
    m, d = h.shape
    seq = m // batch
    tm = min(ROW_TILE, seq)
    nl = seq // tm
    qk = bgk.shape[1]
    heads = GLA_HEADS
    dk, dv = qk // heads, d // heads
    c = GLA_CHUNK
    levels = mk.shape[0]
    gc = d // len(POOL_WINDOWS)
    n_stage = int(np.log2(max(POOL_WINDOWS))) - 1
    consts = (mn, wa, wb, wlr, wgk, bgk, gn, pw, ps, tri, mk, eye)
    kern = functools.partial(_mixer_kernel, pos0=0)
    return pl.pallas_call(
        kern,
        grid=(batch, nl),
        in_specs=[pl.BlockSpec((tm, d), lambda b, l: (b * nl + l, 0))] + [_const_spec(a.shape) for a in consts],
        out_specs=[pl.BlockSpec((tm, d), lambda b, l: (b * nl + l, 0)),
                   pl.BlockSpec((1, POOL_BUF, d), lambda b, l: (b, 0, 0)),
                   pl.BlockSpec((1, heads, dk, dv), lambda b, l: (b, 0, 0, 0))],
        out_shape=[jax.ShapeDtypeStruct((m, d), BF16),
                   jax.ShapeDtypeStruct((batch, POOL_BUF, d), F32),
                   jax.ShapeDtypeStruct((batch, heads, dk, dv), F32)],
        scratch_shapes=[pltpu.VMEM((tm, d), BF16),
                        pltpu.VMEM((POOL_PAD + HIST + tm, d), F32),
                        pltpu.VMEM((n_stage, POOL_PAD + HIST + tm, gc), F32),
                        pltpu.VMEM((tm, qk), F32),
                        pltpu.VMEM((tm, qk), F32),
                        pltpu.VMEM((tm, d), BF16),
                        pltpu.VMEM((tm, qk), F32),
                        pltpu.VMEM((tm, d), F32),
                        pltpu.VMEM((tm, d), F32),
                        pltpu.VMEM((tm, qk), F32),
                        pltpu.VMEM((tm, qk), BF16),
                        pltpu.VMEM((tm, qk), BF16),
                        pltpu.VMEM((tm // c * 8, qk), F32),
                        pltpu.VMEM((levels, tm, qk), BF16),
                        pltpu.VMEM((tm, heads * c), BF16),
                        pltpu.VMEM((tm, 3 * d), F32)],
        compiler_params=pltpu.CompilerParams(dimension_semantics=("arbitrary", "arbitrary"),
                                             vmem_limit_bytes=VMEM_LIMIT),
        name="mixer_prompt",
    )(h, *consts)


def _proj_kernel(h_ref, mn_ref, wa_ref, wb_ref, wlr_ref, wgk_ref, bgk_ref, z_ref, g_ref):
    d = h_ref.shape[1]
    qk = g_ref.shape[1]
    dk = qk // GLA_HEADS
    na = wa_ref.shape[1]
    xb = _rms(h_ref[...], mn_ref[...]).astype(BF16)
    z_ref[:, 0:na] = _dot(xb, wa_ref[...])
    z_ref[:, na:] = _dot(xb, wb_ref[...])
    z_ref[:, d:d + qk] = z_ref[:, d:d + qk] * (dk ** -0.5)
    lr = _dot(xb, wlr_ref[...]).astype(BF16)
    g_ref[...] = _log_sigmoid(_dot(lr, wgk_ref[...]) + bgk_ref[...]) * LOG2_DECAY_SCALE


def _proj(h, mn, wa, wb, wlr, wgk, bgk):
    m, d = h.shape
    qk = bgk.shape[1]
    wide = wa.shape[1] + wb.shape[1]
    return pl.pallas_call(
        _proj_kernel,
        grid=(1,),
        in_specs=[pl.BlockSpec((m, d), lambda i: (0, 0)),
                  _const_spec(mn.shape), _const_spec(wa.shape), _const_spec(wb.shape), _const_spec(wlr.shape),
                  _const_spec(wgk.shape), _const_spec(bgk.shape)],
        out_specs=[pl.BlockSpec((m, wide), lambda i: (0, 0)), pl.BlockSpec((m, qk), lambda i: (0, 0))],
        out_shape=[jax.ShapeDtypeStruct((m, wide), F32), jax.ShapeDtypeStruct((m, qk), F32)],
        compiler_params=pltpu.CompilerParams(dimension_semantics=("arbitrary",),
                                             vmem_limit_bytes=VMEM_LIMIT),
        name="proj_sample",
    )(h, mn, wa, wb, wlr, wgk, bgk)


def _sample_mixer_kernel(z_ref, g_ref, sp_ref, sg_ref, gn_ref, pw_ref, ps_ref, eye_ref,
                         mix_ref, npool_ref, ngla_ref,
                         qin_s, kdec_s, v_s, o_s, pl_s, ost_s, *, pos0):
    n_tok, bb, _ = z_ref.shape
    d = o_s.shape[1]
    heads = GLA_HEADS
    qk = g_ref.shape[2]
    dk, dv = qk // heads, d // heads
    o_u, o_q, o_k, o_v = 0, d, d + qk, d + 2 * qk
    o_g = o_v + d
    o_ga, o_gb = o_g + d, o_g + 2 * d
    gc = d // len(POOL_WINDOWS)
    plane = lambda t: slice(t * bb, (t + 1) * bb)

    @pl.when(pl.program_id(0) == 0)
    def _():
        pad = (SAMPLE_PAD - n_tok) * bb
        qin_s[:, (n_tok + 1) * bb:, :] = jnp.zeros((qk // LANE, pad - bb, LANE), F32)
        kdec_s[:, n_tok * bb:, :] = jnp.zeros((qk // LANE, pad, LANE), F32)
        v_s[:, n_tok * bb:, :] = jnp.zeros((d // LANE, pad, LANE), F32)

    b = []
    for t in range(n_tok):
        b.append(g_ref[t] if t == 0 else b[-1] + g_ref[t])
    b_last = b[-1]
    q = [z_ref[t, :, o_q:o_q + qk] for t in range(n_tok)]
    k = [z_ref[t, :, o_k:o_k + qk] for t in range(n_tok)]
    def stage(dst, t, val):
        for i in range(val.shape[1] // LANE):
            dst[i, plane(t), :] = val[:, i * LANE:(i + 1) * LANE]

    for t in range(n_tok):
        stage(qin_s, t, q[t] * jnp.exp2(b[t]))
        stage(kdec_s, t, k[t] * jnp.exp2(b_last - b[t]))
        stage(v_s, t, z_ref[t, :, o_v:o_v + d])
    stage(qin_s, n_tok, jnp.exp2(b_last))
    for t in range(n_tok):
        o_t = [None] * heads
        for j in range(t + 1):
            w_tj = q[t] * k[j] if j == t else q[t] * k[j] * jnp.exp2(b[t] - b[j])
            for hd in range(heads):
                a_tj = jnp.sum(w_tj[:, hd * dk:(hd + 1) * dk], axis=-1, keepdims=True)
                term = a_tj * z_ref[j, :, o_v + hd * dv:o_v + (hd + 1) * dv]
                o_t[hd] = term if o_t[hd] is None else o_t[hd] + term
        for hd in range(heads):
            o_s[plane(t), hd * dv:(hd + 1) * dv] = o_t[hd]

    def seq(n, carry):
        tokens = pl.ds(n, SAMPLE_PAD, stride=bb)
        own_rows = pl.ds(pl.multiple_of(n * SAMPLE_PAD, SAMPLE_PAD), SAMPLE_PAD)
        for hd in range(heads):
            state = sg_ref[n, hd]
            q_tok = qin_s[hd, tokens, :]
            o_state = _dot(q_tok.astype(BF16), state.astype(BF16))
            for i in range(dv // LANE):
                ost_s[hd * (dv // LANE) + i, own_rows, :] = o_state[:, i * LANE:(i + 1) * LANE]
            decay = _column(q_tok[n_tok:n_tok + 1, :], eye_ref[...])
            v_tok = jnp.concatenate([v_s[hd * (dv // LANE) + i, tokens, :] for i in range(dv // LANE)], axis=1)
            ngla_ref[n, hd] = state * decay + _dot_tn(kdec_s[hd, tokens, :].astype(BF16), v_tok.astype(BF16))
        return carry

    lax.fori_loop(0, bb, seq, 0, unroll=2)
    for t in range(n_tok):
        for i in range(d // LANE):
            o_s[plane(t), i * LANE:(i + 1) * LANE] += ost_s[i, pl.ds(t, bb, stride=SAMPLE_PAD), :]

    def pool_row(j):
        return z_ref[j, :, o_u:o_u + d] if j >= 0 else sp_ref[POOL_BUF + j]

    for t in range(n_tok):
        for grp, w in enumerate(POOL_WINDOWS):
            cols = slice(grp * gc, (grp + 1) * gc)
            s = pool_row(t)[:, cols]
            for dlt in range(1, w):
                s = s + pool_row(t - dlt)[:, cols]
            cnt = float(min(pos0 + 1 + t, w))
            pl_s[plane(t), cols] = s * (1.0 / cnt) - pool_row(t)[:, cols]
    for r in range(POOL_BUF):
        npool_ref[r] = pool_row(r + n_tok - POOL_BUF)

    rows = n_tok * bb
    gn = gn_ref[...]
    for hd in range(heads):
        hv = slice(hd * dv, (hd + 1) * dv)
        zcols = lambda off: z_ref[:, :, off + hd * dv:off + (hd + 1) * dv].reshape(rows, dv)
        a_out = _dot(pl_s[:, hv].astype(BF16), pw_ref[hd]) * ps_ref[:, hv]
        on = _rms(o_s[:, hv], gn)
        mix = _sigmoid(zcols(o_ga)) * a_out + _sigmoid(zcols(o_gb)) * (on * _silu(zcols(o_g)))
        mix_ref[:, :, hv] = mix.reshape(n_tok, bb, dv).astype(mix_ref.dtype)


def _sample_mixer(z, g, sp, sg, gn, pw, ps, eye):
    n_tok, nb, wide = z.shape
    qk = g.shape[2]
    d = sp.shape[2]
    heads = GLA_HEADS
    dk, dv = qk // heads, d // heads
    assert d // len(POOL_WINDOWS) == dv, "pool groups and attention heads share a column split here"
    assert n_tok < SAMPLE_PAD and dk == LANE and dv % LANE == 0
    bb = min(SAMPLE_BLOCK, nb)
    kern = functools.partial(_sample_mixer_kernel, pos0=PAST_LEN)
    blk3 = lambda a, b: pl.BlockSpec((a, bb, b), lambda i: (0, i, 0))
    state_spec = pl.BlockSpec((bb, heads, dk, dv), lambda i: (i, 0, 0, 0))
    return pl.pallas_call(
        kern,
        grid=(nb // bb,),
        in_specs=[blk3(n_tok, wide), blk3(n_tok, qk), blk3(POOL_BUF, d), state_spec,
                  _const_spec(gn.shape), _const_spec(pw.shape), _const_spec(ps.shape), _const_spec(eye.shape)],
        out_specs=[blk3(n_tok, d), blk3(POOL_BUF, d), state_spec],
        out_shape=[jax.ShapeDtypeStruct((n_tok, nb, d), BF16),
                   jax.ShapeDtypeStruct((POOL_BUF, nb, d), F32),
                   jax.ShapeDtypeStruct((nb, heads, dk, dv), F32)],
        scratch_shapes=[pltpu.VMEM((qk // LANE, SAMPLE_PAD * bb, LANE), F32),
                        pltpu.VMEM((qk // LANE, SAMPLE_PAD * bb, LANE), F32),
                        pltpu.VMEM((d // LANE, SAMPLE_PAD * bb, LANE), F32),
                        pltpu.VMEM((n_tok * bb, d), F32),
                        pltpu.VMEM((n_tok * bb, d), F32),
                        pltpu.VMEM((d // LANE, SAMPLE_PAD * bb, LANE), F32)],
        compiler_params=pltpu.CompilerParams(dimension_semantics=("arbitrary",),
                                             vmem_limit_bytes=VMEM_LIMIT),
        name="mixer_sample",
    )(z, g, sp, sg, gn, pw, ps, eye)


def _layer_weights(i, ffn1_norm, ffn1_w_gate, ffn1_w_up, ffn1_w_down, mix_norm, w_in, w_gk_up, b_gk,
                   gla_norm, pool_w, pool_scale, w_out, ffn2_norm, ffn2_w_gate, ffn2_w_up, ffn2_w_down,
                   ple_norm, w_ple_gate, w_ple_proj):
    d = w_in.shape[1]
    qk = w_gk_up.shape[2]
    rank = w_gk_up.shape[1]
    o_lr = 2 * d + 2 * qk
    wi = w_in[i]
    wa = wi[:, :o_lr].astype(BF16)
    wb = wi[:, o_lr + rank:].astype(BF16)
    wlr = jnp.pad(wi[:, o_lr:o_lr + rank], ((0, 0), (0, LANE - rank))).astype(BF16)
    wgk = jnp.concatenate([w_gk_up[i], jnp.zeros((LANE - rank, qk), F32)], axis=0).astype(BF16)
    row = lambda a: a[i].reshape(1, -1)
    return dict(
        ffn1=(row(ffn1_norm), ffn1_w_gate[i].astype(BF16), ffn1_w_up[i].astype(BF16), ffn1_w_down[i].astype(BF16)),
        proj=(row(mix_norm), wa, wb, wlr, wgk, row(b_gk)),
        mix=(row(gla_norm), pool_w[i].astype(BF16), row(pool_scale)),
        tail=(w_out[i].astype(BF16), row(ffn2_norm), ffn2_w_gate[i].astype(BF16), ffn2_w_up[i].astype(BF16),
              ffn2_w_down[i].astype(BF16), row(ple_norm), w_ple_gate[i].astype(BF16), w_ple_proj[i].astype(BF16)),
    )


def kernel(x_prompt, x_sample, p_prompt, p_sample, state_pool, state_gla, ffn1_norm, ffn1_w_gate, ffn1_w_up, ffn1_w_down, mix_norm, w_in, w_gk_up, b_gk, gla_norm, pool_w, pool_scale, w_out, ffn2_norm, ffn2_w_gate, ffn2_w_up, ffn2_w_down, ple_norm, w_ple_gate, w_ple_proj, final_norm):
    depth = w_in.shape[0]
    batch, seq, d = x_prompt.shape
    nb, n_tok, _ = x_sample.shape
    assert depth == 1, "the final norm is fused into the tail kernel of the only layer"
    assert n_tok <= 4 and POOL_BUF <= HIST and seq % GLA_CHUNK == 0
    tri, mk = _gla_constants(GLA_CHUNK)
    tri = jnp.asarray(tri, BF16)
    mk = jnp.asarray(mk, F32)
    eye = jnp.eye(w_gk_up.shape[2] // GLA_HEADS, dtype=F32)
    fn = final_norm.reshape(1, -1)

    lw = _layer_weights(0, ffn1_norm, ffn1_w_gate, ffn1_w_up, ffn1_w_down, mix_norm, w_in, w_gk_up, b_gk,
                        gla_norm, pool_w, pool_scale, w_out, ffn2_norm, ffn2_w_gate, ffn2_w_up, ffn2_w_down,
                        ple_norm, w_ple_gate, w_ple_proj)
    h1 = _ffn1(x_prompt.reshape(batch * seq, d), *lw["ffn1"])
    mix, pool_p, gla_p = _mixer(h1, batch, *lw["proj"], *lw["mix"], tri, mk, eye)
    y_prompt = _tail(h1, mix, p_prompt[0].reshape(batch * seq, -1), *lw["tail"], fn)
    tok_major = lambda a: jnp.swapaxes(a, 0, 1)
    h1 = _ffn1(tok_major(x_sample).reshape(n_tok * nb, d), *lw["ffn1"])
    z, g = _proj(h1, *lw["proj"])
    mix, pool_s, gla_s = _sample_mixer(z.reshape(n_tok, nb, -1), g.reshape(n_tok, nb, -1),
                                       tok_major(state_pool[0]), state_gla[0], *lw["mix"], eye)
    y_sample = _tail(h1, mix.reshape(n_tok * nb, d), tok_major(p_sample[0]).reshape(n_tok * nb, -1),
                     *lw["tail"], fn)
    return (y_prompt.reshape(batch, seq, d), tok_major(y_sample.reshape(n_tok, nb, d)),
            pool_p[None], gla_p[None], tok_major(pool_s)[None], gla_s[None])
```

```python
import functools

import numpy as np
import jax
import jax.numpy as jnp
from jax import lax
from jax.experimental import pallas as pl
from jax.experimental.pallas import tpu as pltpu

POOL_WINDOWS = (2, 4, 8, 16)
POOL_BUF = max(POOL_WINDOWS) - 1
GLA_HEADS = 4
GATE_NORMALIZER = 16.0
EPS = 1e-6
PAST_LEN = 16384

ROW_TILE = 512
GLA_CHUNK = 128
HIST = 16
POOL_PAD = 16
SAMPLE_PAD = 8
SAMPLE_BLOCK = 16
LANE = 128
GATE_BLOCK = 256
LOG2_DECAY_SCALE = float(np.log2(np.e)) / GATE_NORMALIZER
VMEM_LIMIT = 56 * 1024 * 1024

F32 = jnp.float32
BF16 = jnp.bfloat16


def _dot(a, b):
    return jnp.dot(a, b, preferred_element_type=F32)


def _dot_nt(a, b):
    return lax.dot_general(a, b, (((1,), (1,)), ((), ())), preferred_element_type=F32)


def _dot_tn(a, b):
    return lax.dot_general(a, b, (((0,), (0,)), ((), ())), preferred_element_type=F32)


def _rms(x, w):
    ms = jnp.mean(x * x, axis=-1, keepdims=True)
    return x * lax.rsqrt(ms + EPS) * w


def _sigmoid(x):
    return 1.0 / (1.0 + jnp.exp(-x))


def _silu(x):
    return x * _sigmoid(x)


def _log_sigmoid(x):
    return jnp.minimum(x, 0.0) - jnp.log1p(jnp.exp(-jnp.abs(x)))


def _ff_chunks(f):
    step = 1024
    return [(lo, min(lo + step, f)) for lo in range(0, f, step)]


def _const_spec(shape):
    nd = len(shape)
    return pl.BlockSpec(shape, lambda *_: (0,) * nd, pipeline_mode=pl.Buffered(1))


def _swiglu_acc(xb, wg_ref, wu_ref, wd_ref):
    acc = None
    for lo, hi in _ff_chunks(wg_ref.shape[1]):
        g = _dot(xb, wg_ref[:, lo:hi])
        u = _dot(xb, wu_ref[:, lo:hi])
        a = (_silu(g) * u).astype(BF16)
        d = _dot(a, wd_ref[lo:hi, :])
        acc = d if acc is None else acc + d
    return acc


def _ffn1_kernel(x_ref, nw_ref, wg_ref, wu_ref, wd_ref, o_ref):
    half = x_ref.shape[0] // 2
    for r in (slice(0, half), slice(half, 2 * half)):
        x = x_ref[r, :]
        xb = _rms(x, nw_ref[...]).astype(BF16)
        o_ref[r, :] = x + 0.5 * _swiglu_acc(xb, wg_ref, wu_ref, wd_ref)


def _ffn1(x, nw, wg, wu, wd):
    m, d = x.shape
    f = wg.shape[1]
    tm = min(ROW_TILE, m)
    return pl.pallas_call(
        _ffn1_kernel,
        grid=(m // tm,),
        in_specs=[pl.BlockSpec((tm, d), lambda i: (i, 0)),
                  _const_spec((1, d)), _const_spec((d, f)), _const_spec((d, f)), _const_spec((f, d))],
        out_specs=pl.BlockSpec((tm, d), lambda i: (i, 0)),
        out_shape=jax.ShapeDtypeStruct((m, d), F32),
        compiler_params=pltpu.CompilerParams(dimension_semantics=("arbitrary",),
                                             vmem_limit_bytes=VMEM_LIMIT),
        name="ffn1",
    )(x, nw, wg, wu, wd)


def _tail_kernel(h_ref, mix_ref, p_ref, wo_ref, nw_ref, wg_ref, wu_ref, wd_ref,
                 pn_ref, wpg_ref, wpp_ref, fn_ref, o_ref):
    h = h_ref[...] + _dot(mix_ref[...], wo_ref[...])
    xb = _rms(h, nw_ref[...]).astype(BF16)
    h = h + 0.5 * _swiglu_acc(xb, wg_ref, wu_ref, wd_ref)
    gate = _sigmoid(_dot(_rms(h, pn_ref[...]).astype(BF16), wpg_ref[...]))
    h = h + gate * _dot(p_ref[...].astype(BF16), wpp_ref[...])
    o_ref[...] = _rms(h, fn_ref[...])


def _tail(h, mix, p, wo, nw, wg, wu, wd, pn, wpg, wpp, fn):
    m, d = h.shape
    f = wg.shape[1]
    pd = p.shape[1]
    tm = min(ROW_TILE, m)
    row = lambda w: pl.BlockSpec((tm, w), lambda i: (i, 0))
    return pl.pallas_call(
        _tail_kernel,
        grid=(m // tm,),
        in_specs=[row(d), row(d), row(pd),
                  _const_spec((d, d)), _const_spec((1, d)),
                  _const_spec((d, f)), _const_spec((d, f)), _const_spec((f, d)),
                  _const_spec((1, d)), _const_spec((d, d)), _const_spec((pd, d)), _const_spec((1, d))],
        out_specs=row(d),
        out_shape=jax.ShapeDtypeStruct((m, d), F32),
        compiler_params=pltpu.CompilerParams(dimension_semantics=("arbitrary",),
                                             vmem_limit_bytes=VMEM_LIMIT),
        name="tail",
    )(h, mix, p, wo, nw, wg, wu, wd, pn, wpg, wpp, fn)


def _gla_constants(c):
    idx = np.arange(c)
    i, t = idx[:, None], idx[None, :]
    masks = []
    s = 1
    while s < c:
        same = (i // (2 * s)) == (t // (2 * s))
        masks.append(same & ((i % (2 * s)) >= s) & ((t % (2 * s)) < s))
        s *= 2
    tri = (t <= i).astype(np.float32)
    return tri, np.stack([m.astype(np.float32) for m in masks], axis=0)


def _level_exponents(g, b, b_ref, r0):
    c, n = g.shape
    row = lax.broadcasted_iota(jnp.int32, (c, n), 0)
    m4 = row & 3
    g_prev = pltpu.roll(g, 1, 0)
    g_next = pltpu.roll(g, c - 1, 0)
    out = [jnp.where((row & 1) == 1, g, 0.0),
           jnp.where(m4 == 0, g_next, jnp.where(m4 == 1, 0.0, jnp.where(m4 == 2, g, g + g_prev)))]
    s = 4
    while s < c:
        blocks = [jnp.broadcast_to(b_ref[r0 + m * 2 * s + s - 1:r0 + m * 2 * s + s, :], (2 * s, n))
                  for m in range(c // (2 * s))]
        ref = blocks[0] if len(blocks) == 1 else jnp.concatenate(blocks, axis=0)
        sign = jnp.where((lax.broadcasted_iota(jnp.int32, (c, 1), 0) & s) != 0, 1.0, -1.0)
        out.append((b - ref) * sign)
        s *= 2
    return out


def _split_hi_lo(g):
    hi = g.astype(BF16)
    lo = (g - hi.astype(F32)).astype(BF16)
    return jnp.concatenate([hi, lo], axis=1)


def _column(row, eye):
    return jnp.sum(eye * row, axis=-1, keepdims=True)


def _mixer_kernel(h_ref, mn_ref, wa_ref, wb_ref, wlr_ref, wgk_ref, bgk_ref, gn_ref, pw_ref, ps_ref,
                  tri_ref, mk_ref, eye_ref,
                  mix_ref, npool_ref, ngla_ref,
                  xb_s, ext_s, tmp_s, q_s, k_s, v_s, g_s, o_s, a_s, b_s, qin_s, kdec_s, dcy_s, m_s, att_s, gate_s,
                  *, pos0):
    tm, d = h_ref.shape
    heads = GLA_HEADS
    qk = q_s.shape[1]
    dk, dv = qk // heads, d // heads
    c = GLA_CHUNK
    n_chunks = tm // c
    levels = mk_ref.shape[0]
    l = pl.program_id(1)
    x0 = POOL_PAD + HIST
    o_q, o_k, o_v = d, d + qk, d + 2 * qk

    @pl.when(l == 0)
    def _():
        ext_s[0:x0, :] = jnp.zeros((x0, d), F32)
        ngla_ref[...] = jnp.zeros(ngla_ref.shape, F32)

    xb_s[...] = _rms(h_ref[...], mn_ref[...]).astype(BF16)
    q_s[...] = _dot(xb_s[...], wa_ref[:, o_q:o_q + qk]) * (dk ** -0.5)
    k_s[...] = _dot(xb_s[...], wa_ref[:, o_k:o_k + qk])
    lr = _dot(xb_s[...], wlr_ref[...]).astype(BF16)
    g_s[...] = _log_sigmoid(_dot(lr, wgk_ref[...]) + bgk_ref[...]) * LOG2_DECAY_SCALE

    def u_block(lo):
        ext_s[x0:x0 + tm, lo:lo + GATE_BLOCK] = _dot(xb_s[...], wa_ref[:, lo:lo + GATE_BLOCK])

    def v_block(lo):
        v_s[:, lo:lo + GATE_BLOCK] = _dot(xb_s[...], wa_ref[:, o_v + lo:o_v + lo + GATE_BLOCK]).astype(BF16)

    def gate_block(lo):
        z = _dot(xb_s[...], wb_ref[:, lo:lo + GATE_BLOCK])
        gate_s[:, lo:lo + GATE_BLOCK] = _silu(z) if lo < d else _sigmoid(z)

    early_work = ([functools.partial(u_block, lo) for lo in range(0, d, GATE_BLOCK)]
                  + [functools.partial(v_block, lo) for lo in range(0, d, GATE_BLOCK)])
    early_split = np.array_split(np.arange(len(early_work)), n_chunks)
    late_work = [functools.partial(gate_block, lo) for lo in range(0, 3 * d, GATE_BLOCK)]
    late_split = np.array_split(np.arange(len(late_work)), 2 * n_chunks)

    def head_cols(hd):
        return slice(hd * dk, (hd + 1) * dk), slice(hd * dv, (hd + 1) * dv), slice(hd * c, (hd + 1) * c)

    row = lax.broadcasted_iota(jnp.int32, (c, qk), 0)
    for ci in range(n_chunks):
        for idx in early_split[ci]:
            early_work[int(idx)]()
        r0 = ci * c
        rows = slice(r0, r0 + c)
        g = g_s[rows, :]
        sums = _dot(tri_ref[...], _split_hi_lo(g))
        b = sums[:, :qk] + sums[:, qk:]
        b_s[rows, :] = b
        b_last = b[c - 1:c, :]
        q = q_s[rows, :]
        k = k_s[rows, :]
        qin_s[rows, :] = (q * jnp.exp2(b)).astype(BF16)
        kdec_s[rows, :] = (k * jnp.exp2(b_last - b)).astype(BF16)
        dcy_s[ci * 8:ci * 8 + 1, :] = jnp.exp2(b_last)
        for lv, ex in enumerate(_level_exponents(g, b, b_s, r0)):
            operand = jnp.where((row & (1 << lv)) != 0, q, k) * jnp.exp2(ex)
            m_s[lv, rows, :] = operand.astype(BF16)

    pos = (pos0 + 1 + l * tm + lax.broadcasted_iota(jnp.int32, (tm, 1), 0))
    gc = d // len(POOL_WINDOWS)

    def pool_group(grp):
        w = POOL_WINDOWS[grp]
        cols = slice(grp * gc, (grp + 1) * gc)
        shifts = [1 << i for i in range(int(np.log2(w)))]
        starts = [x0] * len(shifts)
        for i in range(len(shifts) - 2, -1, -1):
            starts[i] = (starts[i + 1] - shifts[i + 1]) // 8 * 8
        for i, (sh, st) in enumerate(zip(shifts, starts)):
            if i == 0:
                val = ext_s[st:x0 + tm, cols] + ext_s[st - sh:x0 + tm - sh, cols]
            else:
                val = tmp_s[i - 1, st:x0 + tm, :] + tmp_s[i - 1, st - sh:x0 + tm - sh, :]
            if i < len(shifts) - 1:
                tmp_s[i, st:x0 + tm, :] = val
        cur = ext_s[x0:x0 + tm, cols]
        inv = 1.0 / jnp.minimum(pos, w).astype(F32)
        pooled = (val * inv - cur).astype(BF16)
        a_s[:, cols] = _dot(pooled, pw_ref[grp]) * ps_ref[:, cols]

    pool_split = np.array_split(np.arange(len(POOL_WINDOWS)), n_chunks)
    for ci in range(n_chunks):
        for idx in late_split[ci]:
            late_work[int(idx)]()
        rows = slice(ci * c, (ci + 1) * c)
        for hd in range(heads):
            hc, _, ha = head_cols(hd)
            att = None
            for lv in range(levels):
                operand = m_s[lv, rows, hc]
                t = mk_ref[lv] * _dot_nt(operand, operand)
                att = t if att is None else att + t
            att_s[rows, ha] = att.astype(BF16)
        for grp in pool_split[ci]:
            pool_group(int(grp))

    for ci in range(n_chunks):
        for idx in late_split[n_chunks + ci]:
            late_work[int(idx)]()
        rows = slice(ci * c, (ci + 1) * c)
        qk_prod = q_s[rows, :] * k_s[rows, :]
        for hd in range(heads):
            hc, hv, ha = head_cols(hd)
            state = ngla_ref[0, hd]
            vh = v_s[rows, hv]
            lhs = jnp.concatenate([qin_s[rows, hc], att_s[rows, ha]], axis=1)
            rhs = jnp.concatenate([state.astype(BF16), vh], axis=0)
            diag = jnp.sum(qk_prod[:, hc], axis=-1, keepdims=True)
            o_s[rows, hv] = _dot(lhs, rhs) + diag * vh.astype(F32)
            decay = _column(dcy_s[ci * 8:ci * 8 + 1, hc], eye_ref[...])
            ngla_ref[0, hd] = state * decay + _dot_tn(kdec_s[rows, hc], vh)

    gn = gn_ref[...]
    for hd in range(heads):
        hv = slice(hd * dv, (hd + 1) * dv)
        on = _rms(o_s[:, hv], gn)
        gate = gate_s[:, hd * dv:(hd + 1) * dv]
        ga = gate_s[:, d + hd * dv:d + (hd + 1) * dv]
        gb = gate_s[:, 2 * d + hd * dv:2 * d + (hd + 1) * dv]
        mix_ref[:, hv] = (ga * a_s[:, hv] + gb * (on * gate)).astype(BF16)

    @pl.when(l == pl.num_programs(1) - 1)
    def _():
        npool_ref[0] = ext_s[x0 + tm - POOL_BUF:x0 + tm, :]

    ext_s[POOL_PAD:x0, :] = ext_s[tm + POOL_PAD:tm + x0, :]


def _mixer(h, batch, mn, wa, wb, wlr, wgk, bgk, gn, pw, ps, tri, mk, eye):
    m, d = h.shape
    seq = m // batch
    tm = min(ROW_TILE, seq)
    nl = seq // tm
    qk = bgk.shape[1]
    heads = GLA_HEADS
    dk, dv = qk // heads, d // heads
    c = GLA_CHUNK
    levels = mk.shape[0]
    gc = d // len(POOL_WINDOWS)
    n_stage = int(np.log2(max(POOL_WINDOWS))) - 1
    consts = (mn, wa, wb, wlr, wgk, bgk, gn, pw, ps, tri, mk, eye)
    kern = functools.partial(_mixer_kernel, pos0=0)
    return pl.pallas_call(
        kern,
        grid=(batch, nl),
        in_specs=[pl.BlockSpec((tm, d), lambda b, l: (b * nl + l, 0))] + [_const_spec(a.shape) for a in consts],
        out_specs=[pl.BlockSpec((tm, d), lambda b, l: (b * nl + l, 0)),
                   pl.BlockSpec((1, POOL_BUF, d), lambda b, l: (b, 0, 0)),
                   pl.BlockSpec((1, heads, dk, dv), lambda b, l: (b, 0, 0, 0))],
        out_shape=[jax.ShapeDtypeStruct((m, d), BF16),
                   jax.ShapeDtypeStruct((batch, POOL_BUF, d), F32),
                   jax.ShapeDtypeStruct((batch, heads, dk, dv), F32)],
        scratch_shapes=[pltpu.VMEM((tm, d), BF16),
                        pltpu.VMEM((POOL_PAD + HIST + tm, d), F32),
                        pltpu.VMEM((n_stage, POOL_PAD + HIST + tm, gc), F32),
                        pltpu.VMEM((tm, qk), F32),
                        pltpu.VMEM((tm, qk), F32),
                        pltpu.VMEM((tm, d), BF16),
                        pltpu.VMEM((tm, qk), F32),
                        pltpu.VMEM((tm, d), F32),
                        pltpu.VMEM((tm, d), F32),
                        pltpu.VMEM((tm, qk), F32),
                        pltpu.VMEM((tm, qk), BF16),
                        pltpu.VMEM((tm, qk), BF16),
                        pltpu.VMEM((tm // c * 8, qk), F32),
                        pltpu.VMEM((levels, tm, qk), BF16),
                        pltpu.VMEM((tm, heads * c), BF16),
                        pltpu.VMEM((tm, 3 * d), F32)],
        compiler_params=pltpu.CompilerParams(dimension_semantics=("arbitrary", "arbitrary"),
                                             vmem_limit_bytes=VMEM_LIMIT),
        name="mixer_prompt",
    )(h, *consts)


def _proj_kernel(h_ref, mn_ref, wa_ref, wb_ref, wlr_ref, wgk_ref, bgk_ref, z_ref, g_ref):
    d = h_ref.shape[1]
    qk = g_ref.shape[1]
    dk = qk // GLA_HEADS
    na = wa_ref.shape[1]
    xb = _rms(h_ref[...], mn_ref[...]).astype(BF16)
    z_ref[:, 0:na] = _dot(xb, wa_ref[...])
    z_ref[:, na:] = _dot(xb, wb_ref[...])
    z_ref[:, d:d + qk] = z_ref[:, d:d + qk] * (dk ** -0.5)
    lr = _dot(xb, wlr_ref[...]).astype(BF16)
    g_ref[...] = _log_sigmoid(_dot(lr, wgk_ref[...]) + bgk_ref[...]) * LOG2_DECAY_SCALE


def _proj(h, mn, wa, wb, wlr, wgk, bgk):
    m, d = h.shape
    qk = bgk.shape[1]
    wide = wa.shape[1] + wb.shape[1]
    return pl.pallas_call(
        _proj_kernel,
        grid=(1,),
        in_specs=[pl.BlockSpec((m, d), lambda i: (0, 0)),
                  _const_spec(mn.shape), _const_spec(wa.shape), _const_spec(wb.shape), _const_spec(wlr.shape),
                  _const_spec(wgk.shape), _const_spec(bgk.shape)],
        out_specs=[pl.BlockSpec((m, wide), lambda i: (0, 0)), pl.BlockSpec((m, qk), lambda i: (0, 0))],
        out_shape=[jax.ShapeDtypeStruct((m, wide), F32), jax.ShapeDtypeStruct((m, qk), F32)],
        compiler_params=pltpu.CompilerParams(dimension_semantics=("arbitrary",),
                                             vmem_limit_bytes=VMEM_LIMIT),
        name="proj_sample",
    )(h, mn, wa, wb, wlr, wgk, bgk)


def _sample_mixer_kernel(z_ref, g_ref, sp_ref, sg_ref, gn_ref, pw_ref, ps_ref, eye_ref,
                         mix_ref, npool_ref, ngla_ref,
                         qin_s, kdec_s, v_s, o_s, pl_s, ost_s, *, pos0):
    n_tok, bb, _ = z_ref.shape
    d = o_s.shape[1]
    heads = GLA_HEADS
    qk = g_ref.shape[2]
    dk, dv = qk // heads, d // heads
    o_u, o_q, o_k, o_v = 0, d, d + qk, d + 2 * qk
    o_g = o_v + d
    o_ga, o_gb = o_g + d, o_g + 2 * d
    gc = d // len(POOL_WINDOWS)
    plane = lambda t: slice(t * bb, (t + 1) * bb)

    @pl.when(pl.program_id(0) == 0)
    def _():
        pad = (SAMPLE_PAD - n_tok) * bb
        qin_s[:, (n_tok + 1) * bb:, :] = jnp.zeros((qk // LANE, pad - bb, LANE), F32)
        kdec_s[:, n_tok * bb:, :] = jnp.zeros((qk // LANE, pad, LANE), F32)
        v_s[:, n_tok * bb:, :] = jnp.zeros((d // LANE, pad, LANE), F32)

    b = []
    for t in range(n_tok):
        b.append(g_ref[t] if t == 0 else b[-1] + g_ref[t])
    b_last = b[-1]
    q = [z_ref[t, :, o_q:o_q + qk] for t in range(n_tok)]
    k = [z_ref[t, :, o_k:o_k + qk] for t in range(n_tok)]
    def stage(dst, t, val):
        for i in range(val.shape[1] // LANE):
            dst[i, plane(t), :] = val[:, i * LANE:(i + 1) * LANE]

    for t in range(n_tok):
        stage(qin_s, t, q[t] * jnp.exp2(b[t]))
        stage(kdec_s, t, k[t] * jnp.exp2(b_last - b[t]))
        stage(v_s, t, z_ref[t, :, o_v:o_v + d])
    stage(qin_s, n_tok, jnp.exp2(b_last))
    for t in range(n_tok):
        o_t = [None] * heads
        for j in range(t + 1):
            w_tj = q[t] * k[j] if j == t else q[t] * k[j] * jnp.exp2(b[t] - b[j])
            for hd in range(heads):
                a_tj = jnp.sum(w_tj[:, hd * dk:(hd + 1) * dk], axis=-1, keepdims=True)
                term = a_tj * z_ref[j, :, o_v + hd * dv:o_v + (hd + 1) * dv]
                o_t[hd] = term if o_t[hd] is None else o_t[hd] + term
        for hd in range(heads):
            o_s[plane(t), hd * dv:(hd + 1) * dv] = o_t[hd]

    def seq(n, carry):
        tokens = pl.ds(n, SAMPLE_PAD, stride=bb)
        own_rows = pl.ds(pl.multiple_of(n * SAMPLE_PAD, SAMPLE_PAD), SAMPLE_PAD)
        for hd in range(heads):
            state = sg_ref[n, hd]
            q_tok = qin_s[hd, tokens, :]
            o_state = _dot(q_tok.astype(BF16), state.astype(BF16))
            for i in range(dv // LANE):
                ost_s[hd * (dv // LANE) + i, own_rows, :] = o_state[:, i * LANE:(i + 1) * LANE]
            decay = _column(q_tok[n_tok:n_tok + 1, :], eye_ref[...])
            v_tok = jnp.concatenate([v_s[hd * (dv // LANE) + i, tokens, :] for i in range(dv // LANE)], axis=1)
            ngla_ref[n, hd] = state * decay + _dot_tn(kdec_s[hd, tokens, :].astype(BF16), v_tok.astype(BF16))
        return carry

    lax.fori_loop(0, bb, seq, 0, unroll=2)
    for t in range(n_tok):
        for i in range(d // LANE):
            o_s[plane(t), i * LANE:(i + 1) * LANE] += ost_s[i, pl.ds(t, bb, stride=SAMPLE_PAD), :]

    def pool_row(j):
        return z_ref[j, :, o_u:o_u + d] if j >= 0 else sp_ref[POOL_BUF + j]

    for t in range(n_tok):
        for grp, w in enumerate(POOL_WINDOWS):
            cols = slice(grp * gc, (grp + 1) * gc)
            s = pool_row(t)[:, cols]
            for dlt in range(1, w):
                s = s + pool_row(t - dlt)[:, cols]
            cnt = float(min(pos0 + 1 + t, w))
            pl_s[plane(t), cols] = s * (1.0 / cnt) - pool_row(t)[:, cols]
    for r in range(POOL_BUF):
        npool_ref[r] = pool_row(r + n_tok - POOL_BUF)

    rows = n_tok * bb
    gn = gn_ref[...]
    for hd in range(heads):
        hv = slice(hd * dv, (hd + 1) * dv)
        zcols = lambda off: z_ref[:, :, off + hd * dv:off + (hd + 1) * dv].reshape(rows, dv)
        a_out = _dot(pl_s[:, hv].astype(BF16), pw_ref[hd]) * ps_ref[:, hv]
        on = _rms(o_s[:, hv], gn)
        mix = _sigmoid(zcols(o_ga)) * a_out + _sigmoid(zcols(o_gb)) * (on * _silu(zcols(o_g)))
        mix_ref[:, :, hv] = mix.reshape(n_tok, bb, dv).astype(mix_ref.dtype)


def _sample_mixer(z, g, sp, sg, gn, pw, ps, eye):
    n_tok, nb, wide = z.shape
    qk = g.shape[2]
    d = sp.shape[2]
    heads = GLA_HEADS
    dk, dv = qk // heads, d // heads
    assert d // len(POOL_WINDOWS) == dv, "pool groups and attention heads share a column split here"
    assert n_tok < SAMPLE_PAD and dk == LANE and dv % LANE == 0
    bb = min(SAMPLE_BLOCK, nb)
    kern = functools.partial(_sample_mixer_kernel, pos0=PAST_LEN)
    blk3 = lambda a, b: pl.BlockSpec((a, bb, b), lambda i: (0, i, 0))
    state_spec = pl.BlockSpec((bb, heads, dk, dv), lambda i: (i, 0, 0, 0))
    return pl.pallas_call(
        kern,
        grid=(nb // bb,),
        in_specs=[blk3(n_tok, wide), blk3(n_tok, qk), blk3(POOL_BUF, d), state_spec,
                  _const_spec(gn.shape), _const_spec(pw.shape), _const_spec(ps.shape), _const_spec(eye.shape)],
        out_specs=[blk3(n_tok, d), blk3(POOL_BUF, d), state_spec],
        out_shape=[jax.ShapeDtypeStruct((n_tok, nb, d), BF16),
                   jax.ShapeDtypeStruct((POOL_BUF, nb, d), F32),
                   jax.ShapeDtypeStruct((nb, heads, dk, dv), F32)],
        scratch_shapes=[pltpu.VMEM((qk // LANE, SAMPLE_PAD * bb, LANE), F32),
                        pltpu.VMEM((qk // LANE, SAMPLE_PAD * bb, LANE), F32),
                        pltpu.VMEM((d // LANE, SAMPLE_PAD * bb, LANE), F32),
                        pltpu.VMEM((n_tok * bb, d), F32),
                        pltpu.VMEM((n_tok * bb, d), F32),
                        pltpu.VMEM((d // LANE, SAMPLE_PAD * bb, LANE), F32)],
        compiler_params=pltpu.CompilerParams(dimension_semantics=("arbitrary",),
                                             vmem_limit_bytes=VMEM_LIMIT),
        name="mixer_sample",
    )(z, g, sp, sg, gn, pw, ps, eye)


def _pack_kernel(w_ref, wa_ref, wb_ref, wlr_ref, *, split, rank):
    wa_ref[...] = w_ref[:, 0:split].astype(BF16)
    wb_ref[...] = w_ref[:, split + rank:].astype(BF16)
    lane = lax.broadcasted_iota(jnp.int32, wlr_ref.shape, 1)
    wlr_ref[...] = jnp.where(lane < rank, w_ref[:, split:split + LANE], 0.0).astype(BF16)


def _pack_w_in(w, split, rank):
    d, n = w.shape
    rows = LANE
    row_blk = lambda width: pl.BlockSpec((rows, width), lambda i: (i, 0))
    return pl.pallas_call(
        functools.partial(_pack_kernel, split=split, rank=rank),
        grid=(d // rows,),
        in_specs=[row_blk(n)],
        out_specs=[row_blk(split), row_blk(n - split - rank), row_blk(LANE)],
        out_shape=[jax.ShapeDtypeStruct((d, split), BF16), jax.ShapeDtypeStruct((d, n - split - rank), BF16),
                   jax.ShapeDtypeStruct((d, LANE), BF16)],
        compiler_params=pltpu.CompilerParams(dimension_semantics=("arbitrary",)),
        name="pack_w_in",
    )(w)


def _layer_weights(i, ffn1_norm, ffn1_w_gate, ffn1_w_up, ffn1_w_down, mix_norm, w_in, w_gk_up, b_gk,
                   gla_norm, pool_w, pool_scale, w_out, ffn2_norm, ffn2_w_gate, ffn2_w_up, ffn2_w_down,
                   ple_norm, w_ple_gate, w_ple_proj):
    d = w_in.shape[1]
    qk = w_gk_up.shape[2]
    rank = w_gk_up.shape[1]
    o_lr = 2 * d + 2 * qk
    wa, wb, wlr = _pack_w_in(w_in[i], o_lr, rank)
    wgk = jnp.concatenate([w_gk_up[i], jnp.zeros((LANE - rank, qk), F32)], axis=0).astype(BF16)
    row = lambda a: a[i].reshape(1, -1)
    return dict(
        ffn1=(row(ffn1_norm), ffn1_w_gate[i].astype(BF16), ffn1_w_up[i].astype(BF16), ffn1_w_down[i].astype(BF16)),
        proj=(row(mix_norm), wa, wb, wlr, wgk, row(b_gk)),
        mix=(row(gla_norm), pool_w[i].astype(BF16), row(pool_scale)),
        tail=(w_out[i].astype(BF16), row(ffn2_norm), ffn2_w_gate[i].astype(BF16), ffn2_w_up[i].astype(BF16),
              ffn2_w_down[i].astype(BF16), row(ple_norm), w_ple_gate[i].astype(BF16), w_ple_proj[i].astype(BF16)),
    )


def kernel(x_prompt, x_sample, p_prompt, p_sample, state_pool, state_gla, ffn1_norm, ffn1_w_gate, ffn1_w_up, ffn1_w_down, mix_norm, w_in, w_gk_up, b_gk, gla_norm, pool_w, pool_scale, w_out, ffn2_norm, ffn2_w_gate, ffn2_w_up, ffn2_w_down, ple_norm, w_ple_gate, w_ple_proj, final_norm):
    depth = w_in.shape[0]
    batch, seq, d = x_prompt.shape
    nb, n_tok, _ = x_sample.shape
    assert depth == 1, "the final norm is fused into the tail kernel of the only layer"
    assert n_tok <= 4 and POOL_BUF <= HIST and seq % GLA_CHUNK == 0
    tri, mk = _gla_constants(GLA_CHUNK)
    tri = jnp.asarray(tri, BF16)
    mk = jnp.asarray(mk, F32)
    eye = jnp.eye(w_gk_up.shape[2] // GLA_HEADS, dtype=F32)
    fn = final_norm.reshape(1, -1)

    lw = _layer_weights(0, ffn1_norm, ffn1_w_gate, ffn1_w_up, ffn1_w_down, mix_norm, w_in, w_gk_up, b_gk,
                        gla_norm, pool_w, pool_scale, w_out, ffn2_norm, ffn2_w_gate, ffn2_w_up, ffn2_w_down,
                        ple_norm, w_ple_gate, w_ple_proj)
    h1 = _ffn1(x_prompt.reshape(batch * seq, d), *lw["ffn1"])
    mix, pool_p, gla_p = _mixer(h1, batch, *lw["proj"], *lw["mix"], tri, mk, eye)
    y_prompt = _tail(h1, mix, p_prompt[0].reshape(batch * seq, -1), *lw["tail"], fn)
    tok_major = lambda a: jnp.swapaxes(a, 0, 1)
    h1 = _ffn1(tok_major(x_sample).reshape(n_tok * nb, d), *lw["ffn1"])
    z, g = _proj(h1, *lw["proj"])
    mix, pool_s, gla_s = _sample_mixer(z.reshape(n_tok, nb, -1), g.reshape(n_tok, nb, -1),
                                       tok_major(state_pool[0]), state_gla[0], *lw["mix"], eye)
    y_sample = _tail(h1, mix.reshape(n_tok * nb, d), tok_major(p_sample[0]).reshape(n_tok * nb, -1),
                     *lw["tail"], fn)
    return (y_prompt.reshape(batch, seq, d), tok_major(y_sample.reshape(n_tok, nb, d)),
            pool_p[None], gla_p[None], tok_major(pool_s)[None], gla_s[None])
```

```python
import functools

import numpy as np
import jax
import jax.numpy as jnp
from jax import lax
from jax.experimental import pallas as pl
from jax.experimental.pallas import tpu as pltpu

POOL_WINDOWS = (2, 4, 8, 16)
POOL_BUF = max(POOL_WINDOWS) - 1
GLA_HEADS = 4
GATE_NORMALIZER = 16.0
EPS = 1e-6
PAST_LEN = 16384

ROW_TILE = 512
GLA_CHUNK = 128
HIST = 16
POOL_PAD = 16
SAMPLE_PAD = 8
SAMPLE_BLOCK = 16
LANE = 128
GATE_BLOCK = 256
LOG2_DECAY_SCALE = float(np.log2(np.e)) / GATE_NORMALIZER
VMEM_LIMIT = 56 * 1024 * 1024

F32 = jnp.float32
BF16 = jnp.bfloat16


def _dot(a, b):
    return jnp.dot(a, b, preferred_element_type=F32)


def _dot_nt(a, b):
    return lax.dot_general(a, b, (((1,), (1,)), ((), ())), preferred_element_type=F32)


def _dot_tn(a, b):
    return lax.dot_general(a, b, (((0,), (0,)), ((), ())), preferred_element_type=F32)


def _rms(x, w):
    ms = jnp.mean(x * x, axis=-1, keepdims=True)
    return x * lax.rsqrt(ms + EPS) * w


def _sigmoid(x):
    return 1.0 / (1.0 + jnp.exp(-x))


def _silu(x):
    return x * _sigmoid(x)


def _log_sigmoid(x):
    return jnp.minimum(x, 0.0) - jnp.log1p(jnp.exp(-jnp.abs(x)))


def _ff_chunks(f):
    step = 1024
    return [(lo, min(lo + step, f)) for lo in range(0, f, step)]


def _const_spec(shape):
    nd = len(shape)
    return pl.BlockSpec(shape, lambda *_: (0,) * nd, pipeline_mode=pl.Buffered(1))


def _swiglu_acc(xb, wg_ref, wu_ref, wd_ref):
    acc = None
    for lo, hi in _ff_chunks(wg_ref.shape[1]):
        g = _dot(xb, wg_ref[:, lo:hi])
        u = _dot(xb, wu_ref[:, lo:hi])
        a = (_silu(g) * u).astype(BF16)
        d = _dot(a, wd_ref[lo:hi, :])
        acc = d if acc is None else acc + d
    return acc


def _ffn1_kernel(x_ref, nw_ref, wg_ref, wu_ref, wd_ref, o_ref):
    half = x_ref.shape[0] // 2
    for r in (slice(0, half), slice(half, 2 * half)):
        x = x_ref[r, :]
        xb = _rms(x, nw_ref[...]).astype(BF16)
        o_ref[r, :] = x + 0.5 * _swiglu_acc(xb, wg_ref, wu_ref, wd_ref)


def _ffn1(x, nw, wg, wu, wd):
    m, d = x.shape
    f = wg.shape[1]
    tm = min(ROW_TILE, m)
    return pl.pallas_call(
        _ffn1_kernel,
        grid=(m // tm,),
        in_specs=[pl.BlockSpec((tm, d), lambda i: (i, 0)),
                  _const_spec((1, d)), _const_spec((d, f)), _const_spec((d, f)), _const_spec((f, d))],
        out_specs=pl.BlockSpec((tm, d), lambda i: (i, 0)),
        out_shape=jax.ShapeDtypeStruct((m, d), F32),
        compiler_params=pltpu.CompilerParams(dimension_semantics=("arbitrary",),
                                             vmem_limit_bytes=VMEM_LIMIT),
        name="ffn1",
    )(x, nw, wg, wu, wd)


def _tail_kernel(h_ref, mix_ref, p_ref, wo_ref, nw_ref, wg_ref, wu_ref, wd_ref,
                 pn_ref, wpg_ref, wpp_ref, fn_ref, o_ref):
    h = h_ref[...] + _dot(mix_ref[...], wo_ref[...])
    xb = _rms(h, nw_ref[...]).astype(BF16)
    h = h + 0.5 * _swiglu_acc(xb, wg_ref, wu_ref, wd_ref)
    gate = _sigmoid(_dot(_rms(h, pn_ref[...]).astype(BF16), wpg_ref[...]))
    h = h + gate * _dot(p_ref[...].astype(BF16), wpp_ref[...])
    o_ref[...] = _rms(h, fn_ref[...])


def _tail(h, mix, p, wo, nw, wg, wu, wd, pn, wpg, wpp, fn):
    m, d = h.shape
    f = wg.shape[1]
    pd = p.shape[1]
    tm = min(ROW_TILE, m)
    row = lambda w: pl.BlockSpec((tm, w), lambda i: (i, 0))
    return pl.pallas_call(
        _tail_kernel,
        grid=(m // tm,),
        in_specs=[row(d), row(d), row(pd),
                  _const_spec((d, d)), _const_spec((1, d)),
                  _const_spec((d, f)), _const_spec((d, f)), _const_spec((f, d)),
                  _const_spec((1, d)), _const_spec((d, d)), _const_spec((pd, d)), _const_spec((1, d))],
        out_specs=row(d),
        out_shape=jax.ShapeDtypeStruct((m, d), F32),
        compiler_params=pltpu.CompilerParams(dimension_semantics=("arbitrary",),
                                             vmem_limit_bytes=VMEM_LIMIT),
        name="tail",
    )(h, mix, p, wo, nw, wg, wu, wd, pn, wpg, wpp, fn)


def _gla_constants(c):
    idx = np.arange(c)
    i, t = idx[:, None], idx[None, :]
    masks = []
    s = 1
    while s < c:
        same = (i // (2 * s)) == (t // (2 * s))
        masks.append(same & ((i % (2 * s)) >= s) & ((t % (2 * s)) < s))
        s *= 2
    tri = (t <= i).astype(np.float32)
    return tri, np.stack([m.astype(np.float32) for m in masks], axis=0)


def _level_exponents(g, b, b_ref, r0):
    c, n = g.shape
    row = lax.broadcasted_iota(jnp.int32, (c, n), 0)
    m4 = row & 3
    g_prev = pltpu.roll(g, 1, 0)
    g_next = pltpu.roll(g, c - 1, 0)
    out = [jnp.where((row & 1) == 1, g, 0.0),
           jnp.where(m4 == 0, g_next, jnp.where(m4 == 1, 0.0, jnp.where(m4 == 2, g, g + g_prev)))]
    s = 4
    while s < c:
        blocks = [jnp.broadcast_to(b_ref[r0 + m * 2 * s + s - 1:r0 + m * 2 * s + s, :], (2 * s, n))
                  for m in range(c // (2 * s))]
        ref = blocks[0] if len(blocks) == 1 else jnp.concatenate(blocks, axis=0)
        sign = jnp.where((lax.broadcasted_iota(jnp.int32, (c, 1), 0) & s) != 0, 1.0, -1.0)
        out.append((b - ref) * sign)
        s *= 2
    return out


def _split_hi_lo(g):
    hi = g.astype(BF16)
    lo = (g - hi.astype(F32)).astype(BF16)
    return jnp.concatenate([hi, lo], axis=1)


def _column(row, eye):
    return jnp.sum(eye * row, axis=-1, keepdims=True)


def _mixer_kernel(h_ref, mn_ref, wa_ref, wb_ref, wlr_ref, wgk_ref, bgk_ref, gn_ref, pw_ref, ps_ref,
                  tri_ref, mk_ref, eye_ref,
                  mix_ref, npool_ref, ngla_ref,
                  xb_s, ext_s, tmp_s, q_s, k_s, v_s, g_s, o_s, a_s, b_s, qin_s, kdec_s, dcy_s, m_s, att_s, gate_s,
                  *, pos0):
    tm, d = h_ref.shape
    heads = GLA_HEADS
    qk = q_s.shape[1]
    dk, dv = qk // heads, d // heads
    c = GLA_CHUNK
    n_chunks = tm // c
    levels = mk_ref.shape[0]
    l = pl.program_id(1)
    x0 = POOL_PAD + HIST
    o_q, o_k, o_v = d, d + qk, d + 2 * qk

    @pl.when(l == 0)
    def _():
        ext_s[0:x0, :] = jnp.zeros((x0, d), F32)
        ngla_ref[...] = jnp.zeros(ngla_ref.shape, F32)

    xb_s[...] = _rms(h_ref[...], mn_ref[...]).astype(BF16)
    q_s[...] = _dot_nt(xb_s[...], wa_ref[o_q:o_q + qk, :]) * (dk ** -0.5)
    k_s[...] = _dot_nt(xb_s[...], wa_ref[o_k:o_k + qk, :])
    lr = _dot_nt(xb_s[...], wlr_ref[...]).astype(BF16)
    g_s[...] = _log_sigmoid(_dot(lr, wgk_ref[...]) + bgk_ref[...]) * LOG2_DECAY_SCALE

    def u_block(lo):
        ext_s[x0:x0 + tm, lo:lo + GATE_BLOCK] = _dot_nt(xb_s[...], wa_ref[lo:lo + GATE_BLOCK, :])

    def v_block(lo):
        v_s[:, lo:lo + GATE_BLOCK] = _dot_nt(xb_s[...], wa_ref[o_v + lo:o_v + lo + GATE_BLOCK, :]).astype(BF16)

    def gate_block(lo):
        z = _dot_nt(xb_s[...], wb_ref[lo:lo + GATE_BLOCK, :])
        gate_s[:, lo:lo + GATE_BLOCK] = _silu(z) if lo < d else _sigmoid(z)

    early_work = ([functools.partial(u_block, lo) for lo in range(0, d, GATE_BLOCK)]
                  + [functools.partial(v_block, lo) for lo in range(0, d, GATE_BLOCK)])
    early_split = np.array_split(np.arange(len(early_work)), n_chunks)
    late_work = [functools.partial(gate_block, lo) for lo in range(0, 3 * d, GATE_BLOCK)]
    late_split = np.array_split(np.arange(len(late_work)), 2 * n_chunks)

    def head_cols(hd):
        return slice(hd * dk, (hd + 1) * dk), slice(hd * dv, (hd + 1) * dv), slice(hd * c, (hd + 1) * c)

    row = lax.broadcasted_iota(jnp.int32, (c, qk), 0)
    for ci in range(n_chunks):
        for idx in early_split[ci]:
            early_work[int(idx)]()
        r0 = ci * c
        rows = slice(r0, r0 + c)
        g = g_s[rows, :]
        sums = _dot(tri_ref[...], _split_hi_lo(g))
        b = sums[:, :qk] + sums[:, qk:]
        b_s[rows, :] = b
        b_last = b[c - 1:c, :]
        q = q_s[rows, :]
        k = k_s[rows, :]
        qin_s[rows, :] = (q * jnp.exp2(b)).astype(BF16)
        kdec_s[rows, :] = (k * jnp.exp2(b_last - b)).astype(BF16)
        dcy_s[ci * 8:ci * 8 + 1, :] = jnp.exp2(b_last)
        for lv, ex in enumerate(_level_exponents(g, b, b_s, r0)):
            operand = jnp.where((row & (1 << lv)) != 0, q, k) * jnp.exp2(ex)
            m_s[lv, rows, :] = operand.astype(BF16)

    pos = (pos0 + 1 + l * tm + lax.broadcasted_iota(jnp.int32, (tm, 1), 0))
    gc = d // len(POOL_WINDOWS)

    def pool_group(grp):
        w = POOL_WINDOWS[grp]
        cols = slice(grp * gc, (grp + 1) * gc)
        shifts = [1 << i for i in range(int(np.log2(w)))]
        starts = [x0] * len(shifts)
        for i in range(len(shifts) - 2, -1, -1):
            starts[i] = (starts[i + 1] - shifts[i + 1]) // 8 * 8
        for i, (sh, st) in enumerate(zip(shifts, starts)):
            if i == 0:
                val = ext_s[st:x0 + tm, cols] + ext_s[st - sh:x0 + tm - sh, cols]
            else:
                val = tmp_s[i - 1, st:x0 + tm, :] + tmp_s[i - 1, st - sh:x0 + tm - sh, :]
            if i < len(shifts) - 1:
                tmp_s[i, st:x0 + tm, :] = val
        cur = ext_s[x0:x0 + tm, cols]
        inv = 1.0 / jnp.minimum(pos, w).astype(F32)
        pooled = (val * inv - cur).astype(BF16)
        a_s[:, cols] = _dot(pooled, pw_ref[grp]) * ps_ref[:, cols]

    pool_split = np.array_split(np.arange(len(POOL_WINDOWS)), n_chunks)
    for ci in range(n_chunks):
        for idx in late_split[ci]:
            late_work[int(idx)]()
        rows = slice(ci * c, (ci + 1) * c)
        for hd in range(heads):
            hc, _, ha = head_cols(hd)
            att = None
            for lv in range(levels):
                operand = m_s[lv, rows, hc]
                t = mk_ref[lv] * _dot_nt(operand, operand)
                att = t if att is None else att + t
            att_s[rows, ha] = att.astype(BF16)
        for grp in pool_split[ci]:
            pool_group(int(grp))

    for ci in range(n_chunks):
        for idx in late_split[n_chunks + ci]:
            late_work[int(idx)]()
        rows = slice(ci * c, (ci + 1) * c)
        qk_prod = q_s[rows, :] * k_s[rows, :]
        for hd in range(heads):
            hc, hv, ha = head_cols(hd)
            state = ngla_ref[0, hd]
            vh = v_s[rows, hv]
            lhs = jnp.concatenate([qin_s[rows, hc], att_s[rows, ha]], axis=1)
            rhs = jnp.concatenate([state.astype(BF16), vh], axis=0)
            diag = jnp.sum(qk_prod[:, hc], axis=-1, keepdims=True)
            o_s[rows, hv] = _dot(lhs, rhs) + diag * vh.astype(F32)
            decay = _column(dcy_s[ci * 8:ci * 8 + 1, hc], eye_ref[...])
            ngla_ref[0, hd] = state * decay + _dot_tn(kdec_s[rows, hc], vh)

    gn = gn_ref[...]
    for hd in range(heads):
        hv = slice(hd * dv, (hd + 1) * dv)
        on = _rms(o_s[:, hv], gn)
        gate = gate_s[:, hd * dv:(hd + 1) * dv]
        ga = gate_s[:, d + hd * dv:d + (hd + 1) * dv]
        gb = gate_s[:, 2 * d + hd * dv:2 * d + (hd + 1) * dv]
        mix_ref[:, hv] = (ga * a_s[:, hv] + gb * (on * gate)).astype(BF16)

    @pl.when(l == pl.num_programs(1) - 1)
    def _():
        npool_ref[0] = ext_s[x0 + tm - POOL_BUF:x0 + tm, :]

    ext_s[POOL_PAD:x0, :] = ext_s[tm + POOL_PAD:tm + x0, :]


def _mixer(h, batch, mn, wa, wb, wlr, wgk, bgk, gn, pw, ps, tri, mk, eye):
    m, d = h.shape
    seq = m // batch
    tm = min(ROW_TILE, seq)
    nl = seq // tm
    qk = bgk.shape[1]
    heads = GLA_HEADS
    dk, dv = qk // heads, d // heads
    c = GLA_CHUNK
    levels = mk.shape[0]
    gc = d // len(POOL_WINDOWS)
    n_stage = int(np.log2(max(POOL_WINDOWS))) - 1
    consts = (mn, wa, wb, wlr, wgk, bgk, gn, pw, ps, tri, mk, eye)
    kern = functools.partial(_mixer_kernel, pos0=0)
    return pl.pallas_call(
        kern,
        grid=(batch, nl),
        in_specs=[pl.BlockSpec((tm, d), lambda b, l: (b * nl + l, 0))] + [_const_spec(a.shape) for a in consts],
        out_specs=[pl.BlockSpec((tm, d), lambda b, l: (b * nl + l, 0)),
                   pl.BlockSpec((1, POOL_BUF, d), lambda b, l: (b, 0, 0)),
                   pl.BlockSpec((1, heads, dk, dv), lambda b, l: (b, 0, 0, 0))],
        out_shape=[jax.ShapeDtypeStruct((m, d), BF16),
                   jax.ShapeDtypeStruct((batch, POOL_BUF, d), F32),
                   jax.ShapeDtypeStruct((batch, heads, dk, dv), F32)],
        scratch_shapes=[pltpu.VMEM((tm, d), BF16),
                        pltpu.VMEM((POOL_PAD + HIST + tm, d), F32),
                        pltpu.VMEM((n_stage, POOL_PAD + HIST + tm, gc), F32),
                        pltpu.VMEM((tm, qk), F32),
                        pltpu.VMEM((tm, qk), F32),
                        pltpu.VMEM((tm, d), BF16),
                        pltpu.VMEM((tm, qk), F32),
                        pltpu.VMEM((tm, d), F32),
                        pltpu.VMEM((tm, d), F32),
                        pltpu.VMEM((tm, qk), F32),
                        pltpu.VMEM((tm, qk), BF16),
                        pltpu.VMEM((tm, qk), BF16),
                        pltpu.VMEM((tm // c * 8, qk), F32),
                        pltpu.VMEM((levels, tm, qk), BF16),
                        pltpu.VMEM((tm, heads * c), BF16),
                        pltpu.VMEM((tm, 3 * d), F32)],
        compiler_params=pltpu.CompilerParams(dimension_semantics=("arbitrary", "arbitrary"),
                                             vmem_limit_bytes=VMEM_LIMIT),
        name="mixer_prompt",
    )(h, *consts)


def _proj_kernel(h_ref, mn_ref, wa_ref, wb_ref, wlr_ref, wgk_ref, bgk_ref, z_ref, g_ref):
    d = h_ref.shape[1]
    qk = g_ref.shape[1]
    dk = qk // GLA_HEADS
    na = wa_ref.shape[0]
    xb = _rms(h_ref[...], mn_ref[...]).astype(BF16)
    z_ref[:, 0:na] = _dot_nt(xb, wa_ref[...])
    z_ref[:, na:] = _dot_nt(xb, wb_ref[...])
    z_ref[:, d:d + qk] = z_ref[:, d:d + qk] * (dk ** -0.5)
    lr = _dot_nt(xb, wlr_ref[...]).astype(BF16)
    g_ref[...] = _log_sigmoid(_dot(lr, wgk_ref[...]) + bgk_ref[...]) * LOG2_DECAY_SCALE


def _proj(h, mn, wa, wb, wlr, wgk, bgk):
    m, d = h.shape
    qk = bgk.shape[1]
    wide = wa.shape[0] + wb.shape[0]
    return pl.pallas_call(
        _proj_kernel,
        grid=(1,),
        in_specs=[pl.BlockSpec((m, d), lambda i: (0, 0)),
                  _const_spec(mn.shape), _const_spec(wa.shape), _const_spec(wb.shape), _const_spec(wlr.shape),
                  _const_spec(wgk.shape), _const_spec(bgk.shape)],
        out_specs=[pl.BlockSpec((m, wide), lambda i: (0, 0)), pl.BlockSpec((m, qk), lambda i: (0, 0))],
        out_shape=[jax.ShapeDtypeStruct((m, wide), F32), jax.ShapeDtypeStruct((m, qk), F32)],
        compiler_params=pltpu.CompilerParams(dimension_semantics=("arbitrary",),
                                             vmem_limit_bytes=VMEM_LIMIT),
        name="proj_sample",
    )(h, mn, wa, wb, wlr, wgk, bgk)


def _sample_mixer_kernel(z_ref, g_ref, sp_ref, sg_ref, gn_ref, pw_ref, ps_ref, eye_ref,
                         mix_ref, npool_ref, ngla_ref,
                         qin_s, kdec_s, v_s, o_s, pl_s, ost_s, *, pos0):
    n_tok, bb, _ = z_ref.shape
    d = o_s.shape[1]
    heads = GLA_HEADS
    qk = g_ref.shape[2]
    dk, dv = qk // heads, d // heads
    o_u, o_q, o_k, o_v = 0, d, d + qk, d + 2 * qk
    o_g = o_v + d
    o_ga, o_gb = o_g + d, o_g + 2 * d
    gc = d // len(POOL_WINDOWS)
    plane = lambda t: slice(t * bb, (t + 1) * bb)

    @pl.when(pl.program_id(0) == 0)
    def _():
        pad = (SAMPLE_PAD - n_tok) * bb
        qin_s[:, (n_tok + 1) * bb:, :] = jnp.zeros((qk // LANE, pad - bb, LANE), F32)
        kdec_s[:, n_tok * bb:, :] = jnp.zeros((qk // LANE, pad, LANE), F32)
        v_s[:, n_tok * bb:, :] = jnp.zeros((d // LANE, pad, LANE), F32)

    b = []
    for t in range(n_tok):
        b.append(g_ref[t] if t == 0 else b[-1] + g_ref[t])
    b_last = b[-1]
    q = [z_ref[t, :, o_q:o_q + qk] for t in range(n_tok)]
    k = [z_ref[t, :, o_k:o_k + qk] for t in range(n_tok)]
    def stage(dst, t, val):
        for i in range(val.shape[1] // LANE):
            dst[i, plane(t), :] = val[:, i * LANE:(i + 1) * LANE]

    for t in range(n_tok):
        stage(qin_s, t, q[t] * jnp.exp2(b[t]))
        stage(kdec_s, t, k[t] * jnp.exp2(b_last - b[t]))
        stage(v_s, t, z_ref[t, :, o_v:o_v + d])
    stage(qin_s, n_tok, jnp.exp2(b_last))
    for t in range(n_tok):
        o_t = [None] * heads
        for j in range(t + 1):
            w_tj = q[t] * k[j] if j == t else q[t] * k[j] * jnp.exp2(b[t] - b[j])
            for hd in range(heads):
                a_tj = jnp.sum(w_tj[:, hd * dk:(hd + 1) * dk], axis=-1, keepdims=True)
                term = a_tj * z_ref[j, :, o_v + hd * dv:o_v + (hd + 1) * dv]
                o_t[hd] = term if o_t[hd] is None else o_t[hd] + term
        for hd in range(heads):
            o_s[plane(t), hd * dv:(hd + 1) * dv] = o_t[hd]

    def seq(n, carry):
        tokens = pl.ds(n, SAMPLE_PAD, stride=bb)
        own_rows = pl.ds(pl.multiple_of(n * SAMPLE_PAD, SAMPLE_PAD), SAMPLE_PAD)
        for hd in range(heads):
            state = sg_ref[n, hd]
            q_tok = qin_s[hd, tokens, :]
            o_state = _dot(q_tok.astype(BF16), state.astype(BF16))
            for i in range(dv // LANE):
                ost_s[hd * (dv // LANE) + i, own_rows, :] = o_state[:, i * LANE:(i + 1) * LANE]
            decay = _column(q_tok[n_tok:n_tok + 1, :], eye_ref[...])
            v_tok = jnp.concatenate([v_s[hd * (dv // LANE) + i, tokens, :] for i in range(dv // LANE)], axis=1)
            ngla_ref[n, hd] = state * decay + _dot_tn(kdec_s[hd, tokens, :].astype(BF16), v_tok.astype(BF16))
        return carry

    lax.fori_loop(0, bb, seq, 0, unroll=2)
    for t in range(n_tok):
        for i in range(d // LANE):
            o_s[plane(t), i * LANE:(i + 1) * LANE] += ost_s[i, pl.ds(t, bb, stride=SAMPLE_PAD), :]

    def pool_row(j):
        return z_ref[j, :, o_u:o_u + d] if j >= 0 else sp_ref[POOL_BUF + j]

    for t in range(n_tok):
        for grp, w in enumerate(POOL_WINDOWS):
            cols = slice(grp * gc, (grp + 1) * gc)
            s = pool_row(t)[:, cols]
            for dlt in range(1, w):
                s = s + pool_row(t - dlt)[:, cols]
            cnt = float(min(pos0 + 1 + t, w))
            pl_s[plane(t), cols] = s * (1.0 / cnt) - pool_row(t)[:, cols]
    for r in range(POOL_BUF):
        npool_ref[r] = pool_row(r + n_tok - POOL_BUF)

    rows = n_tok * bb
    gn = gn_ref[...]
    for hd in range(heads):
        hv = slice(hd * dv, (hd + 1) * dv)
        zcols = lambda off: z_ref[:, :, off + hd * dv:off + (hd + 1) * dv].reshape(rows, dv)
        a_out = _dot(pl_s[:, hv].astype(BF16), pw_ref[hd]) * ps_ref[:, hv]
        on = _rms(o_s[:, hv], gn)
        mix = _sigmoid(zcols(o_ga)) * a_out + _sigmoid(zcols(o_gb)) * (on * _silu(zcols(o_g)))
        mix_ref[:, :, hv] = mix.reshape(n_tok, bb, dv).astype(mix_ref.dtype)


def _sample_mixer(z, g, sp, sg, gn, pw, ps, eye):
    n_tok, nb, wide = z.shape
    qk = g.shape[2]
    d = sp.shape[2]
    heads = GLA_HEADS
    dk, dv = qk // heads, d // heads
    assert d // len(POOL_WINDOWS) == dv, "pool groups and attention heads share a column split here"
    assert n_tok < SAMPLE_PAD and dk == LANE and dv % LANE == 0
    bb = min(SAMPLE_BLOCK, nb)
    kern = functools.partial(_sample_mixer_kernel, pos0=PAST_LEN)
    blk3 = lambda a, b: pl.BlockSpec((a, bb, b), lambda i: (0, i, 0))
    state_spec = pl.BlockSpec((bb, heads, dk, dv), lambda i: (i, 0, 0, 0))
    return pl.pallas_call(
        kern,
        grid=(nb // bb,),
        in_specs=[blk3(n_tok, wide), blk3(n_tok, qk), blk3(POOL_BUF, d), state_spec,
                  _const_spec(gn.shape), _const_spec(pw.shape), _const_spec(ps.shape), _const_spec(eye.shape)],
        out_specs=[blk3(n_tok, d), blk3(POOL_BUF, d), state_spec],
        out_shape=[jax.ShapeDtypeStruct((n_tok, nb, d), BF16),
                   jax.ShapeDtypeStruct((POOL_BUF, nb, d), F32),
                   jax.ShapeDtypeStruct((nb, heads, dk, dv), F32)],
        scratch_shapes=[pltpu.VMEM((qk // LANE, SAMPLE_PAD * bb, LANE), F32),
                        pltpu.VMEM((qk // LANE, SAMPLE_PAD * bb, LANE), F32),
                        pltpu.VMEM((d // LANE, SAMPLE_PAD * bb, LANE), F32),
                        pltpu.VMEM((n_tok * bb, d), F32),
                        pltpu.VMEM((n_tok * bb, d), F32),
                        pltpu.VMEM((d // LANE, SAMPLE_PAD * bb, LANE), F32)],
        compiler_params=pltpu.CompilerParams(dimension_semantics=("arbitrary",),
                                             vmem_limit_bytes=VMEM_LIMIT),
        name="mixer_sample",
    )(z, g, sp, sg, gn, pw, ps, eye)


def _layer_weights(i, ffn1_norm, ffn1_w_gate, ffn1_w_up, ffn1_w_down, mix_norm, w_in, w_gk_up, b_gk,
                   gla_norm, pool_w, pool_scale, w_out, ffn2_norm, ffn2_w_gate, ffn2_w_up, ffn2_w_down,
                   ple_norm, w_ple_gate, w_ple_proj):
    d = w_in.shape[1]
    qk = w_gk_up.shape[2]
    rank = w_gk_up.shape[1]
    o_lr = 2 * d + 2 * qk
    wt = jnp.swapaxes(w_in[i], 0, 1)
    wa = wt[:o_lr].astype(BF16)
    wb = wt[o_lr + rank:].astype(BF16)
    wlr = jnp.pad(wt[o_lr:o_lr + rank], ((0, LANE - rank), (0, 0))).astype(BF16)
    wgk = jnp.concatenate([w_gk_up[i], jnp.zeros((LANE - rank, qk), F32)], axis=0).astype(BF16)
    row = lambda a: a[i].reshape(1, -1)
    return dict(
        ffn1=(row(ffn1_norm), ffn1_w_gate[i].astype(BF16), ffn1_w_up[i].astype(BF16), ffn1_w_down[i].astype(BF16)),
        proj=(row(mix_norm), wa, wb, wlr, wgk, row(b_gk)),
        mix=(row(gla_norm), pool_w[i].astype(BF16), row(pool_scale)),
        tail=(w_out[i].astype(BF16), row(ffn2_norm), ffn2_w_gate[i].astype(BF16), ffn2_w_up[i].astype(BF16),
              ffn2_w_down[i].astype(BF16), row(ple_norm), w_ple_gate[i].astype(BF16), w_ple_proj[i].astype(BF16)),
    )


def kernel(x_prompt, x_sample, p_prompt, p_sample, state_pool, state_gla, ffn1_norm, ffn1_w_gate, ffn1_w_up, ffn1_w_down, mix_norm, w_in, w_gk_up, b_gk, gla_norm, pool_w, pool_scale, w_out, ffn2_norm, ffn2_w_gate, ffn2_w_up, ffn2_w_down, ple_norm, w_ple_gate, w_ple_proj, final_norm):
    depth = w_in.shape[0]
    batch, seq, d = x_prompt.shape
    nb, n_tok, _ = x_sample.shape
    assert depth == 1, "the final norm is fused into the tail kernel of the only layer"
    assert n_tok <= 4 and POOL_BUF <= HIST and seq % GLA_CHUNK == 0
    tri, mk = _gla_constants(GLA_CHUNK)
    tri = jnp.asarray(tri, BF16)
    mk = jnp.asarray(mk, F32)
    eye = jnp.eye(w_gk_up.shape[2] // GLA_HEADS, dtype=F32)
    fn = final_norm.reshape(1, -1)

    lw = _layer_weights(0, ffn1_norm, ffn1_w_gate, ffn1_w_up, ffn1_w_down, mix_norm, w_in, w_gk_up, b_gk,
                        gla_norm, pool_w, pool_scale, w_out, ffn2_norm, ffn2_w_gate, ffn2_w_up, ffn2_w_down,
                        ple_norm, w_ple_gate, w_ple_proj)
    h1 = _ffn1(x_prompt.reshape(batch * seq, d), *lw["ffn1"])
    mix, pool_p, gla_p = _mixer(h1, batch, *lw["proj"], *lw["mix"], tri, mk, eye)
    y_prompt = _tail(h1, mix, p_prompt[0].reshape(batch * seq, -1), *lw["tail"], fn)
    tok_major = lambda a: jnp.swapaxes(a, 0, 1)
    h1 = _ffn1(tok_major(x_sample).reshape(n_tok * nb, d), *lw["ffn1"])
    z, g = _proj(h1, *lw["proj"])
    mix, pool_s, gla_s = _sample_mixer(z.reshape(n_tok, nb, -1), g.reshape(n_tok, nb, -1),
                                       tok_major(state_pool[0]), state_gla[0], *lw["mix"], eye)
    y_sample = _tail(h1, mix.reshape(n_tok * nb, d), tok_major(p_sample[0]).reshape(n_tok * nb, -1),
                     *lw["tail"], fn)
    return (y_prompt.reshape(batch, seq, d), tok_major(y_sample.reshape(n_tok, nb, d)),
            pool_p[None], gla_p[None], tok_major(pool_s)[None], gla_s[None])
```

```python
import functools

import numpy as np
import jax
import jax.numpy as jnp
from jax import lax
from jax.experimental import pallas as pl
from jax.experimental.pallas import tpu as pltpu

POOL_WINDOWS = (2, 4, 8, 16)
POOL_BUF = max(POOL_WINDOWS) - 1
GLA_HEADS = 4
GATE_NORMALIZER = 16.0
EPS = 1e-6
PAST_LEN = 16384

ROW_TILE = 512
GLA_CHUNK = 128
HIST = 16
POOL_PAD = 16
SAMPLE_PAD = 8
SAMPLE_BLOCK = 16
LANE = 128
GATE_BLOCK = 256
LOG2_DECAY_SCALE = float(np.log2(np.e)) / GATE_NORMALIZER
VMEM_LIMIT = 56 * 1024 * 1024

F32 = jnp.float32
BF16 = jnp.bfloat16


def _dot(a, b):
    return jnp.dot(a, b, preferred_element_type=F32)


def _dot_nt(a, b):
    return lax.dot_general(a, b, (((1,), (1,)), ((), ())), preferred_element_type=F32)


def _dot_tn(a, b):
    return lax.dot_general(a, b, (((0,), (0,)), ((), ())), preferred_element_type=F32)


def _rms(x, w):
    ms = jnp.mean(x * x, axis=-1, keepdims=True)
    return x * lax.rsqrt(ms + EPS) * w


def _sigmoid(x):
    return 1.0 / (1.0 + jnp.exp(-x))


def _silu(x):
    return x * _sigmoid(x)


def _log_sigmoid(x):
    return jnp.minimum(x, 0.0) - jnp.log1p(jnp.exp(-jnp.abs(x)))


def _ff_chunks(f):
    step = 1024
    return [(lo, min(lo + step, f)) for lo in range(0, f, step)]


def _const_spec(shape):
    nd = len(shape)
    return pl.BlockSpec(shape, lambda *_: (0,) * nd, pipeline_mode=pl.Buffered(1))


def _swiglu_acc(xb, wg_ref, wu_ref, wd_ref):
    acc = None
    for lo, hi in _ff_chunks(wg_ref.shape[1]):
        g = _dot(xb, wg_ref[:, lo:hi])
        u = _dot(xb, wu_ref[:, lo:hi])
        a = (_silu(g) * u).astype(BF16)
        d = _dot(a, wd_ref[lo:hi, :])
        acc = d if acc is None else acc + d
    return acc


def _ffn1_kernel(x_ref, nw_ref, wg_ref, wu_ref, wd_ref, o_ref):
    half = x_ref.shape[0] // 2
    for r in (slice(0, half), slice(half, 2 * half)):
        x = x_ref[r, :]
        xb = _rms(x, nw_ref[...]).astype(BF16)
        o_ref[r, :] = x + 0.5 * _swiglu_acc(xb, wg_ref, wu_ref, wd_ref)


def _ffn1(x, nw, wg, wu, wd):
    m, d = x.shape
    f = wg.shape[1]
    tm = min(ROW_TILE, m)
    return pl.pallas_call(
        _ffn1_kernel,
        grid=(m // tm,),
        in_specs=[pl.BlockSpec((tm, d), lambda i: (i, 0)),
                  _const_spec((1, d)), _const_spec((d, f)), _const_spec((d, f)), _const_spec((f, d))],
        out_specs=pl.BlockSpec((tm, d), lambda i: (i, 0)),
        out_shape=jax.ShapeDtypeStruct((m, d), F32),
        compiler_params=pltpu.CompilerParams(dimension_semantics=("arbitrary",),
                                             vmem_limit_bytes=VMEM_LIMIT),
        name="ffn1",
    )(x, nw, wg, wu, wd)


def _tail_kernel(h_ref, mix_ref, p_ref, wo_ref, nw_ref, wg_ref, wu_ref, wd_ref,
                 pn_ref, wpg_ref, wpp_ref, fn_ref, o_ref):
    h = h_ref[...] + _dot(mix_ref[...], wo_ref[...])
    xb = _rms(h, nw_ref[...]).astype(BF16)
    h = h + 0.5 * _swiglu_acc(xb, wg_ref, wu_ref, wd_ref)
    gate = _sigmoid(_dot(_rms(h, pn_ref[...]).astype(BF16), wpg_ref[...]))
    h = h + gate * _dot(p_ref[...].astype(BF16), wpp_ref[...])
    o_ref[...] = _rms(h, fn_ref[...])


def _tail(h, mix, p, wo, nw, wg, wu, wd, pn, wpg, wpp, fn):
    m, d = h.shape
    f = wg.shape[1]
    pd = p.shape[1]
    tm = min(ROW_TILE, m)
    row = lambda w: pl.BlockSpec((tm, w), lambda i: (i, 0))
    return pl.pallas_call(
        _tail_kernel,
        grid=(m // tm,),
        in_specs=[row(d), row(d), row(pd),
                  _const_spec((d, d)), _const_spec((1, d)),
                  _const_spec((d, f)), _const_spec((d, f)), _const_spec((f, d)),
                  _const_spec((1, d)), _const_spec((d, d)), _const_spec((pd, d)), _const_spec((1, d))],
        out_specs=row(d),
        out_shape=jax.ShapeDtypeStruct((m, d), F32),
        compiler_params=pltpu.CompilerParams(dimension_semantics=("arbitrary",),
                                             vmem_limit_bytes=VMEM_LIMIT),
        name="tail",
    )(h, mix, p, wo, nw, wg, wu, wd, pn, wpg, wpp, fn)


def _gla_constants(c):
    idx = np.arange(c)
    i, t = idx[:, None], idx[None, :]
    masks = []
    s = 1
    while s < c:
        same = (i // (2 * s)) == (t // (2 * s))
        masks.append(same & ((i % (2 * s)) >= s) & ((t % (2 * s)) < s))
        s *= 2
    tri = (t <= i).astype(np.float32)
    return tri, np.stack([m.astype(np.float32) for m in masks], axis=0)


def _level_exponents(g, b, b_ref, r0):
    c, n = g.shape
    row = lax.broadcasted_iota(jnp.int32, (c, n), 0)
    m4 = row & 3
    g_prev = pltpu.roll(g, 1, 0)
    g_next = pltpu.roll(g, c - 1, 0)
    out = [jnp.where((row & 1) == 1, g, 0.0),
           jnp.where(m4 == 0, g_next, jnp.where(m4 == 1, 0.0, jnp.where(m4 == 2, g, g + g_prev)))]
    s = 4
    while s < c:
        blocks = [jnp.broadcast_to(b_ref[r0 + m * 2 * s + s - 1:r0 + m * 2 * s + s, :], (2 * s, n))
                  for m in range(c // (2 * s))]
        ref = blocks[0] if len(blocks) == 1 else jnp.concatenate(blocks, axis=0)
        sign = jnp.where((lax.broadcasted_iota(jnp.int32, (c, 1), 0) & s) != 0, 1.0, -1.0)
        out.append((b - ref) * sign)
        s *= 2
    return out


def _split_hi_lo(g):
    hi = g.astype(BF16)
    lo = (g - hi.astype(F32)).astype(BF16)
    return jnp.concatenate([hi, lo], axis=1)


def _column(row, eye):
    return jnp.sum(eye * row, axis=-1, keepdims=True)


def _mixer_kernel(h_ref, mn_ref, wa_ref, wb_ref, wlr_ref, wgk_ref, bgk_ref, gn_ref, pw_ref, ps_ref,
                  tri_ref, mk_ref, eye_ref,
                  mix_ref, npool_ref, ngla_ref,
                  xb_s, ext_s, tmp_s, q_s, k_s, v_s, g_s, o_s, a_s, b_s, qin_s, kdec_s, dcy_s, m_s, att_s, gate_s,
                  *, pos0):
    tm, d = h_ref.shape
    heads = GLA_HEADS
    qk = q_s.shape[1]
    dk, dv = qk // heads, d // heads
    c = GLA_CHUNK
    n_chunks = tm // c
    levels = mk_ref.shape[0]
    l = pl.program_id(1)
    x0 = POOL_PAD + HIST
    o_q, o_k, o_v = d, d + qk, d + 2 * qk

    @pl.when(l == 0)
    def _():
        ext_s[0:x0, :] = jnp.zeros((x0, d), F32)
        ngla_ref[...] = jnp.zeros(ngla_ref.shape, F32)

    xb_s[...] = _rms(h_ref[...], mn_ref[...]).astype(BF16)
    q_s[...] = _dot(xb_s[...], wa_ref[:, o_q:o_q + qk]) * (dk ** -0.5)
    k_s[...] = _dot(xb_s[...], wa_ref[:, o_k:o_k + qk])
    lr = _dot(xb_s[...], wlr_ref[...]).astype(BF16)
    g_s[...] = _log_sigmoid(_dot(lr, wgk_ref[...]) + bgk_ref[...]) * LOG2_DECAY_SCALE

    def u_block(lo):
        ext_s[x0:x0 + tm, lo:lo + GATE_BLOCK] = _dot(xb_s[...], wa_ref[:, lo:lo + GATE_BLOCK])

    def v_block(lo):
        v_s[:, lo:lo + GATE_BLOCK] = _dot(xb_s[...], wa_ref[:, o_v + lo:o_v + lo + GATE_BLOCK]).astype(BF16)

    def gate_block(lo):
        z = _dot(xb_s[...], wb_ref[:, lo:lo + GATE_BLOCK])
        gate_s[:, lo:lo + GATE_BLOCK] = _silu(z) if lo < d else _sigmoid(z)

    early_work = ([functools.partial(u_block, lo) for lo in range(0, d, GATE_BLOCK)]
                  + [functools.partial(v_block, lo) for lo in range(0, d, GATE_BLOCK)])
    early_split = np.array_split(np.arange(len(early_work)), n_chunks)
    late_work = [functools.partial(gate_block, lo) for lo in range(0, 3 * d, GATE_BLOCK)]
    late_split = np.array_split(np.arange(len(late_work)), 2 * n_chunks)

    def head_cols(hd):
        return slice(hd * dk, (hd + 1) * dk), slice(hd * dv, (hd + 1) * dv), slice(hd * c, (hd + 1) * c)

    row = lax.broadcasted_iota(jnp.int32, (c, qk), 0)
    for ci in range(n_chunks):
        for idx in early_split[ci]:
            early_work[int(idx)]()
        r0 = ci * c
        rows = slice(r0, r0 + c)
        g = g_s[rows, :]
        sums = _dot(tri_ref[...], _split_hi_lo(g))
        b = sums[:, :qk] + sums[:, qk:]
        b_s[rows, :] = b
        b_last = b[c - 1:c, :]
        q = q_s[rows, :]
        k = k_s[rows, :]
        qin_s[rows, :] = (q * jnp.exp2(b)).astype(BF16)
        kdec_s[rows, :] = (k * jnp.exp2(b_last - b)).astype(BF16)
        dcy_s[ci * 8:ci * 8 + 1, :] = jnp.exp2(b_last)
        for lv, ex in enumerate(_level_exponents(g, b, b_s, r0)):
            operand = jnp.where((row & (1 << lv)) != 0, q, k) * jnp.exp2(ex)
            m_s[lv, rows, :] = operand.astype(BF16)

    pos = (pos0 + 1 + l * tm + lax.broadcasted_iota(jnp.int32, (tm, 1), 0))
    gc = d // len(POOL_WINDOWS)

    def pool_group(grp):
        w = POOL_WINDOWS[grp]
        cols = slice(grp * gc, (grp + 1) * gc)
        shifts = [1 << i for i in range(int(np.log2(w)))]
        starts = [x0] * len(shifts)
        for i in range(len(shifts) - 2, -1, -1):
            starts[i] = (starts[i + 1] - shifts[i + 1]) // 8 * 8
        for i, (sh, st) in enumerate(zip(shifts, starts)):
            if i == 0:
                val = ext_s[st:x0 + tm, cols] + ext_s[st - sh:x0 + tm - sh, cols]
            else:
                val = tmp_s[i - 1, st:x0 + tm, :] + tmp_s[i - 1, st - sh:x0 + tm - sh, :]
            if i < len(shifts) - 1:
                tmp_s[i, st:x0 + tm, :] = val
        cur = ext_s[x0:x0 + tm, cols]
        inv = 1.0 / jnp.minimum(pos, w).astype(F32)
        pooled = (val * inv - cur).astype(BF16)
        a_s[:, cols] = _dot(pooled, pw_ref[grp]) * ps_ref[:, cols]

    pool_split = np.array_split(np.arange(len(POOL_WINDOWS)), n_chunks)
    for ci in range(n_chunks):
        for idx in late_split[ci]:
            late_work[int(idx)]()
        rows = slice(ci * c, (ci + 1) * c)
        for hd in range(heads):
            hc, _, ha = head_cols(hd)
            att = None
            for lv in range(levels):
                operand = m_s[lv, rows, hc]
                t = mk_ref[lv] * _dot_nt(operand, operand)
                att = t if att is None else att + t
            att_s[rows, ha] = att.astype(BF16)
        for grp in pool_split[ci]:
            pool_group(int(grp))

    for ci in range(n_chunks):
        for idx in late_split[n_chunks + ci]:
            late_work[int(idx)]()
        rows = slice(ci * c, (ci + 1) * c)
        qk_prod = q_s[rows, :] * k_s[rows, :]
        for hd in range(heads):
            hc, hv, ha = head_cols(hd)
            state = ngla_ref[0, hd]
            vh = v_s[rows, hv]
            lhs = jnp.concatenate([qin_s[rows, hc], att_s[rows, ha]], axis=1)
            rhs = jnp.concatenate([state.astype(BF16), vh], axis=0)
            diag = jnp.sum(qk_prod[:, hc], axis=-1, keepdims=True)
            o_s[rows, hv] = _dot(lhs, rhs) + diag * vh.astype(F32)
            decay = _column(dcy_s[ci * 8:ci * 8 + 1, hc], eye_ref[...])
            ngla_ref[0, hd] = state * decay + _dot_tn(kdec_s[rows, hc], vh)

    gn = gn_ref[...]
    for hd in range(heads):
        hv = slice(hd * dv, (hd + 1) * dv)
        on = _rms(o_s[:, hv], gn)
        gate = gate_s[:, hd * dv:(hd + 1) * dv]
        ga = gate_s[:, d + hd * dv:d + (hd + 1) * dv]
        gb = gate_s[:, 2 * d + hd * dv:2 * d + (hd + 1) * dv]
        mix_ref[:, hv] = (ga * a_s[:, hv] + gb * (on * gate)).astype(BF16)

    @pl.when(l == pl.num_programs(1) - 1)
    def _():
        npool_ref[0] = ext_s[x0 + tm - POOL_BUF:x0 + tm, :]

    ext_s[POOL_PAD:x0, :] = ext_s[tm + POOL_PAD:tm + x0, :]


def _mixer(h, batch, mn, wa, wb, wlr, wgk, bgk, gn, pw, ps, tri, mk, eye):
    m, d = h.shape
    seq = m // batch
    tm = min(ROW_TILE, seq)
    nl = seq // tm
    qk = bgk.shape[1]
    heads = GLA_HEADS
    dk, dv = qk // heads, d // heads
    c = GLA_CHUNK
    levels = mk.shape[0]
    gc = d // len(POOL_WINDOWS)
    n_stage = int(np.log2(max(POOL_WINDOWS))) - 1
    consts = (mn, wa, wb, wlr, wgk, bgk, gn, pw, ps, tri, mk, eye)
    kern = functools.partial(_mixer_kernel, pos0=0)
    return pl.pallas_call(
        kern,
        grid=(batch, nl),
        in_specs=[pl.BlockSpec((tm, d), lambda b, l: (b * nl + l, 0))] + [_const_spec(a.shape) for a in consts],
        out_specs=[pl.BlockSpec((tm, d), lambda b, l: (b * nl + l, 0)),
                   pl.BlockSpec((1, POOL_BUF, d), lambda b, l: (b, 0, 0)),
                   pl.BlockSpec((1, heads, dk, dv), lambda b, l: (b, 0, 0, 0))],
        out_shape=[jax.ShapeDtypeStruct((m, d), BF16),
                   jax.ShapeDtypeStruct((batch, POOL_BUF, d), F32),
                   jax.ShapeDtypeStruct((batch, heads, dk, dv), F32)],
        scratch_shapes=[pltpu.VMEM((tm, d), BF16),
                        pltpu.VMEM((POOL_PAD + HIST + tm, d), F32),
                        pltpu.VMEM((n_stage, POOL_PAD + HIST + tm, gc), F32),
                        pltpu.VMEM((tm, qk), F32),
                        pltpu.VMEM((tm, qk), F32),
                        pltpu.VMEM((tm, d), BF16),
                        pltpu.VMEM((tm, qk), F32),
                        pltpu.VMEM((tm, d), F32),
                        pltpu.VMEM((tm, d), F32),
                        pltpu.VMEM((tm, qk), F32),
                        pltpu.VMEM((tm, qk), BF16),
                        pltpu.VMEM((tm, qk), BF16),
                        pltpu.VMEM((tm // c * 8, qk), F32),
                        pltpu.VMEM((levels, tm, qk), BF16),
                        pltpu.VMEM((tm, heads * c), BF16),
                        pltpu.VMEM((tm, 3 * d), F32)],
        compiler_params=pltpu.CompilerParams(dimension_semantics=("arbitrary", "arbitrary"),
                                             vmem_limit_bytes=VMEM_LIMIT),
        name="mixer_prompt",
    )(h, *consts)


def _proj_kernel(h_ref, mn_ref, wa_ref, wb_ref, wlr_ref, wgk_ref, bgk_ref, z_ref, g_ref):
    d = h_ref.shape[1]
    qk = g_ref.shape[1]
    dk = qk // GLA_HEADS
    na = wa_ref.shape[1]
    xb = _rms(h_ref[...], mn_ref[...]).astype(BF16)
    z_ref[:, 0:na] = _dot(xb, wa_ref[...])
    z_ref[:, na:] = _dot(xb, wb_ref[...])
    z_ref[:, d:d + qk] = z_ref[:, d:d + qk] * (dk ** -0.5)
    lr = _dot(xb, wlr_ref[...]).astype(BF16)
    g_ref[...] = _log_sigmoid(_dot(lr, wgk_ref[...]) + bgk_ref[...]) * LOG2_DECAY_SCALE


def _proj(h, mn, wa, wb, wlr, wgk, bgk):
    m, d = h.shape
    qk = bgk.shape[1]
    wide = wa.shape[1] + wb.shape[1]
    return pl.pallas_call(
        _proj_kernel,
        grid=(1,),
        in_specs=[pl.BlockSpec((m, d), lambda i: (0, 0)),
                  _const_spec(mn.shape), _const_spec(wa.shape), _const_spec(wb.shape), _const_spec(wlr.shape),
                  _const_spec(wgk.shape), _const_spec(bgk.shape)],
        out_specs=[pl.BlockSpec((m, wide), lambda i: (0, 0)), pl.BlockSpec((m, qk), lambda i: (0, 0))],
        out_shape=[jax.ShapeDtypeStruct((m, wide), F32), jax.ShapeDtypeStruct((m, qk), F32)],
        compiler_params=pltpu.CompilerParams(dimension_semantics=("arbitrary",),
                                             vmem_limit_bytes=VMEM_LIMIT),
        name="proj_sample",
    )(h, mn, wa, wb, wlr, wgk, bgk)


def _sample_mixer_kernel(z_ref, g_ref, sp_ref, sg_ref, gn_ref, pw_ref, ps_ref, eye_ref,
                         mix_ref, npool_ref, ngla_ref,
                         qin_s, kdec_s, v_s, o_s, pl_s, ost_s, *, pos0):
    n_tok, bb, _ = z_ref.shape
    d = o_s.shape[1]
    heads = GLA_HEADS
    qk = g_ref.shape[2]
    dk, dv = qk // heads, d // heads
    o_u, o_q, o_k, o_v = 0, d, d + qk, d + 2 * qk
    o_g = o_v + d
    o_ga, o_gb = o_g + d, o_g + 2 * d
    gc = d // len(POOL_WINDOWS)
    plane = lambda t: slice(t * bb, (t + 1) * bb)

    @pl.when(pl.program_id(0) == 0)
    def _():
        pad = (SAMPLE_PAD - n_tok) * bb
        qin_s[:, (n_tok + 1) * bb:, :] = jnp.zeros((qk // LANE, pad - bb, LANE), F32)
        kdec_s[:, n_tok * bb:, :] = jnp.zeros((qk // LANE, pad, LANE), F32)
        v_s[:, n_tok * bb:, :] = jnp.zeros((d // LANE, pad, LANE), F32)

    b = []
    for t in range(n_tok):
        b.append(g_ref[t] if t == 0 else b[-1] + g_ref[t])
    b_last = b[-1]
    q = [z_ref[t, :, o_q:o_q + qk] for t in range(n_tok)]
    k = [z_ref[t, :, o_k:o_k + qk] for t in range(n_tok)]
    def stage(dst, t, val):
        for i in range(val.shape[1] // LANE):
            dst[i, plane(t), :] = val[:, i * LANE:(i + 1) * LANE]

    for t in range(n_tok):
        stage(qin_s, t, q[t] * jnp.exp2(b[t]))
        stage(kdec_s, t, k[t] * jnp.exp2(b_last - b[t]))
        stage(v_s, t, z_ref[t, :, o_v:o_v + d])
    stage(qin_s, n_tok, jnp.exp2(b_last))
    for t in range(n_tok):
        o_t = [None] * heads
        for j in range(t + 1):
            w_tj = q[t] * k[j] if j == t else q[t] * k[j] * jnp.exp2(b[t] - b[j])
            for hd in range(heads):
                a_tj = jnp.sum(w_tj[:, hd * dk:(hd + 1) * dk], axis=-1, keepdims=True)
                term = a_tj * z_ref[j, :, o_v + hd * dv:o_v + (hd + 1) * dv]
                o_t[hd] = term if o_t[hd] is None else o_t[hd] + term
        for hd in range(heads):
            o_s[plane(t), hd * dv:(hd + 1) * dv] = o_t[hd]

    def seq(n, carry):
        tokens = pl.ds(n, SAMPLE_PAD, stride=bb)
        own_rows = pl.ds(pl.multiple_of(n * SAMPLE_PAD, SAMPLE_PAD), SAMPLE_PAD)
        for hd in range(heads):
            state = sg_ref[n, hd]
            q_tok = qin_s[hd, tokens, :]
            o_state = _dot(q_tok.astype(BF16), state.astype(BF16))
            for i in range(dv // LANE):
                ost_s[hd * (dv // LANE) + i, own_rows, :] = o_state[:, i * LANE:(i + 1) * LANE]
            decay = _column(q_tok[n_tok:n_tok + 1, :], eye_ref[...])
            v_tok = jnp.concatenate([v_s[hd * (dv // LANE) + i, tokens, :] for i in range(dv // LANE)], axis=1)
            ngla_ref[n, hd] = state * decay + _dot_tn(kdec_s[hd, tokens, :].astype(BF16), v_tok.astype(BF16))
        return carry

    lax.fori_loop(0, bb, seq, 0, unroll=2)
    for t in range(n_tok):
        for i in range(d // LANE):
            o_s[plane(t), i * LANE:(i + 1) * LANE] += ost_s[i, pl.ds(t, bb, stride=SAMPLE_PAD), :]

    def pool_row(j):
        return z_ref[j, :, o_u:o_u + d] if j >= 0 else sp_ref[POOL_BUF + j]

    for t in range(n_tok):
        for grp, w in enumerate(POOL_WINDOWS):
            cols = slice(grp * gc, (grp + 1) * gc)
            s = pool_row(t)[:, cols]
            for dlt in range(1, w):
                s = s + pool_row(t - dlt)[:, cols]
            cnt = float(min(pos0 + 1 + t, w))
            pl_s[plane(t), cols] = s * (1.0 / cnt) - pool_row(t)[:, cols]
    for r in range(POOL_BUF):
        npool_ref[r] = pool_row(r + n_tok - POOL_BUF)

    rows = n_tok * bb
    gn = gn_ref[...]
    for hd in range(heads):
        hv = slice(hd * dv, (hd + 1) * dv)
        zcols = lambda off: z_ref[:, :, off + hd * dv:off + (hd + 1) * dv].reshape(rows, dv)
        a_out = _dot(pl_s[:, hv].astype(BF16), pw_ref[hd]) * ps_ref[:, hv]
        on = _rms(o_s[:, hv], gn)
        mix = _sigmoid(zcols(o_ga)) * a_out + _sigmoid(zcols(o_gb)) * (on * _silu(zcols(o_g)))
        mix_ref[:, :, hv] = mix.reshape(n_tok, bb, dv).astype(mix_ref.dtype)


def _sample_mixer(z, g, sp, sg, gn, pw, ps, eye):
    n_tok, nb, wide = z.shape
    qk = g.shape[2]
    d = sp.shape[2]
    heads = GLA_HEADS
    dk, dv = qk // heads, d // heads
    assert d // len(POOL_WINDOWS) == dv, "pool groups and attention heads share a column split here"
    assert n_tok < SAMPLE_PAD and dk == LANE and dv % LANE == 0
    bb = min(SAMPLE_BLOCK, nb)
    kern = functools.partial(_sample_mixer_kernel, pos0=PAST_LEN)
    blk3 = lambda a, b: pl.BlockSpec((a, bb, b), lambda i: (0, i, 0))
    state_spec = pl.BlockSpec((bb, heads, dk, dv), lambda i: (i, 0, 0, 0))
    return pl.pallas_call(
        kern,
        grid=(nb // bb,),
        in_specs=[blk3(n_tok, wide), blk3(n_tok, qk), blk3(POOL_BUF, d), state_spec,
                  _const_spec(gn.shape), _const_spec(pw.shape), _const_spec(ps.shape), _const_spec(eye.shape)],
        out_specs=[blk3(n_tok, d), blk3(POOL_BUF, d), state_spec],
        out_shape=[jax.ShapeDtypeStruct((n_tok, nb, d), BF16),
                   jax.ShapeDtypeStruct((POOL_BUF, nb, d), F32),
                   jax.ShapeDtypeStruct((nb, heads, dk, dv), F32)],
        scratch_shapes=[pltpu.VMEM((qk // LANE, SAMPLE_PAD * bb, LANE), F32),
                        pltpu.VMEM((qk // LANE, SAMPLE_PAD * bb, LANE), F32),
                        pltpu.VMEM((d // LANE, SAMPLE_PAD * bb, LANE), F32),
                        pltpu.VMEM((n_tok * bb, d), F32),
                        pltpu.VMEM((n_tok * bb, d), F32),
                        pltpu.VMEM((d // LANE, SAMPLE_PAD * bb, LANE), F32)],
        compiler_params=pltpu.CompilerParams(dimension_semantics=("arbitrary",),
                                             vmem_limit_bytes=VMEM_LIMIT),
        name="mixer_sample",
    )(z, g, sp, sg, gn, pw, ps, eye)


def _pack_kernel(a_ref, b_ref, lr_ref, wa_ref, wb_ref, wlr_ref, *, rank):
    wa_ref[...] = a_ref[...].T.astype(BF16)
    wb_ref[...] = b_ref[...].T.astype(BF16)
    lane = lax.broadcasted_iota(jnp.int32, wlr_ref.shape, 1)
    wlr_ref[...] = jnp.where(lane < rank, lr_ref[...].T, 0.0).astype(BF16)


def _pack_w_in(wt, split, rank):
    n, d = wt.shape
    rest = n - split - rank
    blk = GATE_BLOCK
    assert split % blk == 0 and rest % blk == 0 and split // blk == rest // blk and split % LANE == 0
    return pl.pallas_call(
        functools.partial(_pack_kernel, rank=rank),
        grid=(split // blk,),
        in_specs=[pl.BlockSpec((blk, d), lambda i: (i, 0)),
                  pl.BlockSpec((pl.Element(blk), pl.Element(d)), lambda i: (pl.multiple_of(split + rank + i * blk, 8), 0)),
                  pl.BlockSpec((LANE, d), lambda i: (split // LANE, 0))],
        out_specs=[pl.BlockSpec((d, blk), lambda i: (0, i)), pl.BlockSpec((d, blk), lambda i: (0, i)),
                   pl.BlockSpec((d, LANE), lambda i: (0, 0))],
        out_shape=[jax.ShapeDtypeStruct((d, split), BF16), jax.ShapeDtypeStruct((d, rest), BF16),
                   jax.ShapeDtypeStruct((d, LANE), BF16)],
        compiler_params=pltpu.CompilerParams(dimension_semantics=("arbitrary",)),
        name="pack_w_in",
    )(wt, wt, wt)


def _layer_weights(i, ffn1_norm, ffn1_w_gate, ffn1_w_up, ffn1_w_down, mix_norm, w_in, w_gk_up, b_gk,
                   gla_norm, pool_w, pool_scale, w_out, ffn2_norm, ffn2_w_gate, ffn2_w_up, ffn2_w_down,
                   ple_norm, w_ple_gate, w_ple_proj):
    d = w_in.shape[1]
    qk = w_gk_up.shape[2]
    rank = w_gk_up.shape[1]
    o_lr = 2 * d + 2 * qk
    wa, wb, wlr = _pack_w_in(jnp.swapaxes(w_in[i], 0, 1), o_lr, rank)
    wgk = jnp.concatenate([w_gk_up[i], jnp.zeros((LANE - rank, qk), F32)], axis=0).astype(BF16)
    row = lambda a: a[i].reshape(1, -1)
    return dict(
        ffn1=(row(ffn1_norm), ffn1_w_gate[i].astype(BF16), ffn1_w_up[i].astype(BF16), ffn1_w_down[i].astype(BF16)),
        proj=(row(mix_norm), wa, wb, wlr, wgk, row(b_gk)),
        mix=(row(gla_norm), pool_w[i].astype(BF16), row(pool_scale)),
        tail=(w_out[i].astype(BF16), row(ffn2_norm), ffn2_w_gate[i].astype(BF16), ffn2_w_up[i].astype(BF16),
              ffn2_w_down[i].astype(BF16), row(ple_norm), w_ple_gate[i].astype(BF16), w_ple_proj[i].astype(BF16)),
    )


def kernel(x_prompt, x_sample, p_prompt, p_sample, state_pool, state_gla, ffn1_norm, ffn1_w_gate, ffn1_w_up, ffn1_w_down, mix_norm, w_in, w_gk_up, b_gk, gla_norm, pool_w, pool_scale, w_out, ffn2_norm, ffn2_w_gate, ffn2_w_up, ffn2_w_down, ple_norm, w_ple_gate, w_ple_proj, final_norm):
    depth = w_in.shape[0]
    batch, seq, d = x_prompt.shape
    nb, n_tok, _ = x_sample.shape
    assert depth == 1, "the final norm is fused into the tail kernel of the only layer"
    assert n_tok <= 4 and POOL_BUF <= HIST and seq % GLA_CHUNK == 0
    tri, mk = _gla_constants(GLA_CHUNK)
    tri = jnp.asarray(tri, BF16)
    mk = jnp.asarray(mk, F32)
    eye = jnp.eye(w_gk_up.shape[2] // GLA_HEADS, dtype=F32)
    fn = final_norm.reshape(1, -1)

    lw = _layer_weights(0, ffn1_norm, ffn1_w_gate, ffn1_w_up, ffn1_w_down, mix_norm, w_in, w_gk_up, b_gk,
                        gla_norm, pool_w, pool_scale, w_out, ffn2_norm, ffn2_w_gate, ffn2_w_up, ffn2_w_down,
                        ple_norm, w_ple_gate, w_ple_proj)
    h1 = _ffn1(x_prompt.reshape(batch * seq, d), *lw["ffn1"])
    mix, pool_p, gla_p = _mixer(h1, batch, *lw["proj"], *lw["mix"], tri, mk, eye)
    y_prompt = _tail(h1, mix, p_prompt[0].reshape(batch * seq, -1), *lw["tail"], fn)
    tok_major = lambda a: jnp.swapaxes(a, 0, 1)
    h1 = _ffn1(tok_major(x_sample).reshape(n_tok * nb, d), *lw["ffn1"])
    z, g = _proj(h1, *lw["proj"])
    mix, pool_s, gla_s = _sample_mixer(z.reshape(n_tok, nb, -1), g.reshape(n_tok, nb, -1),
                                       tok_major(state_pool[0]), state_gla[0], *lw["mix"], eye)
    y_sample = _tail(h1, mix.reshape(n_tok * nb, d), tok_major(p_sample[0]).reshape(n_tok * nb, -1),
                     *lw["tail"], fn)
    return (y_prompt.reshape(batch, seq, d), tok_major(y_sample.reshape(n_tok, nb, d)),
            pool_p[None], gla_p[None], tok_major(pool_s)[None], gla_s[None])
```

```python
import functools

import numpy as np
import jax
import jax.numpy as jnp
from jax import lax
from jax.experimental import pallas as pl
from jax.experimental.pallas import tpu as pltpu

POOL_WINDOWS = (2, 4, 8, 16)
POOL_BUF = max(POOL_WINDOWS) - 1
GLA_HEADS = 4
GATE_NORMALIZER = 16.0
EPS = 1e-6
PAST_LEN = 16384

ROW_TILE = 512
GLA_CHUNK = 128
HIST = 16
POOL_PAD = 16
SAMPLE_PAD = 8
SAMPLE_BLOCK = 16
LANE = 128
GATE_BLOCK = 256
LOG2_DECAY_SCALE = float(np.log2(np.e)) / GATE_NORMALIZER
VMEM_LIMIT = 56 * 1024 * 1024

F32 = jnp.float32
BF16 = jnp.bfloat16


def _dot(a, b):
    return jnp.dot(a, b, preferred_element_type=F32)


def _dot_nt(a, b):
    return lax.dot_general(a, b, (((1,), (1,)), ((), ())), preferred_element_type=F32)


def _dot_tn(a, b):
    return lax.dot_general(a, b, (((0,), (0,)), ((), ())), preferred_element_type=F32)


def _rms(x, w):
    ms = jnp.mean(x * x, axis=-1, keepdims=True)
    return x * lax.rsqrt(ms + EPS) * w


def _sigmoid(x):
    return 1.0 / (1.0 + jnp.exp(-x))


def _silu(x):
    return x * _sigmoid(x)


def _log_sigmoid(x):
    return jnp.minimum(x, 0.0) - jnp.log1p(jnp.exp(-jnp.abs(x)))


def _ff_chunks(f):
    step = 1024
    return [(lo, min(lo + step, f)) for lo in range(0, f, step)]


def _const_spec(shape):
    nd = len(shape)
    return pl.BlockSpec(shape, lambda *_: (0,) * nd, pipeline_mode=pl.Buffered(1))


def _swiglu_acc(xb, wg_ref, wu_ref, wd_ref):
    acc = None
    for lo, hi in _ff_chunks(wg_ref.shape[1]):
        g = _dot(xb, wg_ref[:, lo:hi])
        u = _dot(xb, wu_ref[:, lo:hi])
        a = (_silu(g) * u).astype(BF16)
        d = _dot(a, wd_ref[lo:hi, :])
        acc = d if acc is None else acc + d
    return acc


def _ffn1_kernel(xs_ref, xp_ref, nw_ref, wg_ref, wu_ref, wd_ref, o_ref, *, n_s):
    from_sample = pl.program_id(0) < n_s
    half = xs_ref.shape[0] // 2
    for r in (slice(0, half), slice(half, 2 * half)):
        x = jnp.where(from_sample, xs_ref[r, :], xp_ref[r, :])
        xb = _rms(x, nw_ref[...]).astype(BF16)
        o_ref[r, :] = x + 0.5 * _swiglu_acc(xb, wg_ref, wu_ref, wd_ref)


def _tile_counts(ms, mp):
    tm = ROW_TILE
    assert ms % tm == 0 and mp % tm == 0
    return tm, ms // tm, mp // tm


def _ffn1(xs, xp, nw, wg, wu, wd):
    (ms, d), mp = xs.shape, xp.shape[0]
    f = wg.shape[1]
    tm, n_s, n_p = _tile_counts(ms, mp)
    return pl.pallas_call(
        functools.partial(_ffn1_kernel, n_s=n_s),
        grid=(n_s + n_p,),
        in_specs=[pl.BlockSpec((tm, d), lambda i: (jnp.minimum(i, n_s - 1), 0)),
                  pl.BlockSpec((tm, d), lambda i: (jnp.maximum(i - n_s, 0), 0)),
                  _const_spec((1, d)), _const_spec((d, f)), _const_spec((d, f)), _const_spec((f, d))],
        out_specs=pl.BlockSpec((tm, d), lambda i: (i, 0)),
        out_shape=jax.ShapeDtypeStruct((ms + mp, d), F32),
        compiler_params=pltpu.CompilerParams(dimension_semantics=("arbitrary",),
                                             vmem_limit_bytes=VMEM_LIMIT),
        name="ffn1",
    )(xs, xp, nw, wg, wu, wd)


def _tail_kernel(h_ref, mixs_ref, mixp_ref, ps_ref, pp_ref, wo_ref, nw_ref, wg_ref, wu_ref, wd_ref,
                 pn_ref, wpg_ref, wpp_ref, fn_ref, ys_ref, yp_ref, *, n_s):
    from_sample = pl.program_id(0) < n_s
    mix = jnp.where(from_sample, mixs_ref[...], mixp_ref[...])
    p = jnp.where(from_sample, ps_ref[...], pp_ref[...])
    h = h_ref[...] + _dot(mix, wo_ref[...])
    xb = _rms(h, nw_ref[...]).astype(BF16)
    h = h + 0.5 * _swiglu_acc(xb, wg_ref, wu_ref, wd_ref)
    gate = _sigmoid(_dot(_rms(h, pn_ref[...]).astype(BF16), wpg_ref[...]))
    h = h + gate * _dot(p.astype(BF16), wpp_ref[...])
    y = _rms(h, fn_ref[...])
    ys_ref[...] = y
    yp_ref[...] = y


def _tail(h, mix_s, mix_p, p_s, p_p, wo, nw, wg, wu, wd, pn, wpg, wpp, fn):
    d = h.shape[1]
    ms, mp = mix_s.shape[0], mix_p.shape[0]
    f = wg.shape[1]
    pd = p_s.shape[1]
    tm, n_s, n_p = _tile_counts(ms, mp)
    sample = lambda w: pl.BlockSpec((tm, w), lambda i: (jnp.minimum(i, n_s - 1), 0))
    prompt = lambda w: pl.BlockSpec((tm, w), lambda i: (jnp.maximum(i - n_s, 0), 0))
    ys, yp = pl.pallas_call(
        functools.partial(_tail_kernel, n_s=n_s),
        grid=(n_s + n_p,),
        in_specs=[pl.BlockSpec((tm, d), lambda i: (i, 0)), sample(d), prompt(d), sample(pd), prompt(pd),
                  _const_spec((d, d)), _const_spec((1, d)),
                  _const_spec((d, f)), _const_spec((d, f)), _const_spec((f, d)),
                  _const_spec((1, d)), _const_spec((d, d)), _const_spec((pd, d)), _const_spec((1, d))],
        out_specs=[pl.BlockSpec((tm, d), lambda i: (jnp.minimum(i, n_s), 0)), prompt(d)],
        out_shape=[jax.ShapeDtypeStruct((ms + tm, d), F32), jax.ShapeDtypeStruct((mp, d), F32)],
        compiler_params=pltpu.CompilerParams(dimension_semantics=("arbitrary",),
                                             vmem_limit_bytes=VMEM_LIMIT),
        name="tail",
    )(h, mix_s, mix_p, p_s, p_p, wo, nw, wg, wu, wd, pn, wpg, wpp, fn)
    return ys[:ms], yp


def _gla_constants(c):
    idx = np.arange(c)
    i, t = idx[:, None], idx[None, :]
    masks = []
    s = 1
    while s < c:
        same = (i // (2 * s)) == (t // (2 * s))
        masks.append(same & ((i % (2 * s)) >= s) & ((t % (2 * s)) < s))
        s *= 2
    tri = (t <= i).astype(np.float32)
    return tri, np.stack([m.astype(np.float32) for m in masks], axis=0)


def _level_exponents(g, b, b_ref, r0):
    c, n = g.shape
    row = lax.broadcasted_iota(jnp.int32, (c, n), 0)
    m4 = row & 3
    g_prev = pltpu.roll(g, 1, 0)
    g_next = pltpu.roll(g, c - 1, 0)
    out = [jnp.where((row & 1) == 1, g, 0.0),
           jnp.where(m4 == 0, g_next, jnp.where(m4 == 1, 0.0, jnp.where(m4 == 2, g, g + g_prev)))]
    s = 4
    while s < c:
        blocks = [jnp.broadcast_to(b_ref[r0 + m * 2 * s + s - 1:r0 + m * 2 * s + s, :], (2 * s, n))
                  for m in range(c // (2 * s))]
        ref = blocks[0] if len(blocks) == 1 else jnp.concatenate(blocks, axis=0)
        sign = jnp.where((lax.broadcasted_iota(jnp.int32, (c, 1), 0) & s) != 0, 1.0, -1.0)
        out.append((b - ref) * sign)
        s *= 2
    return out


def _split_hi_lo(g):
    hi = g.astype(BF16)
    lo = (g - hi.astype(F32)).astype(BF16)
    return jnp.concatenate([hi, lo], axis=1)


def _column(row, eye):
    return jnp.sum(eye * row, axis=-1, keepdims=True)


def _mixer_kernel(h_ref, mn_ref, wa_ref, wb_ref, wlr_ref, wgk_ref, bgk_ref, gn_ref, pw_ref, ps_ref,
                  tri_ref, mk_ref, eye_ref,
                  mix_ref, npool_ref, ngla_ref,
                  xb_s, ext_s, tmp_s, q_s, k_s, v_s, g_s, o_s, a_s, b_s, qin_s, kdec_s, dcy_s, m_s, att_s, gate_s,
                  *, pos0):
    tm, d = h_ref.shape
    heads = GLA_HEADS
    qk = q_s.shape[1]
    dk, dv = qk // heads, d // heads
    c = GLA_CHUNK
    n_chunks = tm // c
    levels = mk_ref.shape[0]
    l = pl.program_id(1)
    x0 = POOL_PAD + HIST
    o_q, o_k, o_v = d, d + qk, d + 2 * qk

    @pl.when(l == 0)
    def _():
        ext_s[0:x0, :] = jnp.zeros((x0, d), F32)
        ngla_ref[...] = jnp.zeros(ngla_ref.shape, F32)

    xb_s[...] = _rms(h_ref[...], mn_ref[...]).astype(BF16)
    q_s[...] = _dot(xb_s[...], wa_ref[:, o_q:o_q + qk]) * (dk ** -0.5)
    k_s[...] = _dot(xb_s[...], wa_ref[:, o_k:o_k + qk])
    lr = _dot(xb_s[...], wlr_ref[...]).astype(BF16)
    g_s[...] = _log_sigmoid(_dot(lr, wgk_ref[...]) + bgk_ref[...]) * LOG2_DECAY_SCALE

    def u_block(lo):
        ext_s[x0:x0 + tm, lo:lo + GATE_BLOCK] = _dot(xb_s[...], wa_ref[:, lo:lo + GATE_BLOCK])

    def v_block(lo):
        v_s[:, lo:lo + GATE_BLOCK] = _dot(xb_s[...], wa_ref[:, o_v + lo:o_v + lo + GATE_BLOCK]).astype(BF16)

    def gate_block(lo):
        z = _dot(xb_s[...], wb_ref[:, lo:lo + GATE_BLOCK])
        gate_s[:, lo:lo + GATE_BLOCK] = _silu(z) if lo < d else _sigmoid(z)

    early_work = ([functools.partial(u_block, lo) for lo in range(0, d, GATE_BLOCK)]
                  + [functools.partial(v_block, lo) for lo in range(0, d, GATE_BLOCK)])
    early_split = np.array_split(np.arange(len(early_work)), n_chunks)
    late_work = [functools.partial(gate_block, lo) for lo in range(0, 3 * d, GATE_BLOCK)]
    late_split = np.array_split(np.arange(len(late_work)), 2 * n_chunks)

    def head_cols(hd):
        return slice(hd * dk, (hd + 1) * dk), slice(hd * dv, (hd + 1) * dv), slice(hd * c, (hd + 1) * c)

    row = lax.broadcasted_iota(jnp.int32, (c, qk), 0)
    for ci in range(n_chunks):
        for idx in early_split[ci]:
            early_work[int(idx)]()
        r0 = ci * c
        rows = slice(r0, r0 + c)
        g = g_s[rows, :]
        sums = _dot(tri_ref[...], _split_hi_lo(g))
        b = sums[:, :qk] + sums[:, qk:]
        b_s[rows, :] = b
        b_last = b[c - 1:c, :]
        q = q_s[rows, :]
        k = k_s[rows, :]
        qin_s[rows, :] = (q * jnp.exp2(b)).astype(BF16)
        kdec_s[rows, :] = (k * jnp.exp2(b_last - b)).astype(BF16)
        dcy_s[ci * 8:ci * 8 + 1, :] = jnp.exp2(b_last)
        for lv, ex in enumerate(_level_exponents(g, b, b_s, r0)):
            operand = jnp.where((row & (1 << lv)) != 0, q, k) * jnp.exp2(ex)
            m_s[lv, rows, :] = operand.astype(BF16)

    pos = (pos0 + 1 + l * tm + lax.broadcasted_iota(jnp.int32, (tm, 1), 0))
    gc = d // len(POOL_WINDOWS)

    def pool_group(grp):
        w = POOL_WINDOWS[grp]
        cols = slice(grp * gc, (grp + 1) * gc)
        shifts = [1 << i for i in range(int(np.log2(w)))]
        starts = [x0] * len(shifts)
        for i in range(len(shifts) - 2, -1, -1):
            starts[i] = (starts[i + 1] - shifts[i + 1]) // 8 * 8
        for i, (sh, st) in enumerate(zip(shifts, starts)):
            if i == 0:
                val = ext_s[st:x0 + tm, cols] + ext_s[st - sh:x0 + tm - sh, cols]
            else:
                val = tmp_s[i - 1, st:x0 + tm, :] + tmp_s[i - 1, st - sh:x0 + tm - sh, :]
            if i < len(shifts) - 1:
                tmp_s[i, st:x0 + tm, :] = val
        cur = ext_s[x0:x0 + tm, cols]
        inv = 1.0 / jnp.minimum(pos, w).astype(F32)
        pooled = (val * inv - cur).astype(BF16)
        a_s[:, cols] = _dot(pooled, pw_ref[grp]) * ps_ref[:, cols]

    pool_split = np.array_split(np.arange(len(POOL_WINDOWS)), n_chunks)
    for ci in range(n_chunks):
        for idx in late_split[ci]:
            late_work[int(idx)]()
        rows = slice(ci * c, (ci + 1) * c)
        for hd in range(heads):
            hc, _, ha = head_cols(hd)
            att = None
            for lv in range(levels):
                operand = m_s[lv, rows, hc]
                t = mk_ref[lv] * _dot_nt(operand, operand)
                att = t if att is None else att + t
            att_s[rows, ha] = att.astype(BF16)
        for grp in pool_split[ci]:
            pool_group(int(grp))

    for ci in range(n_chunks):
        for idx in late_split[n_chunks + ci]:
            late_work[int(idx)]()
        rows = slice(ci * c, (ci + 1) * c)
        qk_prod = q_s[rows, :] * k_s[rows, :]
        for hd in range(heads):
            hc, hv, ha = head_cols(hd)
            state = ngla_ref[0, hd]
            vh = v_s[rows, hv]
            lhs = jnp.concatenate([qin_s[rows, hc], att_s[rows, ha]], axis=1)
            rhs = jnp.concatenate([state.astype(BF16), vh], axis=0)
            diag = jnp.sum(qk_prod[:, hc], axis=-1, keepdims=True)
            o_s[rows, hv] = _dot(lhs, rhs) + diag * vh.astype(F32)
            decay = _column(dcy_s[ci * 8:ci * 8 + 1, hc], eye_ref[...])
            ngla_ref[0, hd] = state * decay + _dot_tn(kdec_s[rows, hc], vh)

    gn = gn_ref[...]
    for hd in range(heads):
        hv = slice(hd * dv, (hd + 1) * dv)
        on = _rms(o_s[:, hv], gn)
        gate = gate_s[:, hd * dv:(hd + 1) * dv]
        ga = gate_s[:, d + hd * dv:d + (hd + 1) * dv]
        gb = gate_s[:, 2 * d + hd * dv:2 * d + (hd + 1) * dv]
        mix_ref[:, hv] = (ga * a_s[:, hv] + gb * (on * gate)).astype(BF16)

    @pl.when(l == pl.num_programs(1) - 1)
    def _():
        npool_ref[0] = ext_s[x0 + tm - POOL_BUF:x0 + tm, :]

    ext_s[POOL_PAD:x0, :] = ext_s[tm + POOL_PAD:tm + x0, :]


def _mixer(h, first_row, batch, seq, mn, wa, wb, wlr, wgk, bgk, gn, pw, ps, tri, mk, eye):
    d = h.shape[1]
    m = batch * seq
    tm = min(ROW_TILE, seq)
    nl = seq // tm
    qk = bgk.shape[1]
    heads = GLA_HEADS
    dk, dv = qk // heads, d // heads
    c = GLA_CHUNK
    levels = mk.shape[0]
    gc = d // len(POOL_WINDOWS)
    n_stage = int(np.log2(max(POOL_WINDOWS))) - 1
    consts = (mn, wa, wb, wlr, wgk, bgk, gn, pw, ps, tri, mk, eye)
    kern = functools.partial(_mixer_kernel, pos0=0)
    assert first_row % tm == 0
    tile0 = first_row // tm
    return pl.pallas_call(
        kern,
        grid=(batch, nl),
        in_specs=([pl.BlockSpec((tm, d), lambda b, l: (tile0 + b * nl + l, 0))]
                  + [_const_spec(a.shape) for a in consts]),
        out_specs=[pl.BlockSpec((tm, d), lambda b, l: (b * nl + l, 0)),
                   pl.BlockSpec((1, POOL_BUF, d), lambda b, l: (b, 0, 0)),
                   pl.BlockSpec((1, heads, dk, dv), lambda b, l: (b, 0, 0, 0))],
        out_shape=[jax.ShapeDtypeStruct((m, d), BF16),
                   jax.ShapeDtypeStruct((batch, POOL_BUF, d), F32),
                   jax.ShapeDtypeStruct((batch, heads, dk, dv), F32)],
        scratch_shapes=[pltpu.VMEM((tm, d), BF16),
                        pltpu.VMEM((POOL_PAD + HIST + tm, d), F32),
                        pltpu.VMEM((n_stage, POOL_PAD + HIST + tm, gc), F32),
                        pltpu.VMEM((tm, qk), F32),
                        pltpu.VMEM((tm, qk), F32),
                        pltpu.VMEM((tm, d), BF16),
                        pltpu.VMEM((tm, qk), F32),
                        pltpu.VMEM((tm, d), F32),
                        pltpu.VMEM((tm, d), F32),
                        pltpu.VMEM((tm, qk), F32),
                        pltpu.VMEM((tm, qk), BF16),
                        pltpu.VMEM((tm, qk), BF16),
                        pltpu.VMEM((tm // c * 8, qk), F32),
                        pltpu.VMEM((levels, tm, qk), BF16),
                        pltpu.VMEM((tm, heads * c), BF16),
                        pltpu.VMEM((tm, 3 * d), F32)],
        compiler_params=pltpu.CompilerParams(dimension_semantics=("arbitrary", "arbitrary"),
                                             vmem_limit_bytes=VMEM_LIMIT),
        name="mixer_prompt",
    )(h, *consts)


def _proj_kernel(h_ref, mn_ref, wa_ref, wb_ref, wlr_ref, wgk_ref, bgk_ref, z_ref, g_ref):
    d = h_ref.shape[1]
    qk = g_ref.shape[1]
    dk = qk // GLA_HEADS
    na = wa_ref.shape[1]
    xb = _rms(h_ref[...], mn_ref[...]).astype(BF16)
    z_ref[:, 0:na] = _dot(xb, wa_ref[...])
    z_ref[:, na:] = _dot(xb, wb_ref[...])
    z_ref[:, d:d + qk] = z_ref[:, d:d + qk] * (dk ** -0.5)
    lr = _dot(xb, wlr_ref[...]).astype(BF16)
    g_ref[...] = _log_sigmoid(_dot(lr, wgk_ref[...]) + bgk_ref[...]) * LOG2_DECAY_SCALE


def _proj(h, m, mn, wa, wb, wlr, wgk, bgk):
    d = h.shape[1]
    qk = bgk.shape[1]
    wide = wa.shape[1] + wb.shape[1]
    return pl.pallas_call(
        _proj_kernel,
        grid=(1,),
        in_specs=[pl.BlockSpec((m, d), lambda i: (0, 0)),
                  _const_spec(mn.shape), _const_spec(wa.shape), _const_spec(wb.shape), _const_spec(wlr.shape),
                  _const_spec(wgk.shape), _const_spec(bgk.shape)],
        out_specs=[pl.BlockSpec((m, wide), lambda i: (0, 0)), pl.BlockSpec((m, qk), lambda i: (0, 0))],
        out_shape=[jax.ShapeDtypeStruct((m, wide), F32), jax.ShapeDtypeStruct((m, qk), F32)],
        compiler_params=pltpu.CompilerParams(dimension_semantics=("arbitrary",),
                                             vmem_limit_bytes=VMEM_LIMIT),
        name="proj_sample",
    )(h, mn, wa, wb, wlr, wgk, bgk)


def _sample_mixer_kernel(z_ref, g_ref, sp_ref, sg_ref, gn_ref, pw_ref, ps_ref, eye_ref,
                         mix_ref, npool_ref, ngla_ref,
                         qin_s, kdec_s, v_s, o_s, pl_s, ost_s, *, pos0):
    n_tok, bb, _ = z_ref.shape
    d = o_s.shape[1]
    heads = GLA_HEADS
    qk = g_ref.shape[2]
    dk, dv = qk // heads, d // heads
    o_u, o_q, o_k, o_v = 0, d, d + qk, d + 2 * qk
    o_g = o_v + d
    o_ga, o_gb = o_g + d, o_g + 2 * d
    gc = d // len(POOL_WINDOWS)
    plane = lambda t: slice(t * bb, (t + 1) * bb)

    @pl.when(pl.program_id(0) == 0)
    def _():
        pad = (SAMPLE_PAD - n_tok) * bb
        qin_s[:, (n_tok + 1) * bb:, :] = jnp.zeros((qk // LANE, pad - bb, LANE), F32)
        kdec_s[:, n_tok * bb:, :] = jnp.zeros((qk // LANE, pad, LANE), F32)
        v_s[:, n_tok * bb:, :] = jnp.zeros((d // LANE, pad, LANE), F32)

    b = []
    for t in range(n_tok):
        b.append(g_ref[t] if t == 0 else b[-1] + g_ref[t])
    b_last = b[-1]
    q = [z_ref[t, :, o_q:o_q + qk] for t in range(n_tok)]
    k = [z_ref[t, :, o_k:o_k + qk] for t in range(n_tok)]
    def stage(dst, t, val):
        for i in range(val.shape[1] // LANE):
            dst[i, plane(t), :] = val[:, i * LANE:(i + 1) * LANE]

    for t in range(n_tok):
        stage(qin_s, t, q[t] * jnp.exp2(b[t]))
        stage(kdec_s, t, k[t] * jnp.exp2(b_last - b[t]))
        stage(v_s, t, z_ref[t, :, o_v:o_v + d])
    stage(qin_s, n_tok, jnp.exp2(b_last))
    for t in range(n_tok):
        o_t = [None] * heads
        for j in range(t + 1):
            w_tj = q[t] * k[j] if j == t else q[t] * k[j] * jnp.exp2(b[t] - b[j])
            for hd in range(heads):
                a_tj = jnp.sum(w_tj[:, hd * dk:(hd + 1) * dk], axis=-1, keepdims=True)
                term = a_tj * z_ref[j, :, o_v + hd * dv:o_v + (hd + 1) * dv]
                o_t[hd] = term if o_t[hd] is None else o_t[hd] + term
        for hd in range(heads):
            o_s[plane(t), hd * dv:(hd + 1) * dv] = o_t[hd]

    def seq(n, carry):
        tokens = pl.ds(n, SAMPLE_PAD, stride=bb)
        own_rows = pl.ds(pl.multiple_of(n * SAMPLE_PAD, SAMPLE_PAD), SAMPLE_PAD)
        for hd in range(heads):
            state = sg_ref[n, hd]
            q_tok = qin_s[hd, tokens, :]
            o_state = _dot(q_tok.astype(BF16), state.astype(BF16))
            for i in range(dv // LANE):
                ost_s[hd * (dv // LANE) + i, own_rows, :] = o_state[:, i * LANE:(i + 1) * LANE]
            decay = _column(q_tok[n_tok:n_tok + 1, :], eye_ref[...])
            v_tok = jnp.concatenate([v_s[hd * (dv // LANE) + i, tokens, :] for i in range(dv // LANE)], axis=1)
            ngla_ref[n, hd] = state * decay + _dot_tn(kdec_s[hd, tokens, :].astype(BF16), v_tok.astype(BF16))
        return carry

    lax.fori_loop(0, bb, seq, 0, unroll=2)
    for t in range(n_tok):
        for i in range(d // LANE):
            o_s[plane(t), i * LANE:(i + 1) * LANE] += ost_s[i, pl.ds(t, bb, stride=SAMPLE_PAD), :]

    def pool_row(j):
        return z_ref[j, :, o_u:o_u + d] if j >= 0 else sp_ref[POOL_BUF + j]

    for t in range(n_tok):
        for grp, w in enumerate(POOL_WINDOWS):
            cols = slice(grp * gc, (grp + 1) * gc)
            s = pool_row(t)[:, cols]
            for dlt in range(1, w):
                s = s + pool_row(t - dlt)[:, cols]
            cnt = float(min(pos0 + 1 + t, w))
            pl_s[plane(t), cols] = s * (1.0 / cnt) - pool_row(t)[:, cols]
    for r in range(POOL_BUF):
        npool_ref[r] = pool_row(r + n_tok - POOL_BUF)

    rows = n_tok * bb
    gn = gn_ref[...]
    for hd in range(heads):
        hv = slice(hd * dv, (hd + 1) * dv)
        zcols = lambda off: z_ref[:, :, off + hd * dv:off + (hd + 1) * dv].reshape(rows, dv)
        a_out = _dot(pl_s[:, hv].astype(BF16), pw_ref[hd]) * ps_ref[:, hv]
        on = _rms(o_s[:, hv], gn)
        mix = _sigmoid(zcols(o_ga)) * a_out + _sigmoid(zcols(o_gb)) * (on * _silu(zcols(o_g)))
        mix_ref[:, :, hv] = mix.reshape(n_tok, bb, dv).astype(mix_ref.dtype)


def _sample_mixer(z, g, sp, sg, gn, pw, ps, eye):
    n_tok, nb, wide = z.shape
    qk = g.shape[2]
    d = sp.shape[2]
    heads = GLA_HEADS
    dk, dv = qk // heads, d // heads
    assert d // len(POOL_WINDOWS) == dv, "pool groups and attention heads share a column split here"
    assert n_tok < SAMPLE_PAD and dk == LANE and dv % LANE == 0
    bb = min(SAMPLE_BLOCK, nb)
    kern = functools.partial(_sample_mixer_kernel, pos0=PAST_LEN)
    blk3 = lambda a, b: pl.BlockSpec((a, bb, b), lambda i: (0, i, 0))
    state_spec = pl.BlockSpec((bb, heads, dk, dv), lambda i: (i, 0, 0, 0))
    return pl.pallas_call(
        kern,
        grid=(nb // bb,),
        in_specs=[blk3(n_tok, wide), blk3(n_tok, qk), blk3(POOL_BUF, d), state_spec,
                  _const_spec(gn.shape), _const_spec(pw.shape), _const_spec(ps.shape), _const_spec(eye.shape)],
        out_specs=[blk3(n_tok, d), blk3(POOL_BUF, d), state_spec],
        out_shape=[jax.ShapeDtypeStruct((n_tok, nb, d), BF16),
                   jax.ShapeDtypeStruct((POOL_BUF, nb, d), F32),
                   jax.ShapeDtypeStruct((nb, heads, dk, dv), F32)],
        scratch_shapes=[pltpu.VMEM((qk // LANE, SAMPLE_PAD * bb, LANE), F32),
                        pltpu.VMEM((qk // LANE, SAMPLE_PAD * bb, LANE), F32),
                        pltpu.VMEM((d // LANE, SAMPLE_PAD * bb, LANE), F32),
                        pltpu.VMEM((n_tok * bb, d), F32),
                        pltpu.VMEM((n_tok * bb, d), F32),
                        pltpu.VMEM((d // LANE, SAMPLE_PAD * bb, LANE), F32)],
        compiler_params=pltpu.CompilerParams(dimension_semantics=("arbitrary",),
                                             vmem_limit_bytes=VMEM_LIMIT),
        name="mixer_sample",
    )(z, g, sp, sg, gn, pw, ps, eye)


def _pack_kernel(a_ref, b_ref, lr_ref, wa_ref, wb_ref, wlr_ref, *, rank):
    wa_ref[...] = a_ref[...].T.astype(BF16)
    wb_ref[...] = b_ref[...].T.astype(BF16)
    lane = lax.broadcasted_iota(jnp.int32, wlr_ref.shape, 1)
    wlr_ref[...] = jnp.where(lane < rank, lr_ref[...].T, 0.0).astype(BF16)


def _pack_w_in(wt, split, rank):
    n, d = wt.shape
    rest = n - split - rank
    blk = GATE_BLOCK
    assert split % blk == 0 and rest % blk == 0 and split // blk == rest // blk and split % LANE == 0
    return pl.pallas_call(
        functools.partial(_pack_kernel, rank=rank),
        grid=(split // blk,),
        in_specs=[pl.BlockSpec((blk, d), lambda i: (i, 0)),
                  pl.BlockSpec((pl.Element(blk), pl.Element(d)), lambda i: (pl.multiple_of(split + rank + i * blk, 8), 0)),
                  pl.BlockSpec((LANE, d), lambda i: (split // LANE, 0))],
        out_specs=[pl.BlockSpec((d, blk), lambda i: (0, i)), pl.BlockSpec((d, blk), lambda i: (0, i)),
                   pl.BlockSpec((d, LANE), lambda i: (0, 0))],
        out_shape=[jax.ShapeDtypeStruct((d, split), BF16), jax.ShapeDtypeStruct((d, rest), BF16),
                   jax.ShapeDtypeStruct((d, LANE), BF16)],
        compiler_params=pltpu.CompilerParams(dimension_semantics=("arbitrary",)),
        name="pack_w_in",
    )(wt, wt, wt)


def _layer_weights(i, ffn1_norm, ffn1_w_gate, ffn1_w_up, ffn1_w_down, mix_norm, w_in, w_gk_up, b_gk,
                   gla_norm, pool_w, pool_scale, w_out, ffn2_norm, ffn2_w_gate, ffn2_w_up, ffn2_w_down,
                   ple_norm, w_ple_gate, w_ple_proj):
    d = w_in.shape[1]
    qk = w_gk_up.shape[2]
    rank = w_gk_up.shape[1]
    o_lr = 2 * d + 2 * qk
    wa, wb, wlr = _pack_w_in(jnp.swapaxes(w_in[i], 0, 1), o_lr, rank)
    wgk = jnp.concatenate([w_gk_up[i], jnp.zeros((LANE - rank, qk), F32)], axis=0).astype(BF16)
    row = lambda a: a[i].reshape(1, -1)
    return dict(
        ffn1=(row(ffn1_norm), ffn1_w_gate[i].astype(BF16), ffn1_w_up[i].astype(BF16), ffn1_w_down[i].astype(BF16)),
        proj=(row(mix_norm), wa, wb, wlr, wgk, row(b_gk)),
        mix=(row(gla_norm), pool_w[i].astype(BF16), row(pool_scale)),
        tail=(w_out[i].astype(BF16), row(ffn2_norm), ffn2_w_gate[i].astype(BF16), ffn2_w_up[i].astype(BF16),
              ffn2_w_down[i].astype(BF16), row(ple_norm), w_ple_gate[i].astype(BF16), w_ple_proj[i].astype(BF16)),
    )


def kernel(x_prompt, x_sample, p_prompt, p_sample, state_pool, state_gla, ffn1_norm, ffn1_w_gate, ffn1_w_up, ffn1_w_down, mix_norm, w_in, w_gk_up, b_gk, gla_norm, pool_w, pool_scale, w_out, ffn2_norm, ffn2_w_gate, ffn2_w_up, ffn2_w_down, ple_norm, w_ple_gate, w_ple_proj, final_norm):
    depth = w_in.shape[0]
    batch, seq, d = x_prompt.shape
    nb, n_tok, _ = x_sample.shape
    assert depth == 1, "the final norm is fused into the tail kernel of the only layer"
    assert n_tok <= 4 and POOL_BUF <= HIST and seq % GLA_CHUNK == 0
    tri, mk = _gla_constants(GLA_CHUNK)
    tri = jnp.asarray(tri, BF16)
    mk = jnp.asarray(mk, F32)
    eye = jnp.eye(w_gk_up.shape[2] // GLA_HEADS, dtype=F32)
    fn = final_norm.reshape(1, -1)

    lw = _layer_weights(0, ffn1_norm, ffn1_w_gate, ffn1_w_up, ffn1_w_down, mix_norm, w_in, w_gk_up, b_gk,
                        gla_norm, pool_w, pool_scale, w_out, ffn2_norm, ffn2_w_gate, ffn2_w_up, ffn2_w_down,
                        ple_norm, w_ple_gate, w_ple_proj)
    tok_major = lambda a: jnp.swapaxes(a, 0, 1)
    ms, mp = n_tok * nb, batch * seq
    h1 = _ffn1(tok_major(x_sample).reshape(ms, d), x_prompt.reshape(mp, d), *lw["ffn1"])
    mix_p, pool_p, gla_p = _mixer(h1, ms, batch, seq, *lw["proj"], *lw["mix"], tri, mk, eye)
    z, g = _proj(h1, ms, *lw["proj"])
    mix_s, pool_s, gla_s = _sample_mixer(z.reshape(n_tok, nb, -1), g.reshape(n_tok, nb, -1),
                                         tok_major(state_pool[0]), state_gla[0], *lw["mix"], eye)
    y_sample, y_prompt = _tail(h1, mix_s.reshape(ms, d), mix_p, tok_major(p_sample[0]).reshape(ms, -1),
                               p_prompt[0].reshape(mp, -1), *lw["tail"], fn)
    return (y_prompt.reshape(batch, seq, d), tok_major(y_sample.reshape(n_tok, nb, d)),
            pool_p[None], gla_p[None], tok_major(pool_s)[None], gla_s[None])
```

```python
import functools

import numpy as np
import jax
import jax.numpy as jnp
from jax import lax
from jax.experimental import pallas as pl
from jax.experimental.pallas import tpu as pltpu

POOL_WINDOWS = (2, 4, 8, 16)
POOL_BUF = max(POOL_WINDOWS) - 1
GLA_HEADS = 4
GATE_NORMALIZER = 16.0
EPS = 1e-6
PAST_LEN = 16384

ROW_TILE = 512
GLA_CHUNK = 128
HIST = 16
POOL_PAD = 16
LANE = 128
SUBLANE = 8
SAMPLE_PAD = SUBLANE
SAMPLE_BLOCK = 16
GATE_BLOCK = 256
FFN_CHUNK = 6 * GATE_BLOCK
LOG2_DECAY_SCALE = float(np.log2(np.e)) / GATE_NORMALIZER
VMEM_LIMIT = 56 * 1024 * 1024
TAIL_VMEM_LIMIT = 60 * 1024 * 1024

F32 = jnp.float32
BF16 = jnp.bfloat16


def _dot(a, b):
    return jnp.dot(a, b, preferred_element_type=F32)


def _dot_nt(a, b):
    return lax.dot_general(a, b, (((1,), (1,)), ((), ())), preferred_element_type=F32)


def _dot_tn(a, b):
    return lax.dot_general(a, b, (((0,), (0,)), ((), ())), preferred_element_type=F32)


def _rms(x, w):
    ms = jnp.mean(x * x, axis=-1, keepdims=True)
    return x * lax.rsqrt(ms + EPS) * w


def _sigmoid(x):
    return 1.0 / (1.0 + jnp.exp(-x))


def _silu(x):
    return x * _sigmoid(x)


def _log_sigmoid(x):
    return jnp.minimum(x, 0.0) - jnp.log1p(jnp.exp(-jnp.abs(x)))


def _ff_chunks(f):
    return [(lo, min(lo + FFN_CHUNK, f)) for lo in range(0, f, FFN_CHUNK)]


def _const_spec(shape):
    nd = len(shape)
    return pl.BlockSpec(shape, lambda *_: (0,) * nd, pipeline_mode=pl.Buffered(1))


def _swiglu_acc(xb, wg_ref, wu_ref, wd_ref):
    acc = None
    for lo, hi in _ff_chunks(wg_ref.shape[1]):
        g = _dot(xb, wg_ref[:, lo:hi])
        u = _dot(xb, wu_ref[:, lo:hi])
        a = (_silu(g) * u).astype(BF16)
        d = _dot(a, wd_ref[lo:hi, :])
        acc = d if acc is None else acc + d
    return acc


def _ffn1_kernel(xs_ref, xp_ref, nw_ref, wg_ref, wu_ref, wd_ref, o_ref, *, n_s):
    from_sample = pl.program_id(0) < n_s
    half = xs_ref.shape[0] // 2
    for r in (slice(0, half), slice(half, 2 * half)):
        x = jnp.where(from_sample, xs_ref[r, :], xp_ref[r, :])
        xb = _rms(x, nw_ref[...]).astype(BF16)
        o_ref[r, :] = x + 0.5 * _swiglu_acc(xb, wg_ref, wu_ref, wd_ref)


def _tile_counts(ms, mp):
    tm = ROW_TILE
    assert ms % tm == 0 and mp % tm == 0
    return tm, ms // tm, mp // tm


def _ffn1(xs, xp, nw, wg, wu, wd):
    (ms, d), mp = xs.shape, xp.shape[0]
    f = wg.shape[1]
    tm, n_s, n_p = _tile_counts(ms, mp)
    return pl.pallas_call(
        functools.partial(_ffn1_kernel, n_s=n_s),
        grid=(n_s + n_p,),
        in_specs=[pl.BlockSpec((tm, d), lambda i: (jnp.minimum(i, n_s - 1), 0)),
                  pl.BlockSpec((tm, d), lambda i: (jnp.maximum(i - n_s, 0), 0)),
                  _const_spec((1, d)), _const_spec((d, f)), _const_spec((d, f)), _const_spec((f, d))],
        out_specs=pl.BlockSpec((tm, d), lambda i: (i, 0)),
        out_shape=jax.ShapeDtypeStruct((ms + mp, d), F32),
        compiler_params=pltpu.CompilerParams(dimension_semantics=("arbitrary",),
                                             vmem_limit_bytes=VMEM_LIMIT),
        name="ffn1",
    )(xs, xp, nw, wg, wu, wd)


def _tail_kernel(h_ref, mixs_ref, mixp_ref, ps_ref, pp_ref, wo_ref, nw_ref, wg_ref, wu_ref, wd_ref,
                 pn_ref, wpg_ref, wpp_ref, fn_ref, ys_ref, yp_ref, *, n_s):
    from_sample = pl.program_id(0) < n_s
    mix = jnp.where(from_sample, mixs_ref[...], mixp_ref[...])
    p = jnp.where(from_sample, ps_ref[...], pp_ref[...])
    h = h_ref[...] + _dot(mix, wo_ref[...])
    xb = _rms(h, nw_ref[...]).astype(BF16)
    h = h + 0.5 * _swiglu_acc(xb, wg_ref, wu_ref, wd_ref)
    gate = _sigmoid(_dot(_rms(h, pn_ref[...]).astype(BF16), wpg_ref[...]))
    h = h + gate * _dot(p.astype(BF16), wpp_ref[...])
    yp_ref[...] = _rms(h, fn_ref[...])

    @pl.when(from_sample)
    def _():
        ys_ref[...] = yp_ref[...]


def _tail(h, mix_s, mix_p, p_s, p_p, wo, nw, wg, wu, wd, pn, wpg, wpp, fn):
    d = h.shape[1]
    ms, mp = mix_s.shape[0], mix_p.shape[0]
    f = wg.shape[1]
    pd = p_s.shape[1]
    tm, n_s, n_p = _tile_counts(ms, mp)
    sample = lambda w: pl.BlockSpec((tm, w), lambda i: (jnp.minimum(i, n_s - 1), 0))
    prompt = lambda w: pl.BlockSpec((tm, w), lambda i: (jnp.maximum(i - n_s, 0), 0))
    ys, yp = pl.pallas_call(
        functools.partial(_tail_kernel, n_s=n_s),
        grid=(n_s + n_p,),
        in_specs=[pl.BlockSpec((tm, d), lambda i: (i, 0)), sample(d), prompt(d), sample(pd), prompt(pd),
                  _const_spec((d, d)), _const_spec((1, d)),
                  _const_spec((d, f)), _const_spec((d, f)), _const_spec((f, d)),
                  _const_spec((1, d)), _const_spec((d, d)), _const_spec((pd, d)), _const_spec((1, d))],
        out_specs=[sample(d), prompt(d)],
        out_shape=[jax.ShapeDtypeStruct((ms, d), F32), jax.ShapeDtypeStruct((mp, d), F32)],
        compiler_params=pltpu.CompilerParams(dimension_semantics=("arbitrary",),
                                             vmem_limit_bytes=TAIL_VMEM_LIMIT),
        name="tail",
    )(h, mix_s, mix_p, p_s, p_p, wo, nw, wg, wu, wd, pn, wpg, wpp, fn)
    return ys, yp


def _gla_constants(c):
    idx = np.arange(c)
    i, t = idx[:, None], idx[None, :]
    masks = []
    s = 1
    while s < c:
        same = (i // (2 * s)) == (t // (2 * s))
        masks.append(same & ((i % (2 * s)) >= s) & ((t % (2 * s)) < s))
        s *= 2
    tri = (t <= i).astype(np.float32)
    return tri, np.stack([m.astype(np.float32) for m in masks], axis=0)


def _level_exponents(g, b, b_ref, r0):
    c, n = g.shape
    row = lax.broadcasted_iota(jnp.int32, (c, n), 0)
    m4 = row & 3
    g_prev = pltpu.roll(g, 1, 0)
    g_next = pltpu.roll(g, c - 1, 0)
    out = [jnp.where((row & 1) == 1, g, 0.0),
           jnp.where(m4 == 0, g_next, jnp.where(m4 == 1, 0.0, jnp.where(m4 == 2, g, g + g_prev)))]
    s = 4
    while s < c:
        blocks = [jnp.broadcast_to(b_ref[r0 + m * 2 * s + s - 1:r0 + m * 2 * s + s, :], (2 * s, n))
                  for m in range(c // (2 * s))]
        ref = blocks[0] if len(blocks) == 1 else jnp.concatenate(blocks, axis=0)
        sign = jnp.where((lax.broadcasted_iota(jnp.int32, (c, 1), 0) & s) != 0, 1.0, -1.0)
        out.append((b - ref) * sign)
        s *= 2
    return out


def _split_hi_lo(g):
    hi = g.astype(BF16)
    lo = (g - hi.astype(F32)).astype(BF16)
    return jnp.concatenate([hi, lo], axis=1)


def _column(row, eye):
    return jnp.sum(eye * row, axis=-1, keepdims=True)


def _mixer_kernel(h_ref, mn_ref, wa_ref, wb_ref, wlr_ref, wgk_ref, bgk_ref, gn_ref, pw_ref, ps_ref,
                  tri_ref, mk_ref, eye_ref,
                  mix_ref, npool_ref, ngla_ref,
                  xb_s, ext_s, tmp_s, q_s, k_s, v_s, g_s, o_s, a_s, b_s, qin_s, kdec_s, dcy_s, m_s, att_s, gate_s,
                  *, pos0):
    tm, d = h_ref.shape
    heads = GLA_HEADS
    qk = q_s.shape[1]
    dk, dv = qk // heads, d // heads
    c = GLA_CHUNK
    n_chunks = tm // c
    levels = mk_ref.shape[0]
    l = pl.program_id(1)
    x0 = POOL_PAD + HIST
    o_q, o_k, o_v = d, d + qk, d + 2 * qk

    @pl.when(l == 0)
    def _():
        ext_s[0:x0, :] = jnp.zeros((x0, d), F32)
        ngla_ref[...] = jnp.zeros(ngla_ref.shape, F32)

    xb_s[...] = _rms(h_ref[...], mn_ref[...]).astype(BF16)
    q_s[...] = _dot(xb_s[...], wa_ref[:, o_q:o_q + qk]) * (dk ** -0.5)
    k_s[...] = _dot(xb_s[...], wa_ref[:, o_k:o_k + qk])
    lr = _dot(xb_s[...], wlr_ref[...]).astype(BF16)
    g_s[...] = _log_sigmoid(_dot(lr, wgk_ref[...]) + bgk_ref[...]) * LOG2_DECAY_SCALE

    def u_block(lo):
        ext_s[x0:x0 + tm, lo:lo + GATE_BLOCK] = _dot(xb_s[...], wa_ref[:, lo:lo + GATE_BLOCK])

    def v_block(lo):
        v_s[:, lo:lo + GATE_BLOCK] = _dot(xb_s[...], wa_ref[:, o_v + lo:o_v + lo + GATE_BLOCK]).astype(BF16)

    def gate_block(lo):
        z = _dot(xb_s[...], wb_ref[:, lo:lo + GATE_BLOCK])
        gate_s[:, lo:lo + GATE_BLOCK] = _silu(z) if lo < d else _sigmoid(z)

    early_work = ([functools.partial(u_block, lo) for lo in range(0, d, GATE_BLOCK)]
                  + [functools.partial(v_block, lo) for lo in range(0, d, GATE_BLOCK)])
    early_split = np.array_split(np.arange(len(early_work)), n_chunks)
    late_work = [functools.partial(gate_block, lo) for lo in range(0, 3 * d, GATE_BLOCK)]
    late_split = np.array_split(np.arange(len(late_work)), 2 * n_chunks)

    def head_cols(hd):
        return slice(hd * dk, (hd + 1) * dk), slice(hd * dv, (hd + 1) * dv), slice(hd * c, (hd + 1) * c)

    row = lax.broadcasted_iota(jnp.int32, (c, qk), 0)
    for ci in range(n_chunks):
        for idx in early_split[ci]:
            early_work[int(idx)]()
        r0 = ci * c
        rows = slice(r0, r0 + c)
        g = g_s[rows, :]
        sums = _dot(tri_ref[...], _split_hi_lo(g))
        b = sums[:, :qk] + sums[:, qk:]
        b_s[rows, :] = b
        b_last = b[c - 1:c, :]
        q = q_s[rows, :]
        k = k_s[rows, :]
        qin_s[rows, :] = (q * jnp.exp2(b)).astype(BF16)
        kdec_s[rows, :] = (k * jnp.exp2(b_last - b)).astype(BF16)
        dcy_s[ci * SUBLANE:ci * SUBLANE + 1, :] = jnp.exp2(b_last)
        for lv, ex in enumerate(_level_exponents(g, b, b_s, r0)):
            operand = jnp.where((row & (1 << lv)) != 0, q, k) * jnp.exp2(ex)
            m_s[lv, rows, :] = operand.astype(BF16)

    pos = (pos0 + 1 + l * tm + lax.broadcasted_iota(jnp.int32, (tm, 1), 0))
    gc = d // len(POOL_WINDOWS)

    def pool_group(grp):
        w = POOL_WINDOWS[grp]
        cols = slice(grp * gc, (grp + 1) * gc)
        shifts = [1 << i for i in range(int(np.log2(w)))]
        starts = [x0] * len(shifts)
        for i in range(len(shifts) - 2, -1, -1):
            starts[i] = (starts[i + 1] - shifts[i + 1]) // SUBLANE * SUBLANE
        for i, (sh, st) in enumerate(zip(shifts, starts)):
            if i == 0:
                val = ext_s[st:x0 + tm, cols] + ext_s[st - sh:x0 + tm - sh, cols]
            else:
                val = tmp_s[i - 1, st:x0 + tm, :] + tmp_s[i - 1, st - sh:x0 + tm - sh, :]
            if i < len(shifts) - 1:
                tmp_s[i, st:x0 + tm, :] = val
        cur = ext_s[x0:x0 + tm, cols]
        inv = 1.0 / jnp.minimum(pos, w).astype(F32)
        pooled = (val * inv - cur).astype(BF16)
        a_s[:, cols] = _dot(pooled, pw_ref[grp]) * ps_ref[:, cols]

    pool_split = np.array_split(np.arange(len(POOL_WINDOWS)), n_chunks)
    for ci in range(n_chunks):
        for idx in late_split[ci]:
            late_work[int(idx)]()
        rows = slice(ci * c, (ci + 1) * c)
        for hd in range(heads):
            hc, _, ha = head_cols(hd)
            att = None
            for lv in range(levels):
                operand = m_s[lv, rows, hc]
                t = mk_ref[lv] * _dot_nt(operand, operand)
                att = t if att is None else att + t
            att_s[rows, ha] = att.astype(BF16)
        for grp in pool_split[ci]:
            pool_group(int(grp))

    for ci in range(n_chunks):
        for idx in late_split[n_chunks + ci]:
            late_work[int(idx)]()
        rows = slice(ci * c, (ci + 1) * c)
        qk_prod = q_s[rows, :] * k_s[rows, :]
        for hd in range(heads):
            hc, hv, ha = head_cols(hd)
            state = ngla_ref[0, hd]
            vh = v_s[rows, hv]
            lhs = jnp.concatenate([qin_s[rows, hc], att_s[rows, ha]], axis=1)
            rhs = jnp.concatenate([state.astype(BF16), vh], axis=0)
            diag = jnp.sum(qk_prod[:, hc], axis=-1, keepdims=True)
            o_s[rows, hv] = _dot(lhs, rhs) + diag * vh.astype(F32)
            decay = _column(dcy_s[ci * SUBLANE:ci * SUBLANE + 1, hc], eye_ref[...])
            ngla_ref[0, hd] = state * decay + _dot_tn(kdec_s[rows, hc], vh)

    gn = gn_ref[...]
    for hd in range(heads):
        hv = slice(hd * dv, (hd + 1) * dv)
        on = _rms(o_s[:, hv], gn)
        gate = gate_s[:, hd * dv:(hd + 1) * dv]
        ga = gate_s[:, d + hd * dv:d + (hd + 1) * dv]
        gb = gate_s[:, 2 * d + hd * dv:2 * d + (hd + 1) * dv]
        mix_ref[:, hv] = (ga * a_s[:, hv] + gb * (on * gate)).astype(BF16)

    @pl.when(l == pl.num_programs(1) - 1)
    def _():
        npool_ref[0] = ext_s[x0 + tm - POOL_BUF:x0 + tm, :]

    ext_s[POOL_PAD:x0, :] = ext_s[tm + POOL_PAD:tm + x0, :]


def _mixer(h, first_row, batch, seq, mn, wa, wb, wlr, wgk, bgk, gn, pw, ps, tri, mk, eye):
    d = h.shape[1]
    m = batch * seq
    tm = min(ROW_TILE, seq)
    nl = seq // tm
    qk = bgk.shape[1]
    heads = GLA_HEADS
    dk, dv = qk // heads, d // heads
    c = GLA_CHUNK
    levels = mk.shape[0]
    gc = d // len(POOL_WINDOWS)
    n_stage = int(np.log2(max(POOL_WINDOWS))) - 1
    consts = (mn, wa, wb, wlr, wgk, bgk, gn, pw, ps, tri, mk, eye)
    kern = functools.partial(_mixer_kernel, pos0=0)
    assert first_row % tm == 0
    tile0 = first_row // tm
    return pl.pallas_call(
        kern,
        grid=(batch, nl),
        in_specs=([pl.BlockSpec((tm, d), lambda b, l: (tile0 + b * nl + l, 0))]
                  + [_const_spec(a.shape) for a in consts]),
        out_specs=[pl.BlockSpec((tm, d), lambda b, l: (b * nl + l, 0)),
                   pl.BlockSpec((1, POOL_BUF, d), lambda b, l: (b, 0, 0)),
                   pl.BlockSpec((1, heads, dk, dv), lambda b, l: (b, 0, 0, 0))],
        out_shape=[jax.ShapeDtypeStruct((m, d), BF16),
                   jax.ShapeDtypeStruct((batch, POOL_BUF, d), F32),
                   jax.ShapeDtypeStruct((batch, heads, dk, dv), F32)],
        scratch_shapes=[pltpu.VMEM((tm, d), BF16),
                        pltpu.VMEM((POOL_PAD + HIST + tm, d), F32),
                        pltpu.VMEM((n_stage, POOL_PAD + HIST + tm, gc), F32),
                        pltpu.VMEM((tm, qk), F32),
                        pltpu.VMEM((tm, qk), F32),
                        pltpu.VMEM((tm, d), BF16),
                        pltpu.VMEM((tm, qk), F32),
                        pltpu.VMEM((tm, d), F32),
                        pltpu.VMEM((tm, d), F32),
                        pltpu.VMEM((tm, qk), F32),
                        pltpu.VMEM((tm, qk), BF16),
                        pltpu.VMEM((tm, qk), BF16),
                        pltpu.VMEM((tm // c * SUBLANE, qk), F32),
                        pltpu.VMEM((levels, tm, qk), BF16),
                        pltpu.VMEM((tm, heads * c), BF16),
                        pltpu.VMEM((tm, 3 * d), F32)],
        compiler_params=pltpu.CompilerParams(dimension_semantics=("arbitrary", "arbitrary"),
                                             vmem_limit_bytes=VMEM_LIMIT),
        name="mixer_prompt",
    )(h, *consts)


def _proj_kernel(h_ref, mn_ref, wa_ref, wb_ref, wlr_ref, wgk_ref, bgk_ref, z_ref, g_ref):
    d = h_ref.shape[1]
    qk = g_ref.shape[1]
    dk = qk // GLA_HEADS
    na = wa_ref.shape[1]
    xb = _rms(h_ref[...], mn_ref[...]).astype(BF16)
    z_ref[:, 0:na] = _dot(xb, wa_ref[...])
    z_ref[:, na:] = _dot(xb, wb_ref[...])
    z_ref[:, d:d + qk] = z_ref[:, d:d + qk] * (dk ** -0.5)
    lr = _dot(xb, wlr_ref[...]).astype(BF16)
    g_ref[...] = _log_sigmoid(_dot(lr, wgk_ref[...]) + bgk_ref[...]) * LOG2_DECAY_SCALE


def _proj(h, m, mn, wa, wb, wlr, wgk, bgk):
    d = h.shape[1]
    qk = bgk.shape[1]
    wide = wa.shape[1] + wb.shape[1]
    return pl.pallas_call(
        _proj_kernel,
        grid=(1,),
        in_specs=[pl.BlockSpec((m, d), lambda i: (0, 0)),
                  _const_spec(mn.shape), _const_spec(wa.shape), _const_spec(wb.shape), _const_spec(wlr.shape),
                  _const_spec(wgk.shape), _const_spec(bgk.shape)],
        out_specs=[pl.BlockSpec((m, wide), lambda i: (0, 0)), pl.BlockSpec((m, qk), lambda i: (0, 0))],
        out_shape=[jax.ShapeDtypeStruct((m, wide), F32), jax.ShapeDtypeStruct((m, qk), F32)],
        compiler_params=pltpu.CompilerParams(dimension_semantics=("arbitrary",),
                                             vmem_limit_bytes=VMEM_LIMIT),
        name="proj_sample",
    )(h, mn, wa, wb, wlr, wgk, bgk)


def _sample_mixer_kernel(z_ref, g_ref, sp_ref, sg_ref, gn_ref, pw_ref, ps_ref, eye_ref,
                         mix_ref, npool_ref, ngla_ref,
                         qin_s, kdec_s, v_s, o_s, pl_s, ost_s, *, pos0):
    n_tok, bb, _ = z_ref.shape
    d = o_s.shape[1]
    heads = GLA_HEADS
    qk = g_ref.shape[2]
    dk, dv = qk // heads, d // heads
    o_u, o_q, o_k, o_v = 0, d, d + qk, d + 2 * qk
    o_g = o_v + d
    o_ga, o_gb = o_g + d, o_g + 2 * d
    gc = d // len(POOL_WINDOWS)
    plane = lambda t: slice(t * bb, (t + 1) * bb)

    @pl.when(pl.program_id(0) == 0)
    def _():
        pad = (SAMPLE_PAD - n_tok) * bb
        qin_s[:, (n_tok + 1) * bb:, :] = jnp.zeros((qk // LANE, pad - bb, LANE), F32)
        kdec_s[:, n_tok * bb:, :] = jnp.zeros((qk // LANE, pad, LANE), F32)
        v_s[:, n_tok * bb:, :] = jnp.zeros((d // LANE, pad, LANE), F32)

    b = []
    for t in range(n_tok):
        b.append(g_ref[t] if t == 0 else b[-1] + g_ref[t])
    b_last = b[-1]
    q = [z_ref[t, :, o_q:o_q + qk] for t in range(n_tok)]
    k = [z_ref[t, :, o_k:o_k + qk] for t in range(n_tok)]
    def stage(dst, t, val):
        for i in range(val.shape[1] // LANE):
            dst[i, plane(t), :] = val[:, i * LANE:(i + 1) * LANE]

    for t in range(n_tok):
        stage(qin_s, t, q[t] * jnp.exp2(b[t]))
        stage(kdec_s, t, k[t] * jnp.exp2(b_last - b[t]))
        stage(v_s, t, z_ref[t, :, o_v:o_v + d])
    stage(qin_s, n_tok, jnp.exp2(b_last))
    for t in range(n_tok):
        o_t = [None] * heads
        for j in range(t + 1):
            w_tj = q[t] * k[j] if j == t else q[t] * k[j] * jnp.exp2(b[t] - b[j])
            for hd in range(heads):
                a_tj = jnp.sum(w_tj[:, hd * dk:(hd + 1) * dk], axis=-1, keepdims=True)
                term = a_tj * z_ref[j, :, o_v + hd * dv:o_v + (hd + 1) * dv]
                o_t[hd] = term if o_t[hd] is None else o_t[hd] + term
        for hd in range(heads):
            o_s[plane(t), hd * dv:(hd + 1) * dv] = o_t[hd]

    def seq(n, carry):
        tokens = pl.ds(n, SAMPLE_PAD, stride=bb)
        own_rows = pl.ds(pl.multiple_of(n * SAMPLE_PAD, SAMPLE_PAD), SAMPLE_PAD)
        for hd in range(heads):
            state = sg_ref[n, hd]
            q_tok = qin_s[hd, tokens, :]
            o_state = _dot(q_tok.astype(BF16), state.astype(BF16))
            for i in range(dv // LANE):
                ost_s[hd * (dv // LANE) + i, own_rows, :] = o_state[:, i * LANE:(i + 1) * LANE]
            decay = _column(q_tok[n_tok:n_tok + 1, :], eye_ref[...])
            v_tok = jnp.concatenate([v_s[hd * (dv // LANE) + i, tokens, :] for i in range(dv // LANE)], axis=1)
            ngla_ref[n, hd] = state * decay + _dot_tn(kdec_s[hd, tokens, :].astype(BF16), v_tok.astype(BF16))
        return carry

    lax.fori_loop(0, bb, seq, 0, unroll=2)
    for t in range(n_tok):
        for i in range(d // LANE):
            o_s[plane(t), i * LANE:(i + 1) * LANE] += ost_s[i, pl.ds(t, bb, stride=SAMPLE_PAD), :]

    def pool_row(j):
        return z_ref[j, :, o_u:o_u + d] if j >= 0 else sp_ref[POOL_BUF + j]

    for t in range(n_tok):
        for grp, w in enumerate(POOL_WINDOWS):
            cols = slice(grp * gc, (grp + 1) * gc)
            s = pool_row(t)[:, cols]
            for dlt in range(1, w):
                s = s + pool_row(t - dlt)[:, cols]
            cnt = float(min(pos0 + 1 + t, w))
            pl_s[plane(t), cols] = s * (1.0 / cnt) - pool_row(t)[:, cols]
    for r in range(POOL_BUF):
        npool_ref[r] = pool_row(r + n_tok - POOL_BUF)

    rows = n_tok * bb
    gn = gn_ref[...]
    for hd in range(heads):
        hv = slice(hd * dv, (hd + 1) * dv)
        zcols = lambda off: z_ref[:, :, off + hd * dv:off + (hd + 1) * dv].reshape(rows, dv)
        a_out = _dot(pl_s[:, hv].astype(BF16), pw_ref[hd]) * ps_ref[:, hv]
        on = _rms(o_s[:, hv], gn)
        mix = _sigmoid(zcols(o_ga)) * a_out + _sigmoid(zcols(o_gb)) * (on * _silu(zcols(o_g)))
        mix_ref[:, :, hv] = mix.reshape(n_tok, bb, dv).astype(mix_ref.dtype)


def _sample_mixer(z, g, sp, sg, gn, pw, ps, eye):
    n_tok, nb, wide = z.shape
    qk = g.shape[2]
    d = sp.shape[2]
    heads = GLA_HEADS
    dk, dv = qk // heads, d // heads
    assert d // len(POOL_WINDOWS) == dv, "pool groups and attention heads share a column split here"
    assert n_tok < SAMPLE_PAD and dk == LANE and dv % LANE == 0
    bb = min(SAMPLE_BLOCK, nb)
    kern = functools.partial(_sample_mixer_kernel, pos0=PAST_LEN)
    blk3 = lambda a, b: pl.BlockSpec((a, bb, b), lambda i: (0, i, 0))
    state_spec = pl.BlockSpec((bb, heads, dk, dv), lambda i: (i, 0, 0, 0))
    return pl.pallas_call(
        kern,
        grid=(nb // bb,),
        in_specs=[blk3(n_tok, wide), blk3(n_tok, qk), blk3(POOL_BUF, d), state_spec,
                  _const_spec(gn.shape), _const_spec(pw.shape), _const_spec(ps.shape), _const_spec(eye.shape)],
        out_specs=[blk3(n_tok, d), blk3(POOL_BUF, d), state_spec],
        out_shape=[jax.ShapeDtypeStruct((n_tok, nb, d), BF16),
                   jax.ShapeDtypeStruct((POOL_BUF, nb, d), F32),
                   jax.ShapeDtypeStruct((nb, heads, dk, dv), F32)],
        scratch_shapes=[pltpu.VMEM((qk // LANE, SAMPLE_PAD * bb, LANE), F32),
                        pltpu.VMEM((qk // LANE, SAMPLE_PAD * bb, LANE), F32),
                        pltpu.VMEM((d // LANE, SAMPLE_PAD * bb, LANE), F32),
                        pltpu.VMEM((n_tok * bb, d), F32),
                        pltpu.VMEM((n_tok * bb, d), F32),
                        pltpu.VMEM((d // LANE, SAMPLE_PAD * bb, LANE), F32)],
        compiler_params=pltpu.CompilerParams(dimension_semantics=("arbitrary",),
                                             vmem_limit_bytes=VMEM_LIMIT),
        name="mixer_sample",
    )(z, g, sp, sg, gn, pw, ps, eye)


def _pack_kernel(a_ref, b_ref, lr_ref, wa_ref, wb_ref, wlr_ref, *, rank):
    wa_ref[...] = a_ref[...].T.astype(BF16)
    wb_ref[...] = b_ref[...].T.astype(BF16)
    lane = lax.broadcasted_iota(jnp.int32, wlr_ref.shape, 1)
    wlr_ref[...] = jnp.where(lane < rank, lr_ref[...].T, 0.0).astype(BF16)


def _pack_w_in(wt, split, rank):
    n, d = wt.shape
    rest = n - split - rank
    blk = 2 * GATE_BLOCK
    assert split % blk == 0 and rest % blk == 0 and split // blk == rest // blk and split % LANE == 0
    return pl.pallas_call(
        functools.partial(_pack_kernel, rank=rank),
        grid=(split // blk,),
        in_specs=[pl.BlockSpec((blk, d), lambda i: (i, 0)),
                  pl.BlockSpec((pl.Element(blk), pl.Element(d)), lambda i: (pl.multiple_of(split + rank + i * blk, SUBLANE), 0)),
                  pl.BlockSpec((LANE, d), lambda i: (split // LANE, 0))],
        out_specs=[pl.BlockSpec((d, blk), lambda i: (0, i)), pl.BlockSpec((d, blk), lambda i: (0, i)),
                   pl.BlockSpec((d, LANE), lambda i: (0, 0))],
        out_shape=[jax.ShapeDtypeStruct((d, split), BF16), jax.ShapeDtypeStruct((d, rest), BF16),
                   jax.ShapeDtypeStruct((d, LANE), BF16)],
        compiler_params=pltpu.CompilerParams(dimension_semantics=("arbitrary",)),
        name="pack_w_in",
    )(wt, wt, wt)


def _layer_weights(i, ffn1_norm, ffn1_w_gate, ffn1_w_up, ffn1_w_down, mix_norm, w_in, w_gk_up, b_gk,
                   gla_norm, pool_w, pool_scale, w_out, ffn2_norm, ffn2_w_gate, ffn2_w_up, ffn2_w_down,
                   ple_norm, w_ple_gate, w_ple_proj):
    d = w_in.shape[1]
    qk = w_gk_up.shape[2]
    rank = w_gk_up.shape[1]
    o_lr = 2 * d + 2 * qk
    wa, wb, wlr = _pack_w_in(jnp.swapaxes(w_in[i], 0, 1), o_lr, rank)
    wgk = jnp.concatenate([w_gk_up[i], jnp.zeros((LANE - rank, qk), F32)], axis=0).astype(BF16)
    row = lambda a: a[i].reshape(1, -1)
    return dict(
        ffn1=(row(ffn1_norm), ffn1_w_gate[i], ffn1_w_up[i], ffn1_w_down[i]),
        proj=(row(mix_norm), wa, wb, wlr, wgk, row(b_gk)),
        mix=(row(gla_norm), pool_w[i].astype(BF16), row(pool_scale)),
        tail=(w_out[i].astype(BF16), row(ffn2_norm), ffn2_w_gate[i], ffn2_w_up[i],
              ffn2_w_down[i].astype(BF16), row(ple_norm), w_ple_gate[i].astype(BF16), w_ple_proj[i].astype(BF16)),
    )


def kernel(x_prompt, x_sample, p_prompt, p_sample, state_pool, state_gla, ffn1_norm, ffn1_w_gate, ffn1_w_up, ffn1_w_down, mix_norm, w_in, w_gk_up, b_gk, gla_norm, pool_w, pool_scale, w_out, ffn2_norm, ffn2_w_gate, ffn2_w_up, ffn2_w_down, ple_norm, w_ple_gate, w_ple_proj, final_norm):
    depth = w_in.shape[0]
    batch, seq, d = x_prompt.shape
    nb, n_tok, _ = x_sample.shape
    assert depth == 1, "the final norm is fused into the tail kernel of the only layer"
    assert n_tok <= 4 and POOL_BUF <= HIST and seq % GLA_CHUNK == 0
    tri, mk = _gla_constants(GLA_CHUNK)
    tri = jnp.asarray(tri, BF16)
    mk = jnp.asarray(mk, F32)
    eye = jnp.eye(w_gk_up.shape[2] // GLA_HEADS, dtype=F32)
    fn = final_norm.reshape(1, -1)

    lw = _layer_weights(0, ffn1_norm, ffn1_w_gate, ffn1_w_up, ffn1_w_down, mix_norm, w_in, w_gk_up, b_gk,
                        gla_norm, pool_w, pool_scale, w_out, ffn2_norm, ffn2_w_gate, ffn2_w_up, ffn2_w_down,
                        ple_norm, w_ple_gate, w_ple_proj)
    tok_major = lambda a: jnp.swapaxes(a, 0, 1)
    ms, mp = n_tok * nb, batch * seq
    h1 = _ffn1(tok_major(x_sample).reshape(ms, d), x_prompt.reshape(mp, d), *lw["ffn1"])
    mix_p, pool_p, gla_p = _mixer(h1, ms, batch, seq, *lw["proj"], *lw["mix"], tri, mk, eye)
    z, g = _proj(h1, ms, *lw["proj"])
    mix_s, pool_s, gla_s = _sample_mixer(z.reshape(n_tok, nb, -1), g.reshape(n_tok, nb, -1),
                                         tok_major(state_pool[0]), state_gla[0], *lw["mix"], eye)
    y_sample, y_prompt = _tail(h1, mix_s.reshape(ms, d), mix_p, tok_major(p_sample[0]).reshape(ms, -1),
                               p_prompt[0].reshape(mp, -1), *lw["tail"], fn)
    return (y_prompt.reshape(batch, seq, d), tok_major(y_sample.reshape(n_tok, nb, d)),
            pool_p[None], gla_p[None], tok_major(pool_s)[None], gla_s[None])
```

```python
import functools

import numpy as np
import jax
import jax.numpy as jnp
from jax import lax
from jax.experimental import pallas as pl
from jax.experimental.pallas import tpu as pltpu

POOL_WINDOWS = (2, 4, 8, 16)
POOL_BUF = max(POOL_WINDOWS) - 1
GLA_HEADS = 4
GATE_NORMALIZER = 16.0
EPS = 1e-6
PAST_LEN = 16384

ROW_TILE = 512
GLA_CHUNK = 128
HIST = 16
POOL_PAD = 16
LANE = 128
SUBLANE = 8
SAMPLE_PAD = SUBLANE
SAMPLE_BLOCK = 16
GATE_BLOCK = 256
FFN_CHUNK = 6 * GATE_BLOCK
PROJ_BLOCK = 4 * GATE_BLOCK
LOG2_DECAY_SCALE = float(np.log2(np.e)) / GATE_NORMALIZER
VMEM_LIMIT = 56 * 1024 * 1024
TAIL_VMEM_LIMIT = 60 * 1024 * 1024

F32 = jnp.float32
BF16 = jnp.bfloat16


def _dot(a, b):
    return jnp.dot(a, b, preferred_element_type=F32)


def _dot_nt(a, b):
    return lax.dot_general(a, b, (((1,), (1,)), ((), ())), preferred_element_type=F32)


def _dot_tn(a, b):
    return lax.dot_general(a, b, (((0,), (0,)), ((), ())), preferred_element_type=F32)


def _rms(x, w):
    ms = jnp.mean(x * x, axis=-1, keepdims=True)
    return x * lax.rsqrt(ms + EPS) * w


def _sigmoid(x):
    return 1.0 / (1.0 + jnp.exp(-x))


def _silu(x):
    return x * _sigmoid(x)


def _log_sigmoid(x):
    return jnp.minimum(x, 0.0) - jnp.log1p(jnp.exp(-jnp.abs(x)))


def _ff_chunks(f):
    return [(lo, min(lo + FFN_CHUNK, f)) for lo in range(0, f, FFN_CHUNK)]


def _const_spec(shape):
    nd = len(shape)
    return pl.BlockSpec(shape, lambda *_: (0,) * nd, pipeline_mode=pl.Buffered(1))


def _swiglu_acc(xb, wg_ref, wu_ref, wd_ref):
    acc = None
    for lo, hi in _ff_chunks(wg_ref.shape[1]):
        g = _dot(xb, wg_ref[:, lo:hi])
        u = _dot(xb, wu_ref[:, lo:hi])
        a = (_silu(g) * u).astype(BF16)
        d = _dot(a, wd_ref[lo:hi, :])
        acc = d if acc is None else acc + d
    return acc


def _ffn1_kernel(xs_ref, xp_ref, nw_ref, wg_ref, wu_ref, wd_ref, o_ref, *, n_s):
    from_sample = pl.program_id(0) < n_s
    half = xs_ref.shape[0] // 2
    for r in (slice(0, half), slice(half, 2 * half)):
        x = jnp.where(from_sample, xs_ref[r, :], xp_ref[r, :])
        xb = _rms(x, nw_ref[...]).astype(BF16)
        o_ref[r, :] = x + 0.5 * _swiglu_acc(xb, wg_ref, wu_ref, wd_ref)


def _tile_counts(ms, mp):
    tm = ROW_TILE
    assert ms % tm == 0 and mp % tm == 0
    return tm, ms // tm, mp // tm


def _ffn1(xs, xp, nw, wg, wu, wd):
    (ms, d), mp = xs.shape, xp.shape[0]
    f = wg.shape[1]
    tm, n_s, n_p = _tile_counts(ms, mp)
    return pl.pallas_call(
        functools.partial(_ffn1_kernel, n_s=n_s),
        grid=(n_s + n_p,),
        in_specs=[pl.BlockSpec((tm, d), lambda i: (jnp.minimum(i, n_s - 1), 0)),
                  pl.BlockSpec((tm, d), lambda i: (jnp.maximum(i - n_s, 0), 0)),
                  _const_spec((1, d)), _const_spec((d, f)), _const_spec((d, f)), _const_spec((f, d))],
        out_specs=pl.BlockSpec((tm, d), lambda i: (i, 0)),
        out_shape=jax.ShapeDtypeStruct((ms + mp, d), F32),
        compiler_params=pltpu.CompilerParams(dimension_semantics=("arbitrary",),
                                             vmem_limit_bytes=VMEM_LIMIT),
        name="ffn1",
    )(xs, xp, nw, wg, wu, wd)


def _tail_kernel(h_ref, mixs_ref, mixp_ref, ps_ref, pp_ref, wo_ref, nw_ref, wg_ref, wu_ref, wd_ref,
                 pn_ref, wpg_ref, wpp_ref, fn_ref, ys_ref, yp_ref, *, n_s):
    from_sample = pl.program_id(0) < n_s
    mix = jnp.where(from_sample, mixs_ref[...], mixp_ref[...])
    p = jnp.where(from_sample, ps_ref[...], pp_ref[...])
    h = h_ref[...] + _dot(mix, wo_ref[...])
    xb = _rms(h, nw_ref[...]).astype(BF16)
    h = h + 0.5 * _swiglu_acc(xb, wg_ref, wu_ref, wd_ref)
    gate = _sigmoid(_dot(_rms(h, pn_ref[...]).astype(BF16), wpg_ref[...]))
    h = h + gate * _dot(p.astype(BF16), wpp_ref[...])
    yp_ref[...] = _rms(h, fn_ref[...])

    @pl.when(from_sample)
    def _():
        ys_ref[...] = yp_ref[...]


def _tail(h, mix_s, mix_p, p_s, p_p, wo, nw, wg, wu, wd, pn, wpg, wpp, fn):
    d = h.shape[1]
    ms, mp = mix_s.shape[0], mix_p.shape[0]
    f = wg.shape[1]
    pd = p_s.shape[1]
    tm, n_s, n_p = _tile_counts(ms, mp)
    sample = lambda w: pl.BlockSpec((tm, w), lambda i: (jnp.minimum(i, n_s - 1), 0))
    prompt = lambda w: pl.BlockSpec((tm, w), lambda i: (jnp.maximum(i - n_s, 0), 0))
    ys, yp = pl.pallas_call(
        functools.partial(_tail_kernel, n_s=n_s),
        grid=(n_s + n_p,),
        in_specs=[pl.BlockSpec((tm, d), lambda i: (i, 0)), sample(d), prompt(d), sample(pd), prompt(pd),
                  _const_spec((d, d)), _const_spec((1, d)),
                  _const_spec((d, f)), _const_spec((d, f)), _const_spec((f, d)),
                  _const_spec((1, d)), _const_spec((d, d)), _const_spec((pd, d)), _const_spec((1, d))],
        out_specs=[sample(d), prompt(d)],
        out_shape=[jax.ShapeDtypeStruct((ms, d), F32), jax.ShapeDtypeStruct((mp, d), F32)],
        compiler_params=pltpu.CompilerParams(dimension_semantics=("arbitrary",),
                                             vmem_limit_bytes=TAIL_VMEM_LIMIT),
        name="tail",
    )(h, mix_s, mix_p, p_s, p_p, wo, nw, wg, wu, wd, pn, wpg, wpp, fn)
    return ys, yp


def _gla_constants(c):
    idx = np.arange(c)
    i, t = idx[:, None], idx[None, :]
    masks = []
    s = 1
    while s < c:
        same = (i // (2 * s)) == (t // (2 * s))
        masks.append(same & ((i % (2 * s)) >= s) & ((t % (2 * s)) < s))
        s *= 2
    tri = (t <= i).astype(np.float32)
    return tri, np.stack([m.astype(np.float32) for m in masks], axis=0)


def _level_exponents(g, b, b_ref, r0):
    c, n = g.shape
    row = lax.broadcasted_iota(jnp.int32, (c, n), 0)
    m4 = row & 3
    g_prev = pltpu.roll(g, 1, 0)
    g_next = pltpu.roll(g, c - 1, 0)
    out = [jnp.where((row & 1) == 1, g, 0.0),
           jnp.where(m4 == 0, g_next, jnp.where(m4 == 1, 0.0, jnp.where(m4 == 2, g, g + g_prev)))]
    s = 4
    while s < c:
        blocks = [jnp.broadcast_to(b_ref[r0 + m * 2 * s + s - 1:r0 + m * 2 * s + s, :], (2 * s, n))
                  for m in range(c // (2 * s))]
        ref = blocks[0] if len(blocks) == 1 else jnp.concatenate(blocks, axis=0)
        sign = jnp.where((lax.broadcasted_iota(jnp.int32, (c, 1), 0) & s) != 0, 1.0, -1.0)
        out.append((b - ref) * sign)
        s *= 2
    return out


def _split_hi_lo(g):
    hi = g.astype(BF16)
    lo = (g - hi.astype(F32)).astype(BF16)
    return jnp.concatenate([hi, lo], axis=1)


def _column(row, eye):
    return jnp.sum(eye * row, axis=-1, keepdims=True)


def _mixer_kernel(h_ref, mn_ref, wa_ref, wb_ref, wlr_ref, wgk_ref, bgk_ref, gn_ref, pw_ref, ps_ref,
                  tri_ref, mk_ref, eye_ref,
                  mix_ref, npool_ref, ngla_ref,
                  xb_s, ext_s, tmp_s, q_s, k_s, v_s, g_s, o_s, a_s, b_s, qin_s, kdec_s, dcy_s, m_s, att_s, gate_s,
                  *, pos0):
    tm, d = h_ref.shape
    heads = GLA_HEADS
    qk = q_s.shape[1]
    dk, dv = qk // heads, d // heads
    c = GLA_CHUNK
    n_chunks = tm // c
    levels = mk_ref.shape[0]
    l = pl.program_id(1)
    x0 = POOL_PAD + HIST
    o_q, o_k, o_v = d, d + qk, d + 2 * qk

    @pl.when(l == 0)
    def _():
        ext_s[0:x0, :] = jnp.zeros((x0, d), F32)
        ngla_ref[...] = jnp.zeros(ngla_ref.shape, F32)

    xb_s[...] = _rms(h_ref[...], mn_ref[...]).astype(BF16)
    q_s[...] = _dot(xb_s[...], wa_ref[:, o_q:o_q + qk]) * (dk ** -0.5)
    k_s[...] = _dot(xb_s[...], wa_ref[:, o_k:o_k + qk])
    lr = _dot(xb_s[...], wlr_ref[...]).astype(BF16)
    g_s[...] = _log_sigmoid(_dot(lr, wgk_ref[...]) + bgk_ref[...]) * LOG2_DECAY_SCALE

    def u_block(lo):
        ext_s[x0:x0 + tm, lo:lo + GATE_BLOCK] = _dot(xb_s[...], wa_ref[:, lo:lo + GATE_BLOCK])

    def v_block(lo):
        v_s[:, lo:lo + GATE_BLOCK] = _dot(xb_s[...], wa_ref[:, o_v + lo:o_v + lo + GATE_BLOCK]).astype(BF16)

    def gate_block(lo):
        z = _dot(xb_s[...], wb_ref[:, lo:lo + GATE_BLOCK])
        gate_s[:, lo:lo + GATE_BLOCK] = _silu(z) if lo < d else _sigmoid(z)

    early_work = ([functools.partial(u_block, lo) for lo in range(0, d, GATE_BLOCK)]
                  + [functools.partial(v_block, lo) for lo in range(0, d, GATE_BLOCK)])
    early_split = np.array_split(np.arange(len(early_work)), n_chunks)
    late_work = [functools.partial(gate_block, lo) for lo in range(0, 3 * d, GATE_BLOCK)]
    late_split = np.array_split(np.arange(len(late_work)), 2 * n_chunks)

    def head_cols(hd):
        return slice(hd * dk, (hd + 1) * dk), slice(hd * dv, (hd + 1) * dv), slice(hd * c, (hd + 1) * c)

    row = lax.broadcasted_iota(jnp.int32, (c, qk), 0)
    for ci in range(n_chunks):
        for idx in early_split[ci]:
            early_work[int(idx)]()
        r0 = ci * c
        rows = slice(r0, r0 + c)
        g = g_s[rows, :]
        sums = _dot(tri_ref[...], _split_hi_lo(g))
        b = sums[:, :qk] + sums[:, qk:]
        b_s[rows, :] = b
        b_last = b[c - 1:c, :]
        q = q_s[rows, :]
        k = k_s[rows, :]
        qin_s[rows, :] = (q * jnp.exp2(b)).astype(BF16)
        kdec_s[rows, :] = (k * jnp.exp2(b_last - b)).astype(BF16)
        dcy_s[ci * SUBLANE:ci * SUBLANE + 1, :] = jnp.exp2(b_last)
        for lv, ex in enumerate(_level_exponents(g, b, b_s, r0)):
            operand = jnp.where((row & (1 << lv)) != 0, q, k) * jnp.exp2(ex)
            m_s[lv, rows, :] = operand.astype(BF16)

    pos = (pos0 + 1 + l * tm + lax.broadcasted_iota(jnp.int32, (tm, 1), 0))
    gc = d // len(POOL_WINDOWS)

    def pool_group(grp):
        w = POOL_WINDOWS[grp]
        cols = slice(grp * gc, (grp + 1) * gc)
        shifts = [1 << i for i in range(int(np.log2(w)))]
        starts = [x0] * len(shifts)
        for i in range(len(shifts) - 2, -1, -1):
            starts[i] = (starts[i + 1] - shifts[i + 1]) // SUBLANE * SUBLANE
        for i, (sh, st) in enumerate(zip(shifts, starts)):
            if i == 0:
                val = ext_s[st:x0 + tm, cols] + ext_s[st - sh:x0 + tm - sh, cols]
            else:
                val = tmp_s[i - 1, st:x0 + tm, :] + tmp_s[i - 1, st - sh:x0 + tm - sh, :]
            if i < len(shifts) - 1:
                tmp_s[i, st:x0 + tm, :] = val
        cur = ext_s[x0:x0 + tm, cols]
        inv = 1.0 / jnp.minimum(pos, w).astype(F32)
        pooled = (val * inv - cur).astype(BF16)
        a_s[:, cols] = _dot(pooled, pw_ref[grp]) * ps_ref[:, cols]

    pool_split = np.array_split(np.arange(len(POOL_WINDOWS)), n_chunks)
    for ci in range(n_chunks):
        for idx in late_split[ci]:
            late_work[int(idx)]()
        rows = slice(ci * c, (ci + 1) * c)
        for hd in range(heads):
            hc, _, ha = head_cols(hd)
            att = None
            for lv in range(levels):
                operand = m_s[lv, rows, hc]
                t = mk_ref[lv] * _dot_nt(operand, operand)
                att = t if att is None else att + t
            att_s[rows, ha] = att.astype(BF16)
        for grp in pool_split[ci]:
            pool_group(int(grp))

    for ci in range(n_chunks):
        for idx in late_split[n_chunks + ci]:
            late_work[int(idx)]()
        rows = slice(ci * c, (ci + 1) * c)
        qk_prod = q_s[rows, :] * k_s[rows, :]
        for hd in range(heads):
            hc, hv, ha = head_cols(hd)
            state = ngla_ref[0, hd]
            vh = v_s[rows, hv]
            lhs = jnp.concatenate([qin_s[rows, hc], att_s[rows, ha]], axis=1)
            rhs = jnp.concatenate([state.astype(BF16), vh], axis=0)
            diag = jnp.sum(qk_prod[:, hc], axis=-1, keepdims=True)
            o_s[rows, hv] = _dot(lhs, rhs) + diag * vh.astype(F32)
            decay = _column(dcy_s[ci * SUBLANE:ci * SUBLANE + 1, hc], eye_ref[...])
            ngla_ref[0, hd] = state * decay + _dot_tn(kdec_s[rows, hc], vh)

    gn = gn_ref[...]
    for hd in range(heads):
        hv = slice(hd * dv, (hd + 1) * dv)
        on = _rms(o_s[:, hv], gn)
        gate = gate_s[:, hd * dv:(hd + 1) * dv]
        ga = gate_s[:, d + hd * dv:d + (hd + 1) * dv]
        gb = gate_s[:, 2 * d + hd * dv:2 * d + (hd + 1) * dv]
        mix_ref[:, hv] = (ga * a_s[:, hv] + gb * (on * gate)).astype(BF16)

    @pl.when(l == pl.num_programs(1) - 1)
    def _():
        npool_ref[0] = ext_s[x0 + tm - POOL_BUF:x0 + tm, :]

    ext_s[POOL_PAD:x0, :] = ext_s[tm + POOL_PAD:tm + x0, :]


def _mixer(h, first_row, batch, seq, mn, wa, wb, wlr, wgk, bgk, gn, pw, ps, tri, mk, eye):
    d = h.shape[1]
    m = batch * seq
    tm = min(ROW_TILE, seq)
    nl = seq // tm
    qk = bgk.shape[1]
    heads = GLA_HEADS
    dk, dv = qk // heads, d // heads
    c = GLA_CHUNK
    levels = mk.shape[0]
    gc = d // len(POOL_WINDOWS)
    n_stage = int(np.log2(max(POOL_WINDOWS))) - 1
    consts = (mn, wa, wb, wlr, wgk, bgk, gn, pw, ps, tri, mk, eye)
    kern = functools.partial(_mixer_kernel, pos0=0)
    assert first_row % tm == 0
    tile0 = first_row // tm
    return pl.pallas_call(
        kern,
        grid=(batch, nl),
        in_specs=([pl.BlockSpec((tm, d), lambda b, l: (tile0 + b * nl + l, 0))]
                  + [_const_spec(a.shape) for a in consts]),
        out_specs=[pl.BlockSpec((tm, d), lambda b, l: (b * nl + l, 0)),
                   pl.BlockSpec((1, POOL_BUF, d), lambda b, l: (b, 0, 0)),
                   pl.BlockSpec((1, heads, dk, dv), lambda b, l: (b, 0, 0, 0))],
        out_shape=[jax.ShapeDtypeStruct((m, d), BF16),
                   jax.ShapeDtypeStruct((batch, POOL_BUF, d), F32),
                   jax.ShapeDtypeStruct((batch, heads, dk, dv), F32)],
        scratch_shapes=[pltpu.VMEM((tm, d), BF16),
                        pltpu.VMEM((POOL_PAD + HIST + tm, d), F32),
                        pltpu.VMEM((n_stage, POOL_PAD + HIST + tm, gc), F32),
                        pltpu.VMEM((tm, qk), F32),
                        pltpu.VMEM((tm, qk), F32),
                        pltpu.VMEM((tm, d), BF16),
                        pltpu.VMEM((tm, qk), F32),
                        pltpu.VMEM((tm, d), F32),
                        pltpu.VMEM((tm, d), F32),
                        pltpu.VMEM((tm, qk), F32),
                        pltpu.VMEM((tm, qk), BF16),
                        pltpu.VMEM((tm, qk), BF16),
                        pltpu.VMEM((tm // c * SUBLANE, qk), F32),
                        pltpu.VMEM((levels, tm, qk), BF16),
                        pltpu.VMEM((tm, heads * c), BF16),
                        pltpu.VMEM((tm, 3 * d), F32)],
        compiler_params=pltpu.CompilerParams(dimension_semantics=("arbitrary", "arbitrary"),
                                             vmem_limit_bytes=VMEM_LIMIT),
        name="mixer_prompt",
    )(h, *consts)


def _proj_kernel(h_ref, mn_ref, wa_ref, wb_ref, wlr_ref, wgk_ref, bgk_ref, z_ref, g_ref, xb_s, *, n_a):
    j = pl.program_id(0)

    @pl.when(j == 0)
    def _():
        xb_s[...] = _rms(h_ref[...], mn_ref[...]).astype(BF16)
        lr = _dot(xb_s[...], wlr_ref[...]).astype(BF16)
        g_ref[...] = _log_sigmoid(_dot(lr, wgk_ref[...]) + bgk_ref[...]) * LOG2_DECAY_SCALE

    @pl.when(j < n_a)
    def _():
        z_ref[...] = _dot(xb_s[...], wa_ref[...])

    @pl.when(j >= n_a)
    def _():
        z_ref[...] = _dot(xb_s[...], wb_ref[...])


def _proj(h, m, mn, wa, wb, wlr, wgk, bgk):
    d = h.shape[1]
    qk = bgk.shape[1]
    blk = PROJ_BLOCK
    assert wa.shape[1] % blk == 0 and wb.shape[1] % blk == 0
    n_a, n_b = wa.shape[1] // blk, wb.shape[1] // blk
    return pl.pallas_call(
        functools.partial(_proj_kernel, n_a=n_a),
        grid=(n_a + n_b,),
        in_specs=[pl.BlockSpec((m, d), lambda j: (0, 0)), _const_spec(mn.shape),
                  pl.BlockSpec((d, blk), lambda j: (0, jnp.minimum(j, n_a - 1))),
                  pl.BlockSpec((d, blk), lambda j: (0, jnp.maximum(j - n_a, 0))),
                  _const_spec(wlr.shape), _const_spec(wgk.shape), _const_spec(bgk.shape)],
        out_specs=[pl.BlockSpec((m, blk), lambda j: (0, j)), pl.BlockSpec((m, qk), lambda j: (0, 0))],
        out_shape=[jax.ShapeDtypeStruct((m, (n_a + n_b) * blk), F32), jax.ShapeDtypeStruct((m, qk), F32)],
        scratch_shapes=[pltpu.VMEM((m, d), BF16)],
        compiler_params=pltpu.CompilerParams(dimension_semantics=("arbitrary",),
                                             vmem_limit_bytes=VMEM_LIMIT),
        name="proj_sample",
    )(h, mn, wa, wb, wlr, wgk, bgk)


def _sample_mixer_kernel(z_ref, g_ref, sp_ref, sg_ref, gn_ref, pw_ref, ps_ref, eye_ref,
                         mix_ref, npool_ref, ngla_ref,
                         qin_s, kdec_s, v_s, o_s, pl_s, ost_s, *, pos0):
    n_tok, bb, _ = z_ref.shape
    d = o_s.shape[1]
    heads = GLA_HEADS
    qk = g_ref.shape[2]
    dk, dv = qk // heads, d // heads
    o_u, o_q, o_k, o_v = 0, d, d + qk, d + 2 * qk
    o_g = o_v + d
    o_ga, o_gb = o_g + d, o_g + 2 * d
    gc = d // len(POOL_WINDOWS)
    plane = lambda t: slice(t * bb, (t + 1) * bb)

    @pl.when(pl.program_id(0) == 0)
    def _():
        pad = (SAMPLE_PAD - n_tok) * bb
        qin_s[:, (n_tok + 1) * bb:, :] = jnp.zeros((qk // LANE, pad - bb, LANE), F32)
        kdec_s[:, n_tok * bb:, :] = jnp.zeros((qk // LANE, pad, LANE), F32)
        v_s[:, n_tok * bb:, :] = jnp.zeros((d // LANE, pad, LANE), F32)

    b = []
    for t in range(n_tok):
        b.append(g_ref[t] if t == 0 else b[-1] + g_ref[t])
    b_last = b[-1]
    q = [z_ref[t, :, o_q:o_q + qk] * (dk ** -0.5) for t in range(n_tok)]
    k = [z_ref[t, :, o_k:o_k + qk] for t in range(n_tok)]
    def stage(dst, t, val):
        for i in range(val.shape[1] // LANE):
            dst[i, plane(t), :] = val[:, i * LANE:(i + 1) * LANE]

    for t in range(n_tok):
        stage(qin_s, t, q[t] * jnp.exp2(b[t]))
        stage(kdec_s, t, k[t] * jnp.exp2(b_last - b[t]))
        stage(v_s, t, z_ref[t, :, o_v:o_v + d])
    stage(qin_s, n_tok, jnp.exp2(b_last))
    for t in range(n_tok):
        o_t = [None] * heads
        for j in range(t + 1):
            w_tj = q[t] * k[j] if j == t else q[t] * k[j] * jnp.exp2(b[t] - b[j])
            for hd in range(heads):
                a_tj = jnp.sum(w_tj[:, hd * dk:(hd + 1) * dk], axis=-1, keepdims=True)
                term = a_tj * z_ref[j, :, o_v + hd * dv:o_v + (hd + 1) * dv]
                o_t[hd] = term if o_t[hd] is None else o_t[hd] + term
        for hd in range(heads):
            o_s[plane(t), hd * dv:(hd + 1) * dv] = o_t[hd]

    def seq(n, carry):
        tokens = pl.ds(n, SAMPLE_PAD, stride=bb)
        own_rows = pl.ds(pl.multiple_of(n * SAMPLE_PAD, SAMPLE_PAD), SAMPLE_PAD)
        for hd in range(heads):
            state = sg_ref[n, hd]
            q_tok = qin_s[hd, tokens, :]
            o_state = _dot(q_tok.astype(BF16), state.astype(BF16))
            for i in range(dv // LANE):
                ost_s[hd * (dv // LANE) + i, own_rows, :] = o_state[:, i * LANE:(i + 1) * LANE]
            decay = _column(q_tok[n_tok:n_tok + 1, :], eye_ref[...])
            v_tok = jnp.concatenate([v_s[hd * (dv // LANE) + i, tokens, :] for i in range(dv // LANE)], axis=1)
            ngla_ref[n, hd] = state * decay + _dot_tn(kdec_s[hd, tokens, :].astype(BF16), v_tok.astype(BF16))
        return carry

    lax.fori_loop(0, bb, seq, 0, unroll=4)
    for t in range(n_tok):
        for i in range(d // LANE):
            o_s[plane(t), i * LANE:(i + 1) * LANE] += ost_s[i, pl.ds(t, bb, stride=SAMPLE_PAD), :]

    def pool_row(j):
        return z_ref[j, :, o_u:o_u + d] if j >= 0 else sp_ref[POOL_BUF + j]

    for t in range(n_tok):
        for grp, w in enumerate(POOL_WINDOWS):
            cols = slice(grp * gc, (grp + 1) * gc)
            s = pool_row(t)[:, cols]
            for dlt in range(1, w):
                s = s + pool_row(t - dlt)[:, cols]
            cnt = float(min(pos0 + 1 + t, w))
            pl_s[plane(t), cols] = s * (1.0 / cnt) - pool_row(t)[:, cols]
    for r in range(POOL_BUF):
        npool_ref[r] = pool_row(r + n_tok - POOL_BUF)

    rows = n_tok * bb
    gn = gn_ref[...]
    for hd in range(heads):
        hv = slice(hd * dv, (hd + 1) * dv)
        zcols = lambda off: z_ref[:, :, off + hd * dv:off + (hd + 1) * dv].reshape(rows, dv)
        a_out = _dot(pl_s[:, hv].astype(BF16), pw_ref[hd]) * ps_ref[:, hv]
        on = _rms(o_s[:, hv], gn)
        mix = _sigmoid(zcols(o_ga)) * a_out + _sigmoid(zcols(o_gb)) * (on * _silu(zcols(o_g)))
        mix_ref[:, :, hv] = mix.reshape(n_tok, bb, dv).astype(mix_ref.dtype)


def _sample_mixer(z, g, sp, sg, gn, pw, ps, eye):
    n_tok, nb, wide = z.shape
    qk = g.shape[2]
    d = sp.shape[2]
    heads = GLA_HEADS
    dk, dv = qk // heads, d // heads
    assert d // len(POOL_WINDOWS) == dv, "pool groups and attention heads share a column split here"
    assert n_tok < SAMPLE_PAD and dk == LANE and dv % LANE == 0
    bb = min(SAMPLE_BLOCK, nb)
    kern = functools.partial(_sample_mixer_kernel, pos0=PAST_LEN)
    blk3 = lambda a, b: pl.BlockSpec((a, bb, b), lambda i: (0, i, 0))
    state_spec = pl.BlockSpec((bb, heads, dk, dv), lambda i: (i, 0, 0, 0))
    return pl.pallas_call(
        kern,
        grid=(nb // bb,),
        in_specs=[blk3(n_tok, wide), blk3(n_tok, qk), blk3(POOL_BUF, d), state_spec,
                  _const_spec(gn.shape), _const_spec(pw.shape), _const_spec(ps.shape), _const_spec(eye.shape)],
        out_specs=[blk3(n_tok, d), blk3(POOL_BUF, d), state_spec],
        out_shape=[jax.ShapeDtypeStruct((n_tok, nb, d), BF16),
                   jax.ShapeDtypeStruct((POOL_BUF, nb, d), F32),
                   jax.ShapeDtypeStruct((nb, heads, dk, dv), F32)],
        scratch_shapes=[pltpu.VMEM((qk // LANE, SAMPLE_PAD * bb, LANE), F32),
                        pltpu.VMEM((qk // LANE, SAMPLE_PAD * bb, LANE), F32),
                        pltpu.VMEM((d // LANE, SAMPLE_PAD * bb, LANE), F32),
                        pltpu.VMEM((n_tok * bb, d), F32),
                        pltpu.VMEM((n_tok * bb, d), F32),
                        pltpu.VMEM((d // LANE, SAMPLE_PAD * bb, LANE), F32)],
        compiler_params=pltpu.CompilerParams(dimension_semantics=("arbitrary",),
                                             vmem_limit_bytes=VMEM_LIMIT),
        name="mixer_sample",
    )(z, g, sp, sg, gn, pw, ps, eye)


def _pack_kernel(a_ref, b_ref, lr_ref, wa_ref, wb_ref, wlr_ref, *, rank):
    wa_ref[...] = a_ref[...].T.astype(BF16)
    wb_ref[...] = b_ref[...].T.astype(BF16)
    lane = lax.broadcasted_iota(jnp.int32, wlr_ref.shape, 1)
    wlr_ref[...] = jnp.where(lane < rank, lr_ref[...].T, 0.0).astype(BF16)


def _pack_w_in(wt, split, rank):
    n, d = wt.shape
    rest = n - split - rank
    blk = 2 * GATE_BLOCK
    assert split % blk == 0 and rest % blk == 0 and split // blk == rest // blk and split % LANE == 0
    return pl.pallas_call(
        functools.partial(_pack_kernel, rank=rank),
        grid=(split // blk,),
        in_specs=[pl.BlockSpec((blk, d), lambda i: (i, 0)),
                  pl.BlockSpec((pl.Element(blk), pl.Element(d)), lambda i: (pl.multiple_of(split + rank + i * blk, SUBLANE), 0)),
                  pl.BlockSpec((LANE, d), lambda i: (split // LANE, 0))],
        out_specs=[pl.BlockSpec((d, blk), lambda i: (0, i)), pl.BlockSpec((d, blk), lambda i: (0, i)),
                   pl.BlockSpec((d, LANE), lambda i: (0, 0))],
        out_shape=[jax.ShapeDtypeStruct((d, split), BF16), jax.ShapeDtypeStruct((d, rest), BF16),
                   jax.ShapeDtypeStruct((d, LANE), BF16)],
        compiler_params=pltpu.CompilerParams(dimension_semantics=("arbitrary",)),
        name="pack_w_in",
    )(wt, wt, wt)


def _layer_weights(i, ffn1_norm, ffn1_w_gate, ffn1_w_up, ffn1_w_down, mix_norm, w_in, w_gk_up, b_gk,
                   gla_norm, pool_w, pool_scale, w_out, ffn2_norm, ffn2_w_gate, ffn2_w_up, ffn2_w_down,
                   ple_norm, w_ple_gate, w_ple_proj):
    d = w_in.shape[1]
    qk = w_gk_up.shape[2]
    rank = w_gk_up.shape[1]
    o_lr = 2 * d + 2 * qk
    wa, wb, wlr = _pack_w_in(jnp.swapaxes(w_in[i], 0, 1), o_lr, rank)
    wgk = jnp.concatenate([w_gk_up[i], jnp.zeros((LANE - rank, qk), F32)], axis=0).astype(BF16)
    row = lambda a: a[i].reshape(1, -1)
    return dict(
        ffn1=(row(ffn1_norm), ffn1_w_gate[i], ffn1_w_up[i], ffn1_w_down[i]),
        proj=(row(mix_norm), wa, wb, wlr, wgk, row(b_gk)),
        mix=(row(gla_norm), pool_w[i].astype(BF16), row(pool_scale)),
        tail=(w_out[i].astype(BF16), row(ffn2_norm), ffn2_w_gate[i], ffn2_w_up[i],
              ffn2_w_down[i].astype(BF16), row(ple_norm), w_ple_gate[i].astype(BF16), w_ple_proj[i].astype(BF16)),
    )


def kernel(x_prompt, x_sample, p_prompt, p_sample, state_pool, state_gla, ffn1_norm, ffn1_w_gate, ffn1_w_up, ffn1_w_down, mix_norm, w_in, w_gk_up, b_gk, gla_norm, pool_w, pool_scale, w_out, ffn2_norm, ffn2_w_gate, ffn2_w_up, ffn2_w_down, ple_norm, w_ple_gate, w_ple_proj, final_norm):
    depth = w_in.shape[0]
    batch, seq, d = x_prompt.shape
    nb, n_tok, _ = x_sample.shape
    assert depth == 1, "the final norm is fused into the tail kernel of the only layer"
    assert n_tok <= 4 and POOL_BUF <= HIST and seq % GLA_CHUNK == 0
    tri, mk = _gla_constants(GLA_CHUNK)
    tri = jnp.asarray(tri, BF16)
    mk = jnp.asarray(mk, F32)
    eye = jnp.eye(w_gk_up.shape[2] // GLA_HEADS, dtype=F32)
    fn = final_norm.reshape(1, -1)

    lw = _layer_weights(0, ffn1_norm, ffn1_w_gate, ffn1_w_up, ffn1_w_down, mix_norm, w_in, w_gk_up, b_gk,
                        gla_norm, pool_w, pool_scale, w_out, ffn2_norm, ffn2_w_gate, ffn2_w_up, ffn2_w_down,
                        ple_norm, w_ple_gate, w_ple_proj)
    tok_major = lambda a: jnp.swapaxes(a, 0, 1)
    ms, mp = n_tok * nb, batch * seq
    h1 = _ffn1(tok_major(x_sample).reshape(ms, d), x_prompt.reshape(mp, d), *lw["ffn1"])
    mix_p, pool_p, gla_p = _mixer(h1, ms, batch, seq, *lw["proj"], *lw["mix"], tri, mk, eye)
    z, g = _proj(h1, ms, *lw["proj"])
    mix_s, pool_s, gla_s = _sample_mixer(z.reshape(n_tok, nb, -1), g.reshape(n_tok, nb, -1),
                                         tok_major(state_pool[0]), state_gla[0], *lw["mix"], eye)
    y_sample, y_prompt = _tail(h1, mix_s.reshape(ms, d), mix_p, tok_major(p_sample[0]).reshape(ms, -1),
                               p_prompt[0].reshape(mp, -1), *lw["tail"], fn)
    return (y_prompt.reshape(batch, seq, d), tok_major(y_sample.reshape(n_tok, nb, d)),
            pool_p[None], gla_p[None], tok_major(pool_s)[None], gla_s[None])
```

```python
import functools

import numpy as np
import jax
import jax.numpy as jnp
from jax import lax
from jax.experimental import pallas as pl
from jax.experimental.pallas import tpu as pltpu

POOL_WINDOWS = (2, 4, 8, 16)
POOL_BUF = max(POOL_WINDOWS) - 1
GLA_HEADS = 4
GATE_NORMALIZER = 16.0
EPS = 1e-6
PAST_LEN = 16384

ROW_TILE = 512
GLA_CHUNK = 128
HIST = 16
POOL_PAD = 16
LANE = 128
SUBLANE = 8
SAMPLE_PAD = SUBLANE
SAMPLE_BLOCK = 16
GATE_BLOCK = 256
FFN_CHUNK = 6 * GATE_BLOCK
PROJ_BLOCK = 4 * GATE_BLOCK
LOG2_DECAY_SCALE = float(np.log2(np.e)) / GATE_NORMALIZER
VMEM_LIMIT = 56 * 1024 * 1024
TAIL_VMEM_LIMIT = 60 * 1024 * 1024

F32 = jnp.float32
BF16 = jnp.bfloat16


def _dot(a, b):
    return jnp.dot(a, b, preferred_element_type=F32)


def _dot_nt(a, b):
    return lax.dot_general(a, b, (((1,), (1,)), ((), ())), preferred_element_type=F32)


def _dot_tn(a, b):
    return lax.dot_general(a, b, (((0,), (0,)), ((), ())), preferred_element_type=F32)


def _rms(x, w):
    ms = jnp.mean(x * x, axis=-1, keepdims=True)
    return x * lax.rsqrt(ms + EPS) * w


def _sigmoid(x):
    return 0.5 * jnp.tanh(0.5 * x) + 0.5


def _silu(x):
    return x * _sigmoid(x)


def _log_sigmoid(x):
    return jnp.minimum(x, 0.0) - jnp.log1p(jnp.exp(-jnp.abs(x)))


def _ff_chunks(f):
    return [(lo, min(lo + FFN_CHUNK, f)) for lo in range(0, f, FFN_CHUNK)]


def _const_spec(shape):
    nd = len(shape)
    return pl.BlockSpec(shape, lambda *_: (0,) * nd, pipeline_mode=pl.Buffered(1))


def _swiglu_acc(xb, wg_ref, wu_ref, wd_ref):
    acc = None
    for lo, hi in _ff_chunks(wg_ref.shape[1]):
        g = _dot(xb, wg_ref[:, lo:hi])
        u = _dot(xb, wu_ref[:, lo:hi])
        a = (_silu(g) * u).astype(BF16)
        d = _dot(a, wd_ref[lo:hi, :])
        acc = d if acc is None else acc + d
    return acc


def _ffn1_kernel(xs_ref, xp_ref, nw_ref, wg_ref, wu_ref, wd_ref, o_ref, *, n_s):
    from_sample = pl.program_id(0) < n_s
    half = xs_ref.shape[0] // 2
    for r in (slice(0, half), slice(half, 2 * half)):
        x = jnp.where(from_sample, xs_ref[r, :], xp_ref[r, :])
        xb = _rms(x, nw_ref[...]).astype(BF16)
        o_ref[r, :] = x + 0.5 * _swiglu_acc(xb, wg_ref, wu_ref, wd_ref)


def _tile_counts(ms, mp):
    tm = ROW_TILE
    assert ms % tm == 0 and mp % tm == 0
    return tm, ms // tm, mp // tm


def _ffn1(xs, xp, nw, wg, wu, wd):
    (ms, d), mp = xs.shape, xp.shape[0]
    f = wg.shape[1]
    tm, n_s, n_p = _tile_counts(ms, mp)
    return pl.pallas_call(
        functools.partial(_ffn1_kernel, n_s=n_s),
        grid=(n_s + n_p,),
        in_specs=[pl.BlockSpec((tm, d), lambda i: (jnp.minimum(i, n_s - 1), 0)),
                  pl.BlockSpec((tm, d), lambda i: (jnp.maximum(i - n_s, 0), 0)),
                  _const_spec((1, d)), _const_spec((d, f)), _const_spec((d, f)), _const_spec((f, d))],
        out_specs=pl.BlockSpec((tm, d), lambda i: (i, 0)),
        out_shape=jax.ShapeDtypeStruct((ms + mp, d), F32),
        compiler_params=pltpu.CompilerParams(dimension_semantics=("arbitrary",),
                                             vmem_limit_bytes=VMEM_LIMIT),
        name="ffn1",
    )(xs, xp, nw, wg, wu, wd)


def _tail_kernel(h_ref, mixs_ref, mixp_ref, ps_ref, pp_ref, wo_ref, nw_ref, wg_ref, wu_ref, wd_ref,
                 pn_ref, wpg_ref, wpp_ref, fn_ref, ys_ref, yp_ref, *, n_s):
    from_sample = pl.program_id(0) < n_s
    mix = jnp.where(from_sample, mixs_ref[...], mixp_ref[...])
    p = jnp.where(from_sample, ps_ref[...], pp_ref[...])
    h = h_ref[...] + _dot(mix, wo_ref[...])
    xb = _rms(h, nw_ref[...]).astype(BF16)
    h = h + 0.5 * _swiglu_acc(xb, wg_ref, wu_ref, wd_ref)
    gate = _sigmoid(_dot(_rms(h, pn_ref[...]).astype(BF16), wpg_ref[...]))
    h = h + gate * _dot(p.astype(BF16), wpp_ref[...])
    yp_ref[...] = _rms(h, fn_ref[...])

    @pl.when(from_sample)
    def _():
        ys_ref[...] = yp_ref[...]


def _tail(h, mix_s, mix_p, p_s, p_p, wo, nw, wg, wu, wd, pn, wpg, wpp, fn):
    d = h.shape[1]
    ms, mp = mix_s.shape[0], mix_p.shape[0]
    f = wg.shape[1]
    pd = p_s.shape[1]
    tm, n_s, n_p = _tile_counts(ms, mp)
    sample = lambda w: pl.BlockSpec((tm, w), lambda i: (jnp.minimum(i, n_s - 1), 0))
    prompt = lambda w: pl.BlockSpec((tm, w), lambda i: (jnp.maximum(i - n_s, 0), 0))
    ys, yp = pl.pallas_call(
        functools.partial(_tail_kernel, n_s=n_s),
        grid=(n_s + n_p,),
        in_specs=[pl.BlockSpec((tm, d), lambda i: (i, 0)), sample(d), prompt(d), sample(pd), prompt(pd),
                  _const_spec((d, d)), _const_spec((1, d)),
                  _const_spec((d, f)), _const_spec((d, f)), _const_spec((f, d)),
                  _const_spec((1, d)), _const_spec((d, d)), _const_spec((pd, d)), _const_spec((1, d))],
        out_specs=[sample(d), prompt(d)],
        out_shape=[jax.ShapeDtypeStruct((ms, d), F32), jax.ShapeDtypeStruct((mp, d), F32)],
        compiler_params=pltpu.CompilerParams(dimension_semantics=("arbitrary",),
                                             vmem_limit_bytes=TAIL_VMEM_LIMIT),
        name="tail",
    )(h, mix_s, mix_p, p_s, p_p, wo, nw, wg, wu, wd, pn, wpg, wpp, fn)
    return ys, yp


def _gla_constants(c):
    idx = np.arange(c)
    i, t = idx[:, None], idx[None, :]
    masks = []
    s = 1
    while s < c:
        same = (i // (2 * s)) == (t // (2 * s))
        masks.append(same & ((i % (2 * s)) >= s) & ((t % (2 * s)) < s))
        s *= 2
    tri = (t <= i).astype(np.float32)
    return tri, np.stack([m.astype(np.float32) for m in masks], axis=0)


def _level_exponents(g, b, b_ref, r0):
    c, n = g.shape
    row = lax.broadcasted_iota(jnp.int32, (c, n), 0)
    m4 = row & 3
    g_prev = pltpu.roll(g, 1, 0)
    g_next = pltpu.roll(g, c - 1, 0)
    out = [jnp.where((row & 1) == 1, g, 0.0),
           jnp.where(m4 == 0, g_next, jnp.where(m4 == 1, 0.0, jnp.where(m4 == 2, g, g + g_prev)))]
    s = 4
    while s < c:
        blocks = [jnp.broadcast_to(b_ref[r0 + m * 2 * s + s - 1:r0 + m * 2 * s + s, :], (2 * s, n))
                  for m in range(c // (2 * s))]
        ref = blocks[0] if len(blocks) == 1 else jnp.concatenate(blocks, axis=0)
        sign = jnp.where((lax.broadcasted_iota(jnp.int32, (c, 1), 0) & s) != 0, 1.0, -1.0)
        out.append((b - ref) * sign)
        s *= 2
    return out


def _split_hi_lo(g):
    hi = g.astype(BF16)
    lo = (g - hi.astype(F32)).astype(BF16)
    return jnp.concatenate([hi, lo], axis=1)


def _column(row, eye):
    return jnp.sum(eye * row, axis=-1, keepdims=True)


def _mixer_kernel(h_ref, mn_ref, wa_ref, wb_ref, wlr_ref, wgk_ref, bgk_ref, gn_ref, pw_ref, ps_ref,
                  tri_ref, mk_ref, eye_ref,
                  mix_ref, npool_ref, ngla_ref,
                  xb_s, ext_s, tmp_s, q_s, k_s, v_s, g_s, o_s, a_s, b_s, qin_s, kdec_s, dcy_s, m_s, att_s, gate_s,
                  *, pos0):
    tm, d = h_ref.shape
    heads = GLA_HEADS
    qk = q_s.shape[1]
    dk, dv = qk // heads, d // heads
    c = GLA_CHUNK
    n_chunks = tm // c
    levels = mk_ref.shape[0]
    l = pl.program_id(1)
    x0 = POOL_PAD + HIST
    o_q, o_k, o_v = d, d + qk, d + 2 * qk

    @pl.when(l == 0)
    def _():
        ext_s[0:x0, :] = jnp.zeros((x0, d), F32)
        ngla_ref[...] = jnp.zeros(ngla_ref.shape, F32)

    xb_s[...] = _rms(h_ref[...], mn_ref[...]).astype(BF16)
    q_s[...] = _dot(xb_s[...], wa_ref[:, o_q:o_q + qk]) * (dk ** -0.5)
    k_s[...] = _dot(xb_s[...], wa_ref[:, o_k:o_k + qk])
    lr = _dot(xb_s[...], wlr_ref[...]).astype(BF16)
    g_s[...] = _log_sigmoid(_dot(lr, wgk_ref[...]) + bgk_ref[...]) * LOG2_DECAY_SCALE

    def u_block(lo):
        ext_s[x0:x0 + tm, lo:lo + GATE_BLOCK] = _dot(xb_s[...], wa_ref[:, lo:lo + GATE_BLOCK])

    def v_block(lo):
        v_s[:, lo:lo + GATE_BLOCK] = _dot(xb_s[...], wa_ref[:, o_v + lo:o_v + lo + GATE_BLOCK]).astype(BF16)

    def gate_block(lo):
        z = _dot(xb_s[...], wb_ref[:, lo:lo + GATE_BLOCK])
        gate_s[:, lo:lo + GATE_BLOCK] = _silu(z) if lo < d else _sigmoid(z)

    early_work = ([functools.partial(u_block, lo) for lo in range(0, d, GATE_BLOCK)]
                  + [functools.partial(v_block, lo) for lo in range(0, d, GATE_BLOCK)])
    early_split = np.array_split(np.arange(len(early_work)), n_chunks)
    late_work = [functools.partial(gate_block, lo) for lo in range(0, 3 * d, GATE_BLOCK)]
    late_split = np.array_split(np.arange(len(late_work)), 2 * n_chunks)

    def head_cols(hd):
        return slice(hd * dk, (hd + 1) * dk), slice(hd * dv, (hd + 1) * dv), slice(hd * c, (hd + 1) * c)

    row = lax.broadcasted_iota(jnp.int32, (c, qk), 0)
    for ci in range(n_chunks):
        for idx in early_split[ci]:
            early_work[int(idx)]()
        r0 = ci * c
        rows = slice(r0, r0 + c)
        g = g_s[rows, :]
        sums = _dot(tri_ref[...], _split_hi_lo(g))
        b = sums[:, :qk] + sums[:, qk:]
        b_s[rows, :] = b
        b_last = b[c - 1:c, :]
        q = q_s[rows, :]
        k = k_s[rows, :]
        qin_s[rows, :] = (q * jnp.exp2(b)).astype(BF16)
        kdec_s[rows, :] = (k * jnp.exp2(b_last - b)).astype(BF16)
        dcy_s[ci * SUBLANE:ci * SUBLANE + 1, :] = jnp.exp2(b_last)
        for lv, ex in enumerate(_level_exponents(g, b, b_s, r0)):
            operand = jnp.where((row & (1 << lv)) != 0, q, k) * jnp.exp2(ex)
            m_s[lv, rows, :] = operand.astype(BF16)

    pos = (pos0 + 1 + l * tm + lax.broadcasted_iota(jnp.int32, (tm, 1), 0))
    gc = d // len(POOL_WINDOWS)

    def pool_group(grp):
        w = POOL_WINDOWS[grp]
        cols = slice(grp * gc, (grp + 1) * gc)
        shifts = [1 << i for i in range(int(np.log2(w)))]
        starts = [x0] * len(shifts)
        for i in range(len(shifts) - 2, -1, -1):
            starts[i] = (starts[i + 1] - shifts[i + 1]) // SUBLANE * SUBLANE
        for i, (sh, st) in enumerate(zip(shifts, starts)):
            if i == 0:
                val = ext_s[st:x0 + tm, cols] + ext_s[st - sh:x0 + tm - sh, cols]
            else:
                val = tmp_s[i - 1, st:x0 + tm, :] + tmp_s[i - 1, st - sh:x0 + tm - sh, :]
            if i < len(shifts) - 1:
                tmp_s[i, st:x0 + tm, :] = val
        cur = ext_s[x0:x0 + tm, cols]
        inv = 1.0 / jnp.minimum(pos, w).astype(F32)
        pooled = (val * inv - cur).astype(BF16)
        a_s[:, cols] = _dot(pooled, pw_ref[grp]) * ps_ref[:, cols]

    pool_split = np.array_split(np.arange(len(POOL_WINDOWS)), n_chunks)
    for ci in range(n_chunks):
        for idx in late_split[ci]:
            late_work[int(idx)]()
        rows = slice(ci * c, (ci + 1) * c)
        for hd in range(heads):
            hc, _, ha = head_cols(hd)
            att = None
            for lv in range(levels):
                operand = m_s[lv, rows, hc]
                t = mk_ref[lv] * _dot_nt(operand, operand)
                att = t if att is None else att + t
            att_s[rows, ha] = att.astype(BF16)
        for grp in pool_split[ci]:
            pool_group(int(grp))

    for ci in range(n_chunks):
        for idx in late_split[n_chunks + ci]:
            late_work[int(idx)]()
        rows = slice(ci * c, (ci + 1) * c)
        qk_prod = q_s[rows, :] * k_s[rows, :]
        for hd in range(heads):
            hc, hv, ha = head_cols(hd)
            state = ngla_ref[0, hd]
            vh = v_s[rows, hv]
            lhs = jnp.concatenate([qin_s[rows, hc], att_s[rows, ha]], axis=1)
            rhs = jnp.concatenate([state.astype(BF16), vh], axis=0)
            diag = jnp.sum(qk_prod[:, hc], axis=-1, keepdims=True)
            o_s[rows, hv] = _dot(lhs, rhs) + diag * vh.astype(F32)
            decay = _column(dcy_s[ci * SUBLANE:ci * SUBLANE + 1, hc], eye_ref[...])
            ngla_ref[0, hd] = state * decay + _dot_tn(kdec_s[rows, hc], vh)

    gn = gn_ref[...]
    for hd in range(heads):
        hv = slice(hd * dv, (hd + 1) * dv)
        on = _rms(o_s[:, hv], gn)
        gate = gate_s[:, hd * dv:(hd + 1) * dv]
        ga = gate_s[:, d + hd * dv:d + (hd + 1) * dv]
        gb = gate_s[:, 2 * d + hd * dv:2 * d + (hd + 1) * dv]
        mix_ref[:, hv] = (ga * a_s[:, hv] + gb * (on * gate)).astype(BF16)

    @pl.when(l == pl.num_programs(1) - 1)
    def _():
        npool_ref[0] = ext_s[x0 + tm - POOL_BUF:x0 + tm, :]

    ext_s[POOL_PAD:x0, :] = ext_s[tm + POOL_PAD:tm + x0, :]


def _mixer(h, first_row, batch, seq, mn, wa, wb, wlr, wgk, bgk, gn, pw, ps, tri, mk, eye):
    d = h.shape[1]
    m = batch * seq
    tm = min(ROW_TILE, seq)
    nl = seq // tm
    qk = bgk.shape[1]
    heads = GLA_HEADS
    dk, dv = qk // heads, d // heads
    c = GLA_CHUNK
    levels = mk.shape[0]
    gc = d // len(POOL_WINDOWS)
    n_stage = int(np.log2(max(POOL_WINDOWS))) - 1
    consts = (mn, wa, wb, wlr, wgk, bgk, gn, pw, ps, tri, mk, eye)
    kern = functools.partial(_mixer_kernel, pos0=0)
    assert first_row % tm == 0
    tile0 = first_row // tm
    return pl.pallas_call(
        kern,
        grid=(batch, nl),
        in_specs=([pl.BlockSpec((tm, d), lambda b, l: (tile0 + b * nl + l, 0))]
                  + [_const_spec(a.shape) for a in consts]),
        out_specs=[pl.BlockSpec((tm, d), lambda b, l: (b * nl + l, 0)),
                   pl.BlockSpec((1, POOL_BUF, d), lambda b, l: (b, 0, 0)),
                   pl.BlockSpec((1, heads, dk, dv), lambda b, l: (b, 0, 0, 0))],
        out_shape=[jax.ShapeDtypeStruct((m, d), BF16),
                   jax.ShapeDtypeStruct((batch, POOL_BUF, d), F32),
                   jax.ShapeDtypeStruct((batch, heads, dk, dv), F32)],
        scratch_shapes=[pltpu.VMEM((tm, d), BF16),
                        pltpu.VMEM((POOL_PAD + HIST + tm, d), F32),
                        pltpu.VMEM((n_stage, POOL_PAD + HIST + tm, gc), F32),
                        pltpu.VMEM((tm, qk), F32),
                        pltpu.VMEM((tm, qk), F32),
                        pltpu.VMEM((tm, d), BF16),
                        pltpu.VMEM((tm, qk), F32),
                        pltpu.VMEM((tm, d), F32),
                        pltpu.VMEM((tm, d), F32),
                        pltpu.VMEM((tm, qk), F32),
                        pltpu.VMEM((tm, qk), BF16),
                        pltpu.VMEM((tm, qk), BF16),
                        pltpu.VMEM((tm // c * SUBLANE, qk), F32),
                        pltpu.VMEM((levels, tm, qk), BF16),
                        pltpu.VMEM((tm, heads * c), BF16),
                        pltpu.VMEM((tm, 3 * d), F32)],
        compiler_params=pltpu.CompilerParams(dimension_semantics=("arbitrary", "arbitrary"),
                                             vmem_limit_bytes=VMEM_LIMIT),
        name="mixer_prompt",
    )(h, *consts)


def _proj_kernel(h_ref, mn_ref, wa_ref, wb_ref, wlr_ref, wgk_ref, bgk_ref, z_ref, g_ref, xb_s, *, n_a):
    j = pl.program_id(0)

    @pl.when(j == 0)
    def _():
        xb_s[...] = _rms(h_ref[...], mn_ref[...]).astype(BF16)
        lr = _dot(xb_s[...], wlr_ref[...]).astype(BF16)
        g_ref[...] = _log_sigmoid(_dot(lr, wgk_ref[...]) + bgk_ref[...]) * LOG2_DECAY_SCALE

    @pl.when(j < n_a)
    def _():
        z_ref[...] = _dot(xb_s[...], wa_ref[...])

    @pl.when(j >= n_a)
    def _():
        z_ref[...] = _dot(xb_s[...], wb_ref[...])


def _proj(h, m, mn, wa, wb, wlr, wgk, bgk):
    d = h.shape[1]
    qk = bgk.shape[1]
    blk = PROJ_BLOCK
    assert wa.shape[1] % blk == 0 and wb.shape[1] % blk == 0
    n_a, n_b = wa.shape[1] // blk, wb.shape[1] // blk
    return pl.pallas_call(
        functools.partial(_proj_kernel, n_a=n_a),
        grid=(n_a + n_b,),
        in_specs=[pl.BlockSpec((m, d), lambda j: (0, 0)), _const_spec(mn.shape),
                  pl.BlockSpec((d, blk), lambda j: (0, jnp.minimum(j, n_a - 1))),
                  pl.BlockSpec((d, blk), lambda j: (0, jnp.maximum(j - n_a, 0))),
                  _const_spec(wlr.shape), _const_spec(wgk.shape), _const_spec(bgk.shape)],
        out_specs=[pl.BlockSpec((m, blk), lambda j: (0, j)), pl.BlockSpec((m, qk), lambda j: (0, 0))],
        out_shape=[jax.ShapeDtypeStruct((m, (n_a + n_b) * blk), F32), jax.ShapeDtypeStruct((m, qk), F32)],
        scratch_shapes=[pltpu.VMEM((m, d), BF16)],
        compiler_params=pltpu.CompilerParams(dimension_semantics=("arbitrary",),
                                             vmem_limit_bytes=VMEM_LIMIT),
        name="proj_sample",
    )(h, mn, wa, wb, wlr, wgk, bgk)


def _sample_mixer_kernel(z_ref, g_ref, sp_ref, sg_ref, gn_ref, pw_ref, ps_ref, eye_ref,
                         mix_ref, npool_ref, ngla_ref,
                         qin_s, kdec_s, v_s, o_s, pl_s, ost_s, *, pos0):
    n_tok, bb, _ = z_ref.shape
    d = o_s.shape[1]
    heads = GLA_HEADS
    qk = g_ref.shape[2]
    dk, dv = qk // heads, d // heads
    o_u, o_q, o_k, o_v = 0, d, d + qk, d + 2 * qk
    o_g = o_v + d
    o_ga, o_gb = o_g + d, o_g + 2 * d
    gc = d // len(POOL_WINDOWS)
    plane = lambda t: slice(t * bb, (t + 1) * bb)

    @pl.when(pl.program_id(0) == 0)
    def _():
        pad = (SAMPLE_PAD - n_tok) * bb
        qin_s[:, (n_tok + 1) * bb:, :] = jnp.zeros((qk // LANE, pad - bb, LANE), F32)
        kdec_s[:, n_tok * bb:, :] = jnp.zeros((qk // LANE, pad, LANE), F32)
        v_s[:, n_tok * bb:, :] = jnp.zeros((d // LANE, pad, LANE), F32)

    b = []
    for t in range(n_tok):
        b.append(g_ref[t] if t == 0 else b[-1] + g_ref[t])
    b_last = b[-1]
    q = [z_ref[t, :, o_q:o_q + qk] * (dk ** -0.5) for t in range(n_tok)]
    k = [z_ref[t, :, o_k:o_k + qk] for t in range(n_tok)]
    def stage(dst, t, val):
        for i in range(val.shape[1] // LANE):
            dst[i, plane(t), :] = val[:, i * LANE:(i + 1) * LANE]

    for t in range(n_tok):
        stage(qin_s, t, q[t] * jnp.exp2(b[t]))
        stage(kdec_s, t, k[t] * jnp.exp2(b_last - b[t]))
        stage(v_s, t, z_ref[t, :, o_v:o_v + d])
    stage(qin_s, n_tok, jnp.exp2(b_last))
    for t in range(n_tok):
        o_t = [None] * heads
        for j in range(t + 1):
            w_tj = q[t] * k[j] if j == t else q[t] * k[j] * jnp.exp2(b[t] - b[j])
            for hd in range(heads):
                a_tj = jnp.sum(w_tj[:, hd * dk:(hd + 1) * dk], axis=-1, keepdims=True)
                term = a_tj * z_ref[j, :, o_v + hd * dv:o_v + (hd + 1) * dv]
                o_t[hd] = term if o_t[hd] is None else o_t[hd] + term
        for hd in range(heads):
            o_s[plane(t), hd * dv:(hd + 1) * dv] = o_t[hd]

    def seq(n, carry):
        tokens = pl.ds(n, SAMPLE_PAD, stride=bb)
        own_rows = pl.ds(pl.multiple_of(n * SAMPLE_PAD, SAMPLE_PAD), SAMPLE_PAD)
        for hd in range(heads):
            state = sg_ref[n, hd]
            q_tok = qin_s[hd, tokens, :]
            o_state = _dot(q_tok.astype(BF16), state.astype(BF16))
            for i in range(dv // LANE):
                ost_s[hd * (dv // LANE) + i, own_rows, :] = o_state[:, i * LANE:(i + 1) * LANE]
            decay = _column(q_tok[n_tok:n_tok + 1, :], eye_ref[...])
            v_tok = jnp.concatenate([v_s[hd * (dv // LANE) + i, tokens, :] for i in range(dv // LANE)], axis=1)
            ngla_ref[n, hd] = state * decay + _dot_tn(kdec_s[hd, tokens, :].astype(BF16), v_tok.astype(BF16))
        return carry

    lax.fori_loop(0, bb, seq, 0, unroll=4)
    for t in range(n_tok):
        for i in range(d // LANE):
            o_s[plane(t), i * LANE:(i + 1) * LANE] += ost_s[i, pl.ds(t, bb, stride=SAMPLE_PAD), :]

    def pool_row(j):
        return z_ref[j, :, o_u:o_u + d] if j >= 0 else sp_ref[POOL_BUF + j]

    for t in range(n_tok):
        for grp, w in enumerate(POOL_WINDOWS):
            cols = slice(grp * gc, (grp + 1) * gc)
            s = pool_row(t)[:, cols]
            for dlt in range(1, w):
                s = s + pool_row(t - dlt)[:, cols]
            cnt = float(min(pos0 + 1 + t, w))
            pl_s[plane(t), cols] = s * (1.0 / cnt) - pool_row(t)[:, cols]
    for r in range(POOL_BUF):
        npool_ref[r] = pool_row(r + n_tok - POOL_BUF)

    rows = n_tok * bb
    gn = gn_ref[...]
    for hd in range(heads):
        hv = slice(hd * dv, (hd + 1) * dv)
        zcols = lambda off: z_ref[:, :, off + hd * dv:off + (hd + 1) * dv].reshape(rows, dv)
        a_out = _dot(pl_s[:, hv].astype(BF16), pw_ref[hd]) * ps_ref[:, hv]
        on = _rms(o_s[:, hv], gn)
        mix = _sigmoid(zcols(o_ga)) * a_out + _sigmoid(zcols(o_gb)) * (on * _silu(zcols(o_g)))
        mix_ref[:, :, hv] = mix.reshape(n_tok, bb, dv).astype(mix_ref.dtype)


def _sample_mixer(z, g, sp, sg, gn, pw, ps, eye):
    n_tok, nb, wide = z.shape
    qk = g.shape[2]
    d = sp.shape[2]
    heads = GLA_HEADS
    dk, dv = qk // heads, d // heads
    assert d // len(POOL_WINDOWS) == dv, "pool groups and attention heads share a column split here"
    assert n_tok < SAMPLE_PAD and dk == LANE and dv % LANE == 0
    bb = min(SAMPLE_BLOCK, nb)
    kern = functools.partial(_sample_mixer_kernel, pos0=PAST_LEN)
    blk3 = lambda a, b: pl.BlockSpec((a, bb, b), lambda i: (0, i, 0))
    state_spec = pl.BlockSpec((bb, heads, dk, dv), lambda i: (i, 0, 0, 0))
    return pl.pallas_call(
        kern,
        grid=(nb // bb,),
        in_specs=[blk3(n_tok, wide), blk3(n_tok, qk), blk3(POOL_BUF, d), state_spec,
                  _const_spec(gn.shape), _const_spec(pw.shape), _const_spec(ps.shape), _const_spec(eye.shape)],
        out_specs=[blk3(n_tok, d), blk3(POOL_BUF, d), state_spec],
        out_shape=[jax.ShapeDtypeStruct((n_tok, nb, d), BF16),
                   jax.ShapeDtypeStruct((POOL_BUF, nb, d), F32),
                   jax.ShapeDtypeStruct((nb, heads, dk, dv), F32)],
        scratch_shapes=[pltpu.VMEM((qk // LANE, SAMPLE_PAD * bb, LANE), F32),
                        pltpu.VMEM((qk // LANE, SAMPLE_PAD * bb, LANE), F32),
                        pltpu.VMEM((d // LANE, SAMPLE_PAD * bb, LANE), F32),
                        pltpu.VMEM((n_tok * bb, d), F32),
                        pltpu.VMEM((n_tok * bb, d), F32),
                        pltpu.VMEM((d // LANE, SAMPLE_PAD * bb, LANE), F32)],
        compiler_params=pltpu.CompilerParams(dimension_semantics=("arbitrary",),
                                             vmem_limit_bytes=VMEM_LIMIT),
        name="mixer_sample",
    )(z, g, sp, sg, gn, pw, ps, eye)


def _pack_kernel(a_ref, b_ref, lr_ref, wa_ref, wb_ref, wlr_ref, *, rank):
    wa_ref[...] = a_ref[...].T.astype(BF16)
    wb_ref[...] = b_ref[...].T.astype(BF16)
    lane = lax.broadcasted_iota(jnp.int32, wlr_ref.shape, 1)
    wlr_ref[...] = jnp.where(lane < rank, lr_ref[...].T, 0.0).astype(BF16)


def _pack_w_in(wt, split, rank):
    n, d = wt.shape
    rest = n - split - rank
    blk = 2 * GATE_BLOCK
    assert split % blk == 0 and rest % blk == 0 and split // blk == rest // blk and split % LANE == 0
    return pl.pallas_call(
        functools.partial(_pack_kernel, rank=rank),
        grid=(split // blk,),
        in_specs=[pl.BlockSpec((blk, d), lambda i: (i, 0)),
                  pl.BlockSpec((pl.Element(blk), pl.Element(d)), lambda i: (pl.multiple_of(split + rank + i * blk, SUBLANE), 0)),
                  pl.BlockSpec((LANE, d), lambda i: (split // LANE, 0))],
        out_specs=[pl.BlockSpec((d, blk), lambda i: (0, i)), pl.BlockSpec((d, blk), lambda i: (0, i)),
                   pl.BlockSpec((d, LANE), lambda i: (0, 0))],
        out_shape=[jax.ShapeDtypeStruct((d, split), BF16), jax.ShapeDtypeStruct((d, rest), BF16),
                   jax.ShapeDtypeStruct((d, LANE), BF16)],
        compiler_params=pltpu.CompilerParams(dimension_semantics=("arbitrary",)),
        name="pack_w_in",
    )(wt, wt, wt)


def _layer_weights(i, ffn1_norm, ffn1_w_gate, ffn1_w_up, ffn1_w_down, mix_norm, w_in, w_gk_up, b_gk,
                   gla_norm, pool_w, pool_scale, w_out, ffn2_norm, ffn2_w_gate, ffn2_w_up, ffn2_w_down,
                   ple_norm, w_ple_gate, w_ple_proj):
    d = w_in.shape[1]
    qk = w_gk_up.shape[2]
    rank = w_gk_up.shape[1]
    o_lr = 2 * d + 2 * qk
    wa, wb, wlr = _pack_w_in(jnp.swapaxes(w_in[i], 0, 1), o_lr, rank)
    wgk = jnp.concatenate([w_gk_up[i], jnp.zeros((LANE - rank, qk), F32)], axis=0).astype(BF16)
    row = lambda a: a[i].reshape(1, -1)
    return dict(
        ffn1=(row(ffn1_norm), ffn1_w_gate[i], ffn1_w_up[i], ffn1_w_down[i]),
        proj=(row(mix_norm), wa, wb, wlr, wgk, row(b_gk)),
        mix=(row(gla_norm), pool_w[i].astype(BF16), row(pool_scale)),
        tail=(w_out[i].astype(BF16), row(ffn2_norm), ffn2_w_gate[i], ffn2_w_up[i],
              ffn2_w_down[i].astype(BF16), row(ple_norm), w_ple_gate[i].astype(BF16), w_ple_proj[i].astype(BF16)),
    )


def kernel(x_prompt, x_sample, p_prompt, p_sample, state_pool, state_gla, ffn1_norm, ffn1_w_gate, ffn1_w_up, ffn1_w_down, mix_norm, w_in, w_gk_up, b_gk, gla_norm, pool_w, pool_scale, w_out, ffn2_norm, ffn2_w_gate, ffn2_w_up, ffn2_w_down, ple_norm, w_ple_gate, w_ple_proj, final_norm):
    depth = w_in.shape[0]
    batch, seq, d = x_prompt.shape
    nb, n_tok, _ = x_sample.shape
    assert depth == 1, "the final norm is fused into the tail kernel of the only layer"
    assert n_tok <= 4 and POOL_BUF <= HIST and seq % GLA_CHUNK == 0
    tri, mk = _gla_constants(GLA_CHUNK)
    tri = jnp.asarray(tri, BF16)
    mk = jnp.asarray(mk, F32)
    eye = jnp.eye(w_gk_up.shape[2] // GLA_HEADS, dtype=F32)
    fn = final_norm.reshape(1, -1)

    lw = _layer_weights(0, ffn1_norm, ffn1_w_gate, ffn1_w_up, ffn1_w_down, mix_norm, w_in, w_gk_up, b_gk,
                        gla_norm, pool_w, pool_scale, w_out, ffn2_norm, ffn2_w_gate, ffn2_w_up, ffn2_w_down,
                        ple_norm, w_ple_gate, w_ple_proj)
    tok_major = lambda a: jnp.swapaxes(a, 0, 1)
    ms, mp = n_tok * nb, batch * seq
    h1 = _ffn1(tok_major(x_sample).reshape(ms, d), x_prompt.reshape(mp, d), *lw["ffn1"])
    mix_p, pool_p, gla_p = _mixer(h1, ms, batch, seq, *lw["proj"], *lw["mix"], tri, mk, eye)
    z, g = _proj(h1, ms, *lw["proj"])
    mix_s, pool_s, gla_s = _sample_mixer(z.reshape(n_tok, nb, -1), g.reshape(n_tok, nb, -1),
                                         tok_major(state_pool[0]), state_gla[0], *lw["mix"], eye)
    y_sample, y_prompt = _tail(h1, mix_s.reshape(ms, d), mix_p, tok_major(p_sample[0]).reshape(ms, -1),
                               p_prompt[0].reshape(mp, -1), *lw["tail"], fn)
    return (y_prompt.reshape(batch, seq, d), tok_major(y_sample.reshape(n_tok, nb, d)),
            pool_p[None], gla_p[None], tok_major(pool_s)[None], gla_s[None])
```

```python
import functools

import numpy as np
import jax
import jax.numpy as jnp
from jax import lax
from jax.experimental import pallas as pl
from jax.experimental.pallas import tpu as pltpu

POOL_WINDOWS = (2, 4, 8, 16)
POOL_BUF = max(POOL_WINDOWS) - 1
GLA_HEADS = 4
GATE_NORMALIZER = 16.0
EPS = 1e-6
PAST_LEN = 16384

ROW_TILE = 512
GLA_CHUNK = 128
HIST = 16
POOL_PAD = 16
LANE = 128
SUBLANE = 8
SAMPLE_PAD = SUBLANE
SAMPLE_BLOCK = 16
GATE_BLOCK = 256
FFN_CHUNK = 6 * GATE_BLOCK
PROJ_BLOCK = 4 * GATE_BLOCK
LOG2_DECAY_SCALE = float(np.log2(np.e)) / GATE_NORMALIZER
VMEM_LIMIT = 56 * 1024 * 1024
TAIL_VMEM_LIMIT = 60 * 1024 * 1024

F32 = jnp.float32
BF16 = jnp.bfloat16


def _dot(a, b):
    return jnp.dot(a, b, preferred_element_type=F32)


def _dot_nt(a, b):
    return lax.dot_general(a, b, (((1,), (1,)), ((), ())), preferred_element_type=F32)


def _dot_tn(a, b):
    return lax.dot_general(a, b, (((0,), (0,)), ((), ())), preferred_element_type=F32)


def _rms(x, w):
    ms = jnp.mean(x * x, axis=-1, keepdims=True)
    return x * lax.rsqrt(ms + EPS) * w


def _sigmoid(x):
    return 0.5 * jnp.tanh(0.5 * x) + 0.5


def _silu(x):
    return x * _sigmoid(x)


def _log_sigmoid(x):
    return jnp.minimum(x, 0.0) - jnp.log1p(jnp.exp(-jnp.abs(x)))


def _ff_chunks(f):
    return [(lo, min(lo + FFN_CHUNK, f)) for lo in range(0, f, FFN_CHUNK)]


def _const_spec(shape):
    nd = len(shape)
    return pl.BlockSpec(shape, lambda *_: (0,) * nd, pipeline_mode=pl.Buffered(1))


def _swiglu_acc(xb, wg_ref, wu_ref, wd_ref):
    acc = None
    for lo, hi in _ff_chunks(wg_ref.shape[1]):
        g = _dot(xb, wg_ref[:, lo:hi])
        u = _dot(xb, wu_ref[:, lo:hi])
        a = (_silu(g) * u).astype(BF16)
        d = _dot(a, wd_ref[lo:hi, :])
        acc = d if acc is None else acc + d
    return acc


def _ffn1_kernel(xs_ref, xp_ref, nw_ref, wg_ref, wu_ref, wd_ref, o_ref, *, n_s):
    from_sample = pl.program_id(0) < n_s
    half = xs_ref.shape[0] // 2
    for r in (slice(0, half), slice(half, 2 * half)):
        x = jnp.where(from_sample, xs_ref[r, :], xp_ref[r, :])
        xb = _rms(x, nw_ref[...]).astype(BF16)
        o_ref[r, :] = x + 0.5 * _swiglu_acc(xb, wg_ref, wu_ref, wd_ref)


def _tile_counts(ms, mp):
    tm = ROW_TILE
    assert ms % tm == 0 and mp % tm == 0
    return tm, ms // tm, mp // tm


def _ffn1(xs, xp, nw, wg, wu, wd):
    (ms, d), mp = xs.shape, xp.shape[0]
    f = wg.shape[1]
    tm, n_s, n_p = _tile_counts(ms, mp)
    return pl.pallas_call(
        functools.partial(_ffn1_kernel, n_s=n_s),
        grid=(n_s + n_p,),
        in_specs=[pl.BlockSpec((tm, d), lambda i: (jnp.minimum(i, n_s - 1), 0)),
                  pl.BlockSpec((tm, d), lambda i: (jnp.maximum(i - n_s, 0), 0)),
                  _const_spec((1, d)), _const_spec((d, f)), _const_spec((d, f)), _const_spec((f, d))],
        out_specs=pl.BlockSpec((tm, d), lambda i: (i, 0)),
        out_shape=jax.ShapeDtypeStruct((ms + mp, d), F32),
        compiler_params=pltpu.CompilerParams(dimension_semantics=("arbitrary",),
                                             vmem_limit_bytes=VMEM_LIMIT),
        name="ffn1",
    )(xs, xp, nw, wg, wu, wd)


def _tail_kernel(h_ref, mixs_ref, mixp_ref, ps_ref, pp_ref, wo_ref, nw_ref, wg_ref, wu_ref, wd_ref,
                 pn_ref, wpg_ref, wpp_ref, fn_ref, ys_ref, yp_ref, *, n_s):
    from_sample = pl.program_id(0) < n_s
    mix = jnp.where(from_sample, mixs_ref[...], mixp_ref[...])
    p = jnp.where(from_sample, ps_ref[...], pp_ref[...])
    h = h_ref[...] + _dot(mix, wo_ref[...])
    xb = _rms(h, nw_ref[...]).astype(BF16)
    h = h + 0.5 * _swiglu_acc(xb, wg_ref, wu_ref, wd_ref)
    gate = _sigmoid(_dot(_rms(h, pn_ref[...]).astype(BF16), wpg_ref[...]))
    h = h + gate * _dot(p.astype(BF16), wpp_ref[...])
    yp_ref[...] = _rms(h, fn_ref[...])

    @pl.when(from_sample)
    def _():
        ys_ref[...] = yp_ref[...]


def _tail(h, mix_s, mix_p, p_s, p_p, wo, nw, wg, wu, wd, pn, wpg, wpp, fn):
    d = h.shape[1]
    ms, mp = mix_s.shape[0], mix_p.shape[0]
    f = wg.shape[1]
    pd = p_s.shape[1]
    tm, n_s, n_p = _tile_counts(ms, mp)
    sample = lambda w: pl.BlockSpec((tm, w), lambda i: (jnp.minimum(i, n_s - 1), 0))
    prompt = lambda w: pl.BlockSpec((tm, w), lambda i: (jnp.maximum(i - n_s, 0), 0))
    ys, yp = pl.pallas_call(
        functools.partial(_tail_kernel, n_s=n_s),
        grid=(n_s + n_p,),
        in_specs=[pl.BlockSpec((tm, d), lambda i: (i, 0)), sample(d), prompt(d), sample(pd), prompt(pd),
                  _const_spec((d, d)), _const_spec((1, d)),
                  _const_spec((d, f)), _const_spec((d, f)), _const_spec((f, d)),
                  _const_spec((1, d)), _const_spec((d, d)), _const_spec((pd, d)), _const_spec((1, d))],
        out_specs=[sample(d), prompt(d)],
        out_shape=[jax.ShapeDtypeStruct((ms, d), F32), jax.ShapeDtypeStruct((mp, d), F32)],
        compiler_params=pltpu.CompilerParams(dimension_semantics=("arbitrary",),
                                             vmem_limit_bytes=TAIL_VMEM_LIMIT),
        name="tail",
    )(h, mix_s, mix_p, p_s, p_p, wo, nw, wg, wu, wd, pn, wpg, wpp, fn)
    return ys, yp


def _gla_constants(c):
    idx = np.arange(c)
    i, t = idx[:, None], idx[None, :]
    masks = []
    s = 1
    while s < c:
        same = (i // (2 * s)) == (t // (2 * s))
        masks.append(same & ((i % (2 * s)) >= s) & ((t % (2 * s)) < s))
        s *= 2
    tri = (t <= i).astype(np.float32)
    return tri, np.stack([m.astype(np.float32) for m in masks], axis=0)


def _level_exponents(g, b, b_ref, r0):
    c, n = g.shape
    row = lax.broadcasted_iota(jnp.int32, (c, n), 0)
    m4 = row & 3
    g_prev = pltpu.roll(g, 1, 0)
    g_next = pltpu.roll(g, c - 1, 0)
    out = [jnp.where((row & 1) == 1, g, 0.0),
           jnp.where(m4 == 0, g_next, jnp.where(m4 == 1, 0.0, jnp.where(m4 == 2, g, g + g_prev)))]
    s = 4
    while s < c:
        blocks = [jnp.broadcast_to(b_ref[r0 + m * 2 * s + s - 1:r0 + m * 2 * s + s, :], (2 * s, n))
                  for m in range(c // (2 * s))]
        ref = blocks[0] if len(blocks) == 1 else jnp.concatenate(blocks, axis=0)
        sign = jnp.where((lax.broadcasted_iota(jnp.int32, (c, 1), 0) & s) != 0, 1.0, -1.0)
        out.append((b - ref) * sign)
        s *= 2
    return out


def _split_hi_lo(g):
    hi = g.astype(BF16)
    lo = (g - hi.astype(F32)).astype(BF16)
    return jnp.concatenate([hi, lo], axis=1)


def _column(row, eye):
    return jnp.sum(eye * row, axis=-1, keepdims=True)


def _mixer_kernel(h_ref, mn_ref, wa_ref, wb_ref, wlr_ref, wgk_ref, bgk_ref, gn_ref, pw_ref, ps_ref,
                  tri_ref, mk_ref, eye_ref,
                  mix_ref, npool_ref, ngla_ref,
                  xb_s, ext_s, tmp_s, q_s, k_s, v_s, g_s, o_s, a_s, b_s, qin_s, kdec_s, dcy_s, m_s, att_s, gate_s,
                  *, pos0):
    tm, d = h_ref.shape
    heads = GLA_HEADS
    qk = q_s.shape[1]
    dk, dv = qk // heads, d // heads
    c = GLA_CHUNK
    n_chunks = tm // c
    levels = mk_ref.shape[0]
    l = pl.program_id(1)
    x0 = POOL_PAD + HIST
    o_q, o_k, o_v = d, d + qk, d + 2 * qk

    @pl.when(l == 0)
    def _():
        ext_s[0:x0, :] = jnp.zeros((x0, d), F32)
        ngla_ref[...] = jnp.zeros(ngla_ref.shape, F32)

    xb_s[...] = _rms(h_ref[...], mn_ref[...]).astype(BF16)
    q_s[...] = _dot(xb_s[...], wa_ref[:, o_q:o_q + qk]) * (dk ** -0.5)
    k_s[...] = _dot(xb_s[...], wa_ref[:, o_k:o_k + qk])
    lr = _dot(xb_s[...], wlr_ref[...]).astype(BF16)
    g_s[...] = _log_sigmoid(_dot(lr, wgk_ref[...]) + bgk_ref[...]) * LOG2_DECAY_SCALE

    def u_block(lo):
        ext_s[x0:x0 + tm, lo:lo + GATE_BLOCK] = _dot(xb_s[...], wa_ref[:, lo:lo + GATE_BLOCK])

    def v_block(lo):
        v_s[:, lo:lo + GATE_BLOCK] = _dot(xb_s[...], wa_ref[:, o_v + lo:o_v + lo + GATE_BLOCK]).astype(BF16)

    def gate_block(lo):
        z = _dot(xb_s[...], wb_ref[:, lo:lo + GATE_BLOCK])
        gate_s[:, lo:lo + GATE_BLOCK] = _silu(z) if lo < d else _sigmoid(z)

    early_work = ([functools.partial(u_block, lo) for lo in range(0, d, GATE_BLOCK)]
                  + [functools.partial(v_block, lo) for lo in range(0, d, GATE_BLOCK)])
    early_split = np.array_split(np.arange(len(early_work)), n_chunks)
    late_work = [functools.partial(gate_block, lo) for lo in range(0, 3 * d, GATE_BLOCK)]
    late_split = np.array_split(np.arange(len(late_work)), 2 * n_chunks)

    def head_cols(hd):
        return slice(hd * dk, (hd + 1) * dk), slice(hd * dv, (hd + 1) * dv), slice(hd * c, (hd + 1) * c)

    row = lax.broadcasted_iota(jnp.int32, (c, qk), 0)
    for ci in range(n_chunks):
        for idx in early_split[ci]:
            early_work[int(idx)]()
        r0 = ci * c
        rows = slice(r0, r0 + c)
        g = g_s[rows, :]
        sums = _dot(tri_ref[...], _split_hi_lo(g))
        b = sums[:, :qk] + sums[:, qk:]
        b_s[rows, :] = b
        b_last = b[c - 1:c, :]
        q = q_s[rows, :]
        k = k_s[rows, :]
        qin_s[rows, :] = (q * jnp.exp2(b)).astype(BF16)
        kdec_s[rows, :] = (k * jnp.exp2(b_last - b)).astype(BF16)
        dcy_s[ci * SUBLANE:ci * SUBLANE + 1, :] = jnp.exp2(b_last)
        for lv, ex in enumerate(_level_exponents(g, b, b_s, r0)):
            operand = jnp.where((row & (1 << lv)) != 0, q, k) * jnp.exp2(ex)
            m_s[lv, rows, :] = operand.astype(BF16)

    pos = (pos0 + 1 + l * tm + lax.broadcasted_iota(jnp.int32, (tm, 1), 0))
    gc = d // len(POOL_WINDOWS)

    def pool_group(grp):
        w = POOL_WINDOWS[grp]
        cols = slice(grp * gc, (grp + 1) * gc)
        shifts = [1 << i for i in range(int(np.log2(w)))]
        starts = [x0] * len(shifts)
        for i in range(len(shifts) - 2, -1, -1):
            starts[i] = (starts[i + 1] - shifts[i + 1]) // SUBLANE * SUBLANE
        for i, (sh, st) in enumerate(zip(shifts, starts)):
            if i == 0:
                val = ext_s[st:x0 + tm, cols] + ext_s[st - sh:x0 + tm - sh, cols]
            else:
                val = tmp_s[i - 1, st:x0 + tm, :] + tmp_s[i - 1, st - sh:x0 + tm - sh, :]
            if i < len(shifts) - 1:
                tmp_s[i, st:x0 + tm, :] = val
        cur = ext_s[x0:x0 + tm, cols]
        inv = 1.0 / jnp.minimum(pos, w).astype(F32)
        pooled = (val * inv - cur).astype(BF16)
        a_s[:, cols] = _dot(pooled, pw_ref[grp]) * ps_ref[:, cols]

    pool_split = np.array_split(np.arange(len(POOL_WINDOWS)), n_chunks)
    for ci in range(n_chunks):
        for idx in late_split[ci]:
            late_work[int(idx)]()
        rows = slice(ci * c, (ci + 1) * c)
        for hd in range(heads):
            hc, _, ha = head_cols(hd)
            att = None
            for lv in range(levels):
                operand = m_s[lv, rows, hc]
                t = mk_ref[lv] * _dot_nt(operand, operand).astype(BF16)
                att = t if att is None else att + t
            att_s[rows, ha] = att
        for grp in pool_split[ci]:
            pool_group(int(grp))

    for ci in range(n_chunks):
        for idx in late_split[n_chunks + ci]:
            late_work[int(idx)]()
        rows = slice(ci * c, (ci + 1) * c)
        qk_prod = q_s[rows, :] * k_s[rows, :]
        for hd in range(heads):
            hc, hv, ha = head_cols(hd)
            state = ngla_ref[0, hd]
            vh = v_s[rows, hv]
            lhs = jnp.concatenate([qin_s[rows, hc], att_s[rows, ha]], axis=1)
            rhs = jnp.concatenate([state.astype(BF16), vh], axis=0)
            diag = jnp.sum(qk_prod[:, hc], axis=-1, keepdims=True)
            o_s[rows, hv] = _dot(lhs, rhs) + diag * vh.astype(F32)
            decay = _column(dcy_s[ci * SUBLANE:ci * SUBLANE + 1, hc], eye_ref[...])
            ngla_ref[0, hd] = state * decay + _dot_tn(kdec_s[rows, hc], vh)

    gn = gn_ref[...]
    for hd in range(heads):
        hv = slice(hd * dv, (hd + 1) * dv)
        on = _rms(o_s[:, hv], gn)
        gate = gate_s[:, hd * dv:(hd + 1) * dv]
        ga = gate_s[:, d + hd * dv:d + (hd + 1) * dv]
        gb = gate_s[:, 2 * d + hd * dv:2 * d + (hd + 1) * dv]
        mix_ref[:, hv] = (ga * a_s[:, hv] + gb * (on * gate)).astype(BF16)

    @pl.when(l == pl.num_programs(1) - 1)
    def _():
        npool_ref[0] = ext_s[x0 + tm - POOL_BUF:x0 + tm, :]

    ext_s[POOL_PAD:x0, :] = ext_s[tm + POOL_PAD:tm + x0, :]


def _mixer(h, first_row, batch, seq, mn, wa, wb, wlr, wgk, bgk, gn, pw, ps, tri, mk, eye):
    d = h.shape[1]
    m = batch * seq
    tm = min(ROW_TILE, seq)
    nl = seq // tm
    qk = bgk.shape[1]
    heads = GLA_HEADS
    dk, dv = qk // heads, d // heads
    c = GLA_CHUNK
    levels = mk.shape[0]
    gc = d // len(POOL_WINDOWS)
    n_stage = int(np.log2(max(POOL_WINDOWS))) - 1
    consts = (mn, wa, wb, wlr, wgk, bgk, gn, pw, ps, tri, mk, eye)
    kern = functools.partial(_mixer_kernel, pos0=0)
    assert first_row % tm == 0
    tile0 = first_row // tm
    return pl.pallas_call(
        kern,
        grid=(batch, nl),
        in_specs=([pl.BlockSpec((tm, d), lambda b, l: (tile0 + b * nl + l, 0))]
                  + [_const_spec(a.shape) for a in consts]),
        out_specs=[pl.BlockSpec((tm, d), lambda b, l: (b * nl + l, 0)),
                   pl.BlockSpec((1, POOL_BUF, d), lambda b, l: (b, 0, 0)),
                   pl.BlockSpec((1, heads, dk, dv), lambda b, l: (b, 0, 0, 0))],
        out_shape=[jax.ShapeDtypeStruct((m, d), BF16),
                   jax.ShapeDtypeStruct((batch, POOL_BUF, d), F32),
                   jax.ShapeDtypeStruct((batch, heads, dk, dv), F32)],
        scratch_shapes=[pltpu.VMEM((tm, d), BF16),
                        pltpu.VMEM((POOL_PAD + HIST + tm, d), F32),
                        pltpu.VMEM((n_stage, POOL_PAD + HIST + tm, gc), F32),
                        pltpu.VMEM((tm, qk), F32),
                        pltpu.VMEM((tm, qk), F32),
                        pltpu.VMEM((tm, d), BF16),
                        pltpu.VMEM((tm, qk), F32),
                        pltpu.VMEM((tm, d), F32),
                        pltpu.VMEM((tm, d), F32),
                        pltpu.VMEM((tm, qk), F32),
                        pltpu.VMEM((tm, qk), BF16),
                        pltpu.VMEM((tm, qk), BF16),
                        pltpu.VMEM((tm // c * SUBLANE, qk), F32),
                        pltpu.VMEM((levels, tm, qk), BF16),
                        pltpu.VMEM((tm, heads * c), BF16),
                        pltpu.VMEM((tm, 3 * d), F32)],
        compiler_params=pltpu.CompilerParams(dimension_semantics=("arbitrary", "arbitrary"),
                                             vmem_limit_bytes=VMEM_LIMIT),
        name="mixer_prompt",
    )(h, *consts)


def _proj_kernel(h_ref, mn_ref, wa_ref, wb_ref, wlr_ref, wgk_ref, bgk_ref, z_ref, g_ref, xb_s, *, n_a):
    j = pl.program_id(0)

    @pl.when(j == 0)
    def _():
        xb_s[...] = _rms(h_ref[...], mn_ref[...]).astype(BF16)
        lr = _dot(xb_s[...], wlr_ref[...]).astype(BF16)
        g_ref[...] = _log_sigmoid(_dot(lr, wgk_ref[...]) + bgk_ref[...]) * LOG2_DECAY_SCALE

    @pl.when(j < n_a)
    def _():
        z_ref[...] = _dot(xb_s[...], wa_ref[...])

    @pl.when(j >= n_a)
    def _():
        z_ref[...] = _dot(xb_s[...], wb_ref[...])


def _proj(h, m, mn, wa, wb, wlr, wgk, bgk):
    d = h.shape[1]
    qk = bgk.shape[1]
    blk = PROJ_BLOCK
    assert wa.shape[1] % blk == 0 and wb.shape[1] % blk == 0
    n_a, n_b = wa.shape[1] // blk, wb.shape[1] // blk
    return pl.pallas_call(
        functools.partial(_proj_kernel, n_a=n_a),
        grid=(n_a + n_b,),
        in_specs=[pl.BlockSpec((m, d), lambda j: (0, 0)), _const_spec(mn.shape),
                  pl.BlockSpec((d, blk), lambda j: (0, jnp.minimum(j, n_a - 1))),
                  pl.BlockSpec((d, blk), lambda j: (0, jnp.maximum(j - n_a, 0))),
                  _const_spec(wlr.shape), _const_spec(wgk.shape), _const_spec(bgk.shape)],
        out_specs=[pl.BlockSpec((m, blk), lambda j: (0, j)), pl.BlockSpec((m, qk), lambda j: (0, 0))],
        out_shape=[jax.ShapeDtypeStruct((m, (n_a + n_b) * blk), F32), jax.ShapeDtypeStruct((m, qk), F32)],
        scratch_shapes=[pltpu.VMEM((m, d), BF16)],
        compiler_params=pltpu.CompilerParams(dimension_semantics=("arbitrary",),
                                             vmem_limit_bytes=VMEM_LIMIT),
        name="proj_sample",
    )(h, mn, wa, wb, wlr, wgk, bgk)


def _sample_mixer_kernel(z_ref, g_ref, sp_ref, sg_ref, gn_ref, pw_ref, ps_ref, eye_ref,
                         mix_ref, npool_ref, ngla_ref,
                         qin_s, kdec_s, v_s, o_s, pl_s, ost_s, *, pos0):
    n_tok, bb, _ = z_ref.shape
    d = o_s.shape[1]
    heads = GLA_HEADS
    qk = g_ref.shape[2]
    dk, dv = qk // heads, d // heads
    o_u, o_q, o_k, o_v = 0, d, d + qk, d + 2 * qk
    o_g = o_v + d
    o_ga, o_gb = o_g + d, o_g + 2 * d
    gc = d // len(POOL_WINDOWS)
    plane = lambda t: slice(t * bb, (t + 1) * bb)

    @pl.when(pl.program_id(0) == 0)
    def _():
        pad = (SAMPLE_PAD - n_tok) * bb
        qin_s[:, (n_tok + 1) * bb:, :] = jnp.zeros((qk // LANE, pad - bb, LANE), F32)
        kdec_s[:, n_tok * bb:, :] = jnp.zeros((qk // LANE, pad, LANE), F32)
        v_s[:, n_tok * bb:, :] = jnp.zeros((d // LANE, pad, LANE), F32)

    b = []
    for t in range(n_tok):
        b.append(g_ref[t] if t == 0 else b[-1] + g_ref[t])
    b_last = b[-1]
    q = [z_ref[t, :, o_q:o_q + qk] * (dk ** -0.5) for t in range(n_tok)]
    k = [z_ref[t, :, o_k:o_k + qk] for t in range(n_tok)]
    def stage(dst, t, val):
        for i in range(val.shape[1] // LANE):
            dst[i, plane(t), :] = val[:, i * LANE:(i + 1) * LANE]

    for t in range(n_tok):
        stage(qin_s, t, q[t] * jnp.exp2(b[t]))
        stage(kdec_s, t, k[t] * jnp.exp2(b_last - b[t]))
        stage(v_s, t, z_ref[t, :, o_v:o_v + d])
    stage(qin_s, n_tok, jnp.exp2(b_last))
    for t in range(n_tok):
        o_t = [None] * heads
        for j in range(t + 1):
            w_tj = q[t] * k[j] if j == t else q[t] * k[j] * jnp.exp2(b[t] - b[j])
            for hd in range(heads):
                a_tj = jnp.sum(w_tj[:, hd * dk:(hd + 1) * dk], axis=-1, keepdims=True)
                term = a_tj * z_ref[j, :, o_v + hd * dv:o_v + (hd + 1) * dv]
                o_t[hd] = term if o_t[hd] is None else o_t[hd] + term
        for hd in range(heads):
            o_s[plane(t), hd * dv:(hd + 1) * dv] = o_t[hd]

    def seq(n, carry):
        tokens = pl.ds(n, SAMPLE_PAD, stride=bb)
        own_rows = pl.ds(pl.multiple_of(n * SAMPLE_PAD, SAMPLE_PAD), SAMPLE_PAD)
        for hd in range(heads):
            state = sg_ref[n, hd]
            q_tok = qin_s[hd, tokens, :]
            o_state = _dot(q_tok.astype(BF16), state.astype(BF16))
            for i in range(dv // LANE):
                ost_s[hd * (dv // LANE) + i, own_rows, :] = o_state[:, i * LANE:(i + 1) * LANE]
            decay = _column(q_tok[n_tok:n_tok + 1, :], eye_ref[...])
            v_tok = jnp.concatenate([v_s[hd * (dv // LANE) + i, tokens, :] for i in range(dv // LANE)], axis=1)
            ngla_ref[n, hd] = state * decay + _dot_tn(kdec_s[hd, tokens, :].astype(BF16), v_tok.astype(BF16))
        return carry

    lax.fori_loop(0, bb, seq, 0, unroll=4)
    for t in range(n_tok):
        for i in range(d // LANE):
            o_s[plane(t), i * LANE:(i + 1) * LANE] += ost_s[i, pl.ds(t, bb, stride=SAMPLE_PAD), :]

    def pool_row(j):
        return z_ref[j, :, o_u:o_u + d] if j >= 0 else sp_ref[POOL_BUF + j]

    for t in range(n_tok):
        for grp, w in enumerate(POOL_WINDOWS):
            cols = slice(grp * gc, (grp + 1) * gc)
            s = pool_row(t)[:, cols]
            for dlt in range(1, w):
                s = s + pool_row(t - dlt)[:, cols]
            cnt = float(min(pos0 + 1 + t, w))
            pl_s[plane(t), cols] = s * (1.0 / cnt) - pool_row(t)[:, cols]
    for r in range(POOL_BUF):
        npool_ref[r] = pool_row(r + n_tok - POOL_BUF)

    rows = n_tok * bb
    gn = gn_ref[...]
    for hd in range(heads):
        hv = slice(hd * dv, (hd + 1) * dv)
        zcols = lambda off: z_ref[:, :, off + hd * dv:off + (hd + 1) * dv].reshape(rows, dv)
        a_out = _dot(pl_s[:, hv].astype(BF16), pw_ref[hd]) * ps_ref[:, hv]
        on = _rms(o_s[:, hv], gn)
        mix = _sigmoid(zcols(o_ga)) * a_out + _sigmoid(zcols(o_gb)) * (on * _silu(zcols(o_g)))
        mix_ref[:, :, hv] = mix.reshape(n_tok, bb, dv).astype(mix_ref.dtype)


def _sample_mixer(z, g, sp, sg, gn, pw, ps, eye):
    n_tok, nb, wide = z.shape
    qk = g.shape[2]
    d = sp.shape[2]
    heads = GLA_HEADS
    dk, dv = qk // heads, d // heads
    assert d // len(POOL_WINDOWS) == dv, "pool groups and attention heads share a column split here"
    assert n_tok < SAMPLE_PAD and dk == LANE and dv % LANE == 0
    bb = min(SAMPLE_BLOCK, nb)
    kern = functools.partial(_sample_mixer_kernel, pos0=PAST_LEN)
    blk3 = lambda a, b: pl.BlockSpec((a, bb, b), lambda i: (0, i, 0))
    state_spec = pl.BlockSpec((bb, heads, dk, dv), lambda i: (i, 0, 0, 0))
    return pl.pallas_call(
        kern,
        grid=(nb // bb,),
        in_specs=[blk3(n_tok, wide), blk3(n_tok, qk), blk3(POOL_BUF, d), state_spec,
                  _const_spec(gn.shape), _const_spec(pw.shape), _const_spec(ps.shape), _const_spec(eye.shape)],
        out_specs=[blk3(n_tok, d), blk3(POOL_BUF, d), state_spec],
        out_shape=[jax.ShapeDtypeStruct((n_tok, nb, d), BF16),
                   jax.ShapeDtypeStruct((POOL_BUF, nb, d), F32),
                   jax.ShapeDtypeStruct((nb, heads, dk, dv), F32)],
        scratch_shapes=[pltpu.VMEM((qk // LANE, SAMPLE_PAD * bb, LANE), F32),
                        pltpu.VMEM((qk // LANE, SAMPLE_PAD * bb, LANE), F32),
                        pltpu.VMEM((d // LANE, SAMPLE_PAD * bb, LANE), F32),
                        pltpu.VMEM((n_tok * bb, d), F32),
                        pltpu.VMEM((n_tok * bb, d), F32),
                        pltpu.VMEM((d // LANE, SAMPLE_PAD * bb, LANE), F32)],
        compiler_params=pltpu.CompilerParams(dimension_semantics=("arbitrary",),
                                             vmem_limit_bytes=VMEM_LIMIT),
        name="mixer_sample",
    )(z, g, sp, sg, gn, pw, ps, eye)


def _pack_kernel(a_ref, b_ref, lr_ref, wa_ref, wb_ref, wlr_ref, *, rank):
    wa_ref[...] = a_ref[...].T.astype(BF16)
    wb_ref[...] = b_ref[...].T.astype(BF16)
    lane = lax.broadcasted_iota(jnp.int32, wlr_ref.shape, 1)
    wlr_ref[...] = jnp.where(lane < rank, lr_ref[...].T, 0.0).astype(BF16)


def _pack_w_in(wt, split, rank):
    n, d = wt.shape
    rest = n - split - rank
    blk = 2 * GATE_BLOCK
    assert split % blk == 0 and rest % blk == 0 and split // blk == rest // blk and split % LANE == 0
    return pl.pallas_call(
        functools.partial(_pack_kernel, rank=rank),
        grid=(split // blk,),
        in_specs=[pl.BlockSpec((blk, d), lambda i: (i, 0)),
                  pl.BlockSpec((pl.Element(blk), pl.Element(d)), lambda i: (pl.multiple_of(split + rank + i * blk, SUBLANE), 0)),
                  pl.BlockSpec((LANE, d), lambda i: (split // LANE, 0))],
        out_specs=[pl.BlockSpec((d, blk), lambda i: (0, i)), pl.BlockSpec((d, blk), lambda i: (0, i)),
                   pl.BlockSpec((d, LANE), lambda i: (0, 0))],
        out_shape=[jax.ShapeDtypeStruct((d, split), BF16), jax.ShapeDtypeStruct((d, rest), BF16),
                   jax.ShapeDtypeStruct((d, LANE), BF16)],
        compiler_params=pltpu.CompilerParams(dimension_semantics=("arbitrary",)),
        name="pack_w_in",
    )(wt, wt, wt)


def _layer_weights(i, ffn1_norm, ffn1_w_gate, ffn1_w_up, ffn1_w_down, mix_norm, w_in, w_gk_up, b_gk,
                   gla_norm, pool_w, pool_scale, w_out, ffn2_norm, ffn2_w_gate, ffn2_w_up, ffn2_w_down,
                   ple_norm, w_ple_gate, w_ple_proj):
    d = w_in.shape[1]
    qk = w_gk_up.shape[2]
    rank = w_gk_up.shape[1]
    o_lr = 2 * d + 2 * qk
    wa, wb, wlr = _pack_w_in(jnp.swapaxes(w_in[i], 0, 1), o_lr, rank)
    wgk = jnp.concatenate([w_gk_up[i], jnp.zeros((LANE - rank, qk), F32)], axis=0).astype(BF16)
    row = lambda a: a[i].reshape(1, -1)
    return dict(
        ffn1=(row(ffn1_norm), ffn1_w_gate[i], ffn1_w_up[i], ffn1_w_down[i]),
        proj=(row(mix_norm), wa, wb, wlr, wgk, row(b_gk)),
        mix=(row(gla_norm), pool_w[i].astype(BF16), row(pool_scale)),
        tail=(w_out[i].astype(BF16), row(ffn2_norm), ffn2_w_gate[i], ffn2_w_up[i],
              ffn2_w_down[i].astype(BF16), row(ple_norm), w_ple_gate[i].astype(BF16), w_ple_proj[i].astype(BF16)),
    )


def kernel(x_prompt, x_sample, p_prompt, p_sample, state_pool, state_gla, ffn1_norm, ffn1_w_gate, ffn1_w_up, ffn1_w_down, mix_norm, w_in, w_gk_up, b_gk, gla_norm, pool_w, pool_scale, w_out, ffn2_norm, ffn2_w_gate, ffn2_w_up, ffn2_w_down, ple_norm, w_ple_gate, w_ple_proj, final_norm):
    depth = w_in.shape[0]
    batch, seq, d = x_prompt.shape
    nb, n_tok, _ = x_sample.shape
    assert depth == 1, "the final norm is fused into the tail kernel of the only layer"
    assert n_tok <= 4 and POOL_BUF <= HIST and seq % GLA_CHUNK == 0
    tri, mk = _gla_constants(GLA_CHUNK)
    tri = jnp.asarray(tri, BF16)
    mk = jnp.asarray(mk, BF16)
    eye = jnp.eye(w_gk_up.shape[2] // GLA_HEADS, dtype=F32)
    fn = final_norm.reshape(1, -1)

    lw = _layer_weights(0, ffn1_norm, ffn1_w_gate, ffn1_w_up, ffn1_w_down, mix_norm, w_in, w_gk_up, b_gk,
                        gla_norm, pool_w, pool_scale, w_out, ffn2_norm, ffn2_w_gate, ffn2_w_up, ffn2_w_down,
                        ple_norm, w_ple_gate, w_ple_proj)
    tok_major = lambda a: jnp.swapaxes(a, 0, 1)
    ms, mp = n_tok * nb, batch * seq
    h1 = _ffn1(tok_major(x_sample).reshape(ms, d), x_prompt.reshape(mp, d), *lw["ffn1"])
    mix_p, pool_p, gla_p = _mixer(h1, ms, batch, seq, *lw["proj"], *lw["mix"], tri, mk, eye)
    z, g = _proj(h1, ms, *lw["proj"])
    mix_s, pool_s, gla_s = _sample_mixer(z.reshape(n_tok, nb, -1), g.reshape(n_tok, nb, -1),
                                         tok_major(state_pool[0]), state_gla[0], *lw["mix"], eye)
    y_sample, y_prompt = _tail(h1, mix_s.reshape(ms, d), mix_p, tok_major(p_sample[0]).reshape(ms, -1),
                               p_prompt[0].reshape(mp, -1), *lw["tail"], fn)
    return (y_prompt.reshape(batch, seq, d), tok_major(y_sample.reshape(n_tok, nb, d)),
            pool_p[None], gla_p[None], tok_major(pool_s)[None], gla_s[None])
```
